```python
import math
import numpy as np
import jax
import jax.numpy as jnp
from jax import lax

D_MODEL = 1024
BATCH = 4
SEQ = 4096
DEPTH = 2
DEC_BATCH = 32
DEC_SEQ = 8
PAST_LEN = 8192
PAGE_SIZE = 128

N_MIXERS = 2
N_HGRN_LAYERS = (DEPTH + N_MIXERS - 1) // N_MIXERS
N_ATTN_LAYERS = DEPTH // N_MIXERS
HG_EXPAND = 128
HG_HEADS = D_MODEL // HG_EXPAND
HG_DK = HG_EXPAND
HG_DV = D_MODEL // HG_HEADS
HG_WIDTH = HG_HEADS * HG_DK
HG_CHUNK = 64
GROUPS = ((128, 1), (512, 4), (2048, 16))
N_GROUPS = len(GROUPS)
ATT_HEAD_DIM = 64
ATT_HEADS = D_MODEL // ATT_HEAD_DIM
ATT_WIDTH = ATT_HEADS * ATT_HEAD_DIM
ATT_SCALE = ATT_HEAD_DIM ** -0.5
D_FF = 4 * D_MODEL
RMS_EPS = 1e-6

kernel_name = 'hybrid_hgrn2_dilated_swa_step'


def _rmsnorm(x, w):
    x32 = x.astype(jnp.float32)
    y = x32 * lax.rsqrt(jnp.mean(x32 * x32, axis=-1, keepdims=True) + RMS_EPS) * w.astype(jnp.float32)
    return y.astype(x.dtype)


def _alibi_slopes():
    n = N_GROUPS * ATT_HEADS
    s = 2.0 ** (-8.0 * np.arange(1, n + 1) / n)
    return jnp.asarray(s.reshape(N_GROUPS, ATT_HEADS), dtype=jnp.float32)


def _sqrelu_mlp(x, w_up, w_down):
    h = jax.nn.relu(x @ w_up)
    return (h * h) @ w_down


def _hgrn_recurrence(q, k, v, logf, s0):
    b_, t_, h_, _ = q.shape
    dv = v.shape[-1]
    c = math.gcd(t_, HG_CHUNK)
    n = t_ // c

    def chunks(a):
        return a.astype(jnp.float32).reshape(b_, n, c, h_, a.shape[-1]).transpose(1, 0, 3, 2, 4)

    causal = jnp.tril(jnp.ones((c, c), dtype=bool))

    def step(s, inp):
        qc, kc, vc, gc = inp
        cum = jnp.cumsum(gc, axis=2)
        diff = cum[:, :, :, None, :] - cum[:, :, None, :, :]
        decay = jnp.exp(jnp.where(causal[:, :, None], diff, -jnp.inf))
        attn = jnp.einsum('bhtk,bhsk,bhtsk->bhts', qc, kc, decay)
        o = jnp.einsum('bhts,bhsv->bhtv', attn, vc) + jnp.einsum('bhtk,bhkv->bhtv', qc * jnp.exp(cum), s)
        last = cum[:, :, -1:, :]
        s_new = jnp.exp(last[:, :, 0, :, None]) * s + jnp.einsum('bhsk,bhsv->bhkv', kc * jnp.exp(last - cum), vc)
        return s_new, o

    s_t, o = lax.scan(step, s0.astype(jnp.float32), (chunks(q), chunks(k), chunks(v), chunks(logf)))
    o = o.transpose(1, 0, 3, 2, 4).reshape(b_, t_, h_, dv)
    return o, s_t


def _hgrn_mixer(xn, s0, lb, w_q, w_f, w_i, w_g, w_o, w_gnorm):
    b_, t_, _ = xn.shape

    def heads(a):
        return a.reshape(b_, t_, HG_HEADS, -1)

    q = jax.nn.silu(xn @ w_q)
    f = lb + (1.0 - lb) * jax.nn.sigmoid((xn @ w_f).astype(jnp.float32))
    k = 1.0 - f
    i = xn @ w_i
    o, s_t = _hgrn_recurrence(heads(q), heads(k), heads(i), heads(jnp.log(f)), s0)
    o = _rmsnorm(o.reshape(b_, t_, HG_WIDTH), w_gnorm) * jax.nn.silu((xn @ w_g).astype(jnp.float32))
    return o.astype(xn.dtype) @ w_o, s_t


def _qkv(xn, w_qkv, w_qn, w_kn):
    b_, t_, _ = xn.shape
    qkv = (xn @ w_qkv).reshape(b_, t_, 3, N_GROUPS, ATT_HEADS, ATT_HEAD_DIM)
    q = _rmsnorm(qkv[:, :, 0], w_qn)
    k = _rmsnorm(qkv[:, :, 1], w_kn)
    return q, k, qkv[:, :, 2]


def _banded_window(q, k, v, n_back, slopes):
    n_, l_, h_, dh = q.shape
    bq = n_back
    nb = -(-l_ // bq)
    pad = nb * bq - l_
    qb = jnp.pad(q, ((0, 0), (0, pad), (0, 0), (0, 0))).reshape(n_, nb, bq, h_, dh)

    def key_blocks(a):
        ap = jnp.pad(a, ((0, 0), (bq, pad), (0, 0), (0, 0)))
        prev = ap[:, :nb * bq].reshape(n_, nb, bq, h_, dh)
        cur = ap[:, bq:].reshape(n_, nb, bq, h_, dh)
        return jnp.concatenate([prev, cur], axis=2)

    kb, vb = key_blocks(k), key_blocks(v)
    dist = jnp.arange(bq)[:, None] - jnp.arange(2 * bq)[None, :] + bq
    band = (dist >= 0) & (dist <= n_back)
    before_start = (jnp.arange(nb)[:, None, None] == 0) & (jnp.arange(2 * bq)[None, None, :] < bq)
    valid = band[None] & ~before_start
    s = jnp.einsum('nbqhd,nbkhd->nbhqk', qb, kb, preferred_element_type=jnp.float32) * ATT_SCALE
    s = s - slopes[:, None, None] * dist.astype(jnp.float32)
    s = jnp.where(valid[:, None], s, -jnp.inf)
    lse = jax.nn.logsumexp(s, axis=-1)
    p = jnp.exp(s - lse[..., None])
    o = jnp.einsum('nbhqk,nbkhd->nbqhd', p, vb.astype(jnp.float32))
    o = o.reshape(n_, nb * bq, h_, dh)[:, :l_]
    lse = lse.transpose(0, 1, 3, 2).reshape(n_, nb * bq, h_)[:, :l_]
    return o, lse


def _dilated_prompt(q, k, v, window, dilation, slopes):
    b_, t_, h_, dh = q.shape
    l_ = t_ // dilation

    def to_sub(a):
        return a.reshape(b_, l_, dilation, h_, dh).transpose(0, 2, 1, 3, 4).reshape(b_ * dilation, l_, h_, dh)

    o, lse = _banded_window(to_sub(q), to_sub(k), to_sub(v), window // dilation, slopes * dilation)
    o = o.reshape(b_, dilation, l_, h_, dh).transpose(0, 2, 1, 3, 4).reshape(b_, t_, h_, dh)
    lse = lse.reshape(b_, dilation, l_, h_).transpose(0, 2, 1, 3).reshape(b_, t_, h_)
    return o, lse


def _dilated_sample(q, k_all, v_all, window, dilation, slopes):
    s_new = q.shape[1]
    first = k_all.shape[1] - s_new
    steps = jnp.arange(window // dilation + 1)
    idx = first + jnp.arange(s_new)[:, None] - dilation * steps[None, :]
    valid = idx >= 0
    idx = jnp.maximum(idx, 0)
    kg = k_all[:, idx]
    vg = v_all[:, idx]
    s = jnp.einsum('bshd,bskhd->bhsk', q, kg, preferred_element_type=jnp.float32) * ATT_SCALE
    s = s - slopes[None, :, None, None] * (dilation * steps).astype(jnp.float32)
    s = jnp.where(valid[None, None], s, -jnp.inf)
    lse = jax.nn.logsumexp(s, axis=-1)
    p = jnp.exp(s - lse[..., None])
    o = jnp.einsum('bhsk,bskhd->bshd', p, vg.astype(jnp.float32))
    return o, lse.transpose(0, 2, 1)


def _merge_groups(outs, lses):
    wts = jax.nn.softmax(jnp.stack(lses, axis=0), axis=0)
    o = jnp.einsum('gbth,gbthd->bthd', wts, jnp.stack(outs, axis=0))
    return o.reshape(o.shape[0], o.shape[1], ATT_WIDTH)


def _attn_prompt(xn, w_qkv, w_o, w_qn, w_kn):
    q, k, v = _qkv(xn, w_qkv, w_qn, w_kn)
    slopes = _alibi_slopes()
    t_ = xn.shape[1]
    outs, lses, rows = [], [], []
    for g, (window, dilation) in enumerate(GROUPS):
        o, lse = _dilated_prompt(q[:, :, g], k[:, :, g], v[:, :, g], window, dilation, slopes[g])
        outs.append(o)
        lses.append(lse)
        keep = min(window, t_)
        rows.append(jnp.stack([k[:, t_ - keep:, g], v[:, t_ - keep:, g]], axis=2))
    return _merge_groups(outs, lses).astype(xn.dtype) @ w_o, rows


def _attn_sample(xn, bufs, w_qkv, w_o, w_qn, w_kn):
    q, k, v = _qkv(xn, w_qkv, w_qn, w_kn)
    slopes = _alibi_slopes()
    outs, lses, rows = [], [], []
    for g, (window, dilation) in enumerate(GROUPS):
        buf = bufs[g]
        k_all = jnp.concatenate([buf[:, :, 0].astype(k.dtype), k[:, :, g]], axis=1)
        v_all = jnp.concatenate([buf[:, :, 1].astype(v.dtype), v[:, :, g]], axis=1)
        o, lse = _dilated_sample(q[:, :, g], k_all, v_all, window, dilation, slopes[g])
        outs.append(o)
        lses.append(lse)
        rows.append(jnp.stack([k[:, :, g], v[:, :, g]], axis=2))
    return _merge_groups(outs, lses).astype(xn.dtype) @ w_o, rows


def setup_inputs(seed: int = 0) -> dict:
    key = jax.random.key(seed)
    ks = jax.random.split(key, 24)
    f32 = jnp.float32

    def dense(k, shape, fan_in):
        return jax.random.normal(k, shape, f32) * (fan_in ** -0.5)

    def gain(k, shape):
        return 1.0 + 0.05 * jax.random.normal(k, shape, f32)

    def kv_cache(k, window):
        return jax.random.normal(k, (N_ATTN_LAYERS, DEC_BATCH, min(window, PAST_LEN), 2, ATT_HEADS, ATT_HEAD_DIM), f32)

    return {
        'x_prompt': jax.random.normal(ks[0], (BATCH, SEQ, D_MODEL), f32),
        'x_sample': jax.random.normal(ks[1], (DEC_BATCH, DEC_SEQ, D_MODEL), f32),
        'state_hgrn': jax.random.normal(ks[2], (N_HGRN_LAYERS, DEC_BATCH, HG_HEADS, HG_DK, HG_DV), f32),
        'cache_kv_w128': kv_cache(ks[3], GROUPS[0][0]),
        'cache_kv_w512': kv_cache(ks[4], GROUPS[1][0]),
        'cache_kv_w2048': kv_cache(ks[5], GROUPS[2][0]),
        'hg_lb_logits': 0.1 * jax.random.normal(ks[6], (DEPTH + 1, HG_WIDTH), f32),
        'hg_w_q': dense(ks[7], (N_HGRN_LAYERS, D_MODEL, HG_WIDTH), D_MODEL),
        'hg_w_f': dense(ks[8], (N_HGRN_LAYERS, D_MODEL, HG_WIDTH), D_MODEL),
        'hg_w_i': dense(ks[9], (N_HGRN_LAYERS, D_MODEL, HG_HEADS * HG_DV), D_MODEL),
        'hg_w_g': dense(ks[10], (N_HGRN_LAYERS, D_MODEL, HG_HEADS * HG_DV), D_MODEL),
        'hg_w_o': dense(ks[11], (N_HGRN_LAYERS, HG_HEADS * HG_DV, D_MODEL), HG_HEADS * HG_DV),
        'hg_norm_o': gain(ks[12], (N_HGRN_LAYERS, HG_HEADS * HG_DV)),
        'att_w_qkv': dense(ks[13], (N_ATTN_LAYERS, D_MODEL, 3 * N_GROUPS * ATT_WIDTH), D_MODEL),
        'att_w_o': dense(ks[14], (N_ATTN_LAYERS, ATT_WIDTH, D_MODEL), ATT_WIDTH),
        'att_q_norm': gain(ks[15], (N_ATTN_LAYERS, ATT_HEAD_DIM)),
        'att_k_norm': gain(ks[16], (N_ATTN_LAYERS, ATT_HEAD_DIM)),
        'norm_mix': gain(ks[17], (DEPTH, D_MODEL)),
        'norm_ffn': gain(ks[18], (DEPTH, D_MODEL)),
        'ffn_w_up': dense(ks[19], (DEPTH, D_MODEL, D_FF), D_MODEL),
        'ffn_w_down': dense(ks[20], (DEPTH, D_FF, D_MODEL), D_FF),
    }


def reference(x_prompt, x_sample, state_hgrn, cache_kv_w128, cache_kv_w512, cache_kv_w2048,
              hg_lb_logits, hg_w_q, hg_w_f, hg_w_i, hg_w_g, hg_w_o, hg_norm_o,
              att_w_qkv, att_w_o, att_q_norm, att_k_norm,
              norm_mix, norm_ffn, ffn_w_up, ffn_w_down):
    lb_all = jnp.cumsum(jax.nn.softmax(hg_lb_logits.astype(jnp.float32), axis=0), axis=0)
    yp, ys = x_prompt, x_sample
    hg_p, hg_s = [], []
    kv_p = [[] for _ in GROUPS]
    kv_s = [[] for _ in GROUPS]
    for layer in range(DEPTH):
        a = layer // N_MIXERS
        xnp = _rmsnorm(yp, norm_mix[layer])
        xns = _rmsnorm(ys, norm_mix[layer])
        if layer % N_MIXERS == 0:
            lw = (lb_all[layer], hg_w_q[a], hg_w_f[a], hg_w_i[a], hg_w_g[a], hg_w_o[a], hg_norm_o[a])
            s0p = jnp.zeros((yp.shape[0], HG_HEADS, HG_DK, HG_DV), jnp.float32)
            mp, sp = _hgrn_mixer(xnp, s0p, *lw)
            ms, ss = _hgrn_mixer(xns, state_hgrn[a], *lw)
            hg_p.append(sp)
            hg_s.append(ss)
        else:
            aw = (att_w_qkv[a], att_w_o[a], att_q_norm[a], att_k_norm[a])
            mp, rows_p = _attn_prompt(xnp, *aw)
            ms, rows_s = _attn_sample(xns, (cache_kv_w128[a], cache_kv_w512[a], cache_kv_w2048[a]), *aw)
            for g in range(N_GROUPS):
                kv_p[g].append(rows_p[g])
                kv_s[g].append(rows_s[g])
        yp = yp + mp.astype(yp.dtype)
        ys = ys + ms.astype(ys.dtype)
        yp = yp + _sqrelu_mlp(_rmsnorm(yp, norm_ffn[layer]), ffn_w_up[layer], ffn_w_down[layer]).astype(yp.dtype)
        ys = ys + _sqrelu_mlp(_rmsnorm(ys, norm_ffn[layer]), ffn_w_up[layer], ffn_w_down[layer]).astype(ys.dtype)
    return (yp, ys, jnp.stack(hg_p), jnp.stack(hg_s),
            jnp.stack(kv_p[0]), jnp.stack(kv_s[0]),
            jnp.stack(kv_p[1]), jnp.stack(kv_s[1]),
            jnp.stack(kv_p[2]), jnp.stack(kv_s[2]))
```

```python
import functools

import numpy as np
import jax
import jax.numpy as jnp
from jax import lax
from jax.experimental import pallas as pl
from jax.experimental.pallas import tpu as pltpu

F32 = jnp.float32
BF16 = jnp.bfloat16

RMS_EPS = 1e-6
HG_HEADS = 8
HG_DK = 128
GROUPS = ((128, 1), (512, 4), (2048, 16))
N_GROUPS = len(GROUPS)
ATT_HEADS = 16
ATT_HEAD_DIM = 64
ATT_SCALE = ATT_HEAD_DIM ** -0.5
N_BACK = 128
NEG = -1e30

LANES = 128
MIB = 1024 * 1024


def _dot(a, b):
    return jnp.dot(a, b, preferred_element_type=F32)


def _dot_nt(a, b):
    return lax.dot_general(a, b, (((1,), (1,)), ((), ())), preferred_element_type=F32)


def _dot_tn(a, b):
    return lax.dot_general(a, b, (((0,), (0,)), ((), ())), preferred_element_type=F32)


def _split3(x):
    hi = x.astype(BF16)
    r1 = x - hi.astype(F32)
    mid = r1.astype(BF16)
    lo = (r1 - mid.astype(F32)).astype(BF16)
    return hi, mid, lo


def _rms_rows(x, w):
    ms = jnp.mean(x * x, axis=-1, keepdims=True)
    return x * lax.rsqrt(ms + RMS_EPS) * w


def _params(sem, vmem_mib):
    return pltpu.CompilerParams(dimension_semantics=sem, vmem_limit_bytes=vmem_mib * MIB)


def _hgrn_proj_kernel(layer, x_ref, nw_ref, lbl_ref, w_ref, o_ref, xn_ref):
    j = pl.program_id(1)

    @pl.when(j == 0)
    def _():
        xn_ref[...] = _rms_rows(x_ref[...], nw_ref[...]).astype(BF16)

    y = _dot(xn_ref[...], w_ref[0])

    @pl.when((j == 0) | (j == 3))
    def _():
        o_ref[0] = y * jax.nn.sigmoid(y)

    @pl.when(j == 1)
    def _():
        lg = lbl_ref[...]
        e = jnp.exp(lg - jnp.max(lg, axis=0, keepdims=True))
        lb = jnp.sum(e[:layer + 1], axis=0, keepdims=True) / jnp.sum(e, axis=0, keepdims=True)
        o_ref[0] = lb + (1.0 - lb) * jax.nn.sigmoid(y)

    @pl.when(j == 2)
    def _():
        o_ref[0] = y


def _hgrn_proj(x, nw, lb_logits, w4, layer, tm):
    m, d = x.shape
    return pl.pallas_call(
        functools.partial(_hgrn_proj_kernel, layer),
        out_shape=jax.ShapeDtypeStruct((4, m, d), F32),
        grid=(m // tm, 4),
        in_specs=[
            pl.BlockSpec((tm, d), lambda i, j: (i, 0)),
            pl.BlockSpec((1, d), lambda i, j: (0, 0)),
            pl.BlockSpec(lb_logits.shape, lambda i, j: (0, 0)),
            pl.BlockSpec((1, d, d), lambda i, j: (j, 0, 0)),
        ],
        out_specs=pl.BlockSpec((1, tm, d), lambda i, j: (j, i, 0)),
        scratch_shapes=[pltpu.VMEM((tm, d), BF16)],
        compiler_params=_params(("parallel", "arbitrary"), 40),
        name="hgrn_proj",
    )(x, nw, lb_logits, w4)


def _hgrn_rec_kernel(tc, c, q_ref, f_ref, v_ref, s0_ref, o_ref, so_ref, st_ref):
    t = pl.program_id(1)
    dk = HG_DK

    @pl.when(t == 0)
    def _():
        for h in range(HG_HEADS):
            st_ref[h] = s0_ref[0, h].T

    row = lax.broadcasted_iota(jnp.int32, (c, c), 0)
    col = lax.broadcasted_iota(jnp.int32, (c, c), 1)
    tril = jnp.where(row >= col, 1.0, 0.0).astype(BF16)
    rowl = lax.broadcasted_iota(jnp.int32, (c, dk), 0)
    n0 = min(c, 16)
    same_block = {}
    n = c // 2
    while n >= n0:
        shift = n.bit_length() - 1
        same_block[n] = (row >> shift) == (col >> shift)
        n //= 2

    def block_ref_rows(ch, n, pick):
        parts = [jnp.broadcast_to(ch[j * n + pick:j * n + pick + 1, :], (n, dk)) for j in range(c // n)]
        return parts[0] if len(parts) == 1 else jnp.concatenate(parts, axis=0)

    def do_chunk(qc, fc, vc, store):
        hi, mid, lo = _split3(jnp.log(fc))
        cum = _dot(tril, hi) + _dot(tril, mid) + _dot(tril, lo)
        for h in range(HG_HEADS):
            sl = slice(h * dk, (h + 1) * dk)
            qh, kh, vh, ch = qc[:, sl], 1.0 - fc[:, sl], vc[:, sl], cum[:, sl]
            s_t = st_ref[h]
            o = _dot_nt((qh * jnp.exp(ch)).astype(BF16), s_t.astype(BF16))
            attn = jnp.zeros((c, c), F32)
            n = c
            while n > n0:
                half = n // 2
                b = block_ref_rows(ch, n, half - 1)
                upper = (rowl & (n - 1)) >= half
                e = jnp.exp(jnp.where(upper, ch - b, b - ch))
                qt = jnp.where(upper, qh * e, 0.0).astype(BF16)
                kt = jnp.where(upper, 0.0, kh * e).astype(BF16)
                a = _dot_nt(qt, kt)
                if n < c:
                    a = jnp.where(same_block[n], a, 0.0)
                attn = attn + a
                n = half
            dq = ch - block_ref_rows(ch, n0, n0 // 2 - 1)
            a = _dot_nt((qh * jnp.exp(dq)).astype(BF16), (kh * jnp.exp(-dq)).astype(BF16))
            keep = col <= row
            if n0 < c:
                keep = keep & same_block[n0]
            attn = attn + jnp.where(keep, a, 0.0)
            o = o + _dot(attn.astype(BF16), vh.astype(BF16))
            store(sl, o)
            last = ch[c - 1:c, :]
            kd = kh * jnp.exp(last - ch)
            st_ref[h] = s_t * jnp.exp(last) + _dot_tn(vh.astype(BF16), kd.astype(BF16))

    if tc >= c:
        def body(ci, carry):
            rows = pl.ds(pl.multiple_of(ci * c, c), c)

            def store(sl, o):
                o_ref[rows, sl] = o

            do_chunk(q_ref[0, rows, :], f_ref[0, rows, :], v_ref[0, rows, :], store)
            return carry

        lax.fori_loop(0, tc // c, body, 0)
    else:
        pad = c - tc
        z = jnp.zeros((pad, q_ref.shape[-1]), F32)

        def store(sl, o):
            o_ref[:, sl] = o[:tc]

        do_chunk(jnp.concatenate([q_ref[0], z], axis=0),
                 jnp.concatenate([f_ref[0], z + 1.0], axis=0),
                 jnp.concatenate([v_ref[0], z], axis=0), store)

    @pl.when(t == pl.num_programs(1) - 1)
    def _():
        for h in range(HG_HEADS):
            so_ref[0, h] = st_ref[h].T


def _hgrn_rec(qfig, s0, b, t_len, tc, c):
    _, m, d = qfig.shape
    nt = t_len // tc
    h, dk, dv = s0.shape[1:]

    def row_spec(which):
        return pl.BlockSpec((1, tc, d), lambda bi, ti: (which, bi * nt + ti, 0))

    return pl.pallas_call(
        functools.partial(_hgrn_rec_kernel, tc, c),
        out_shape=(jax.ShapeDtypeStruct((m, d), F32), jax.ShapeDtypeStruct(s0.shape, F32)),
        grid=(b, nt),
        in_specs=[row_spec(0), row_spec(1), row_spec(2),
                  pl.BlockSpec((1, h, dk, dv), lambda bi, ti: (bi, 0, 0, 0))],
        out_specs=(pl.BlockSpec((tc, d), lambda bi, ti: (bi * nt + ti, 0)),
                   pl.BlockSpec((1, h, dk, dv), lambda bi, ti: (bi, 0, 0, 0))),
        scratch_shapes=[pltpu.VMEM((h, dv, dk), F32)],
        compiler_params=_params(("parallel", "arbitrary"), 48),
        name="hgrn_rec",
    )(qfig, qfig, qfig, s0)


def _hgrn_out_kernel(o_ref, g_ref, gn_ref, x_ref, w_ref, y_ref):
    a = _rms_rows(o_ref[...], gn_ref[...]) * g_ref[0]
    y_ref[...] = x_ref[...] + _dot(a.astype(BF16), w_ref[...])


def _hgrn_out(o, qfig, gn, x, w, tm):
    m, d = x.shape
    return pl.pallas_call(
        _hgrn_out_kernel,
        out_shape=jax.ShapeDtypeStruct((m, d), F32),
        grid=(m // tm,),
        in_specs=[
            pl.BlockSpec((tm, d), lambda i: (i, 0)),
            pl.BlockSpec((1, tm, d), lambda i: (3, i, 0)),
            pl.BlockSpec((1, d), lambda i: (0, 0)),
            pl.BlockSpec((tm, d), lambda i: (i, 0)),
            pl.BlockSpec((d, d), lambda i: (0, 0)),
        ],
        out_specs=pl.BlockSpec((tm, d), lambda i: (i, 0)),
        compiler_params=_params(("parallel",), 40),
        name="hgrn_out",
    )(o, qfig, gn, x, w)


def _ffn_kernel(x_ref, nw_ref, wu_ref, wd_ref, y_ref, xn_ref):
    j = pl.program_id(1)

    @pl.when(j == 0)
    def _():
        x = x_ref[...]
        xn_ref[...] = _rms_rows(x, nw_ref[...]).astype(BF16)
        y_ref[...] = x

    h = jnp.maximum(_dot(xn_ref[...], wu_ref[...]), 0.0)
    y_ref[...] += _dot((h * h).astype(BF16), wd_ref[...])


def _ffn(x, nw, wu, wd, tm, tf):
    m, d = x.shape
    ff = wu.shape[1]
    return pl.pallas_call(
        _ffn_kernel,
        out_shape=jax.ShapeDtypeStruct((m, d), F32),
        grid=(m // tm, ff // tf),
        in_specs=[
            pl.BlockSpec((tm, d), lambda i, j: (i, 0)),
            pl.BlockSpec((1, d), lambda i, j: (0, 0)),
            pl.BlockSpec((d, tf), lambda i, j: (0, j)),
            pl.BlockSpec((tf, d), lambda i, j: (j, 0)),
        ],
        out_specs=pl.BlockSpec((tm, d), lambda i, j: (i, 0)),
        scratch_shapes=[pltpu.VMEM((tm, d), BF16)],
        compiler_params=_params(("parallel", "arbitrary"), 48),
        name="ffn",
    )(x, nw, wu, wd)


def _qkv_kernel(x_ref, nw_ref, w_ref, qkn_ref, gsum_ref, gexp_ref, o16_ref, o32_ref, xn_ref):
    j = pl.program_id(1)

    @pl.when(j == 0)
    def _():
        xn_ref[...] = _rms_rows(x_ref[...], nw_ref[...]).astype(BF16)

    y = _dot(xn_ref[...], w_ref[...])

    @pl.when(j < 2 * N_GROUPS)
    def _():
        ss = _dot((y * y).astype(BF16), gsum_ref[...])
        r = lax.rsqrt(ss * (1.0 / ATT_HEAD_DIM) + RMS_EPS)
        hi, mid, lo = _split3(r)
        rx = _dot(hi, gexp_ref[...]) + _dot(mid, gexp_ref[...]) + _dot(lo, gexp_ref[...])
        yn = y * rx * qkn_ref[0]
        yn = jnp.where(j < N_GROUPS, yn * ATT_SCALE, yn)
        o16_ref[...] = yn.astype(BF16)
        o32_ref[...] = yn

    @pl.when(j >= 2 * N_GROUPS)
    def _():
        o16_ref[...] = y.astype(BF16)
        o32_ref[...] = y


def _qkv(x, nw, w, qkn, gsum, gexp, tm, first32):
    m, d = x.shape
    n = w.shape[1]
    nblk = n // d
    return pl.pallas_call(
        _qkv_kernel,
        out_shape=(jax.ShapeDtypeStruct((m, n), BF16),
                   jax.ShapeDtypeStruct((m, n - first32 * d), F32)),
        grid=(m // tm, nblk),
        in_specs=[
            pl.BlockSpec((tm, d), lambda i, j: (i, 0)),
            pl.BlockSpec((1, d), lambda i, j: (0, 0)),
            pl.BlockSpec((d, d), lambda i, j: (0, j)),
            pl.BlockSpec((1, 1, d), lambda i, j: (jnp.minimum(j // N_GROUPS, 1), 0, 0)),
            pl.BlockSpec(gsum.shape, lambda i, j: (0, 0)),
            pl.BlockSpec(gexp.shape, lambda i, j: (0, 0)),
        ],
        out_specs=(pl.BlockSpec((tm, d), lambda i, j: (i, j)),
                   pl.BlockSpec((tm, d), lambda i, j: (i, jnp.maximum(j - first32, 0)))),
        scratch_shapes=[pltpu.VMEM((tm, d), BF16)],
        compiler_params=_params(("parallel", "arbitrary"), 40),
        name="qkv",
    )(x, nw, w, qkn, gsum, gexp)


def _attn_prompt_kernel(qb, q_ref, kc_ref, kp_ref, vc_ref, vp_ref, bias_ref, o_ref, lse_ref):
    i = pl.program_id(2)
    first = jnp.where(i == 0, 0, 1)
    lane = lax.broadcasted_iota(jnp.int32, (N_BACK, LANES), 1)
    lo_half = lane < ATT_HEAD_DIM
    for u in range(qb):
        rows = slice(u * N_BACK, (u + 1) * N_BACK)
        lse_acc = jnp.zeros((N_BACK, LANES), F32)
        for pr in range(ATT_HEADS // 2):
            sl = slice(pr * LANES, (pr + 1) * LANES)
            qp = q_ref[0, rows, sl]
            if u == 0:
                kprev, vprev = kp_ref[0, :, sl], vp_ref[0, :, sl]
            else:
                prev = slice((u - 1) * N_BACK, u * N_BACK)
                kprev, vprev = kc_ref[0, prev, sl], vc_ref[0, prev, sl]
            kk = jnp.concatenate([kprev, kc_ref[0, rows, sl]], axis=0)
            vv = jnp.concatenate([vprev, vc_ref[0, rows, sl]], axis=0)
            outs = []
            for a in range(2):
                h = 2 * pr + a
                msk = lo_half if a == 0 else jnp.logical_not(lo_half)
                qa = jnp.where(msk, qp, jnp.zeros_like(qp))
                s = _dot_nt(qa, kk) + (bias_ref[first, h] if u == 0 else bias_ref[1, h])
                mx = jnp.max(s, axis=-1, keepdims=True)
                p = jnp.exp(s - mx)
                l = jnp.sum(p, axis=-1, keepdims=True)
                outs.append(_dot(p.astype(BF16), vv) * (1.0 / l))
                lse_acc = jnp.where(lane == h, mx + jnp.log(l), lse_acc)
            o_ref[0, rows, sl] = jnp.where(lo_half, outs[0], outs[1]).astype(BF16)
        lse_ref[0, rows, :] = lse_acc


def _attn_prompt(qkv16, bias, g, b, t_len, qb):
    _, dil = GROUPS[g]
    ncol = qkv16.shape[-1]
    nblk = ncol // (ATT_HEADS * ATT_HEAD_DIM)
    d = ATT_HEADS * ATT_HEAD_DIM
    l = t_len // dil
    x = qkv16.reshape(b, l, dil * ncol)
    rows = qb * N_BACK
    nb = l // rows

    def cur(which):
        return pl.BlockSpec((1, rows, d), lambda bi, r, i: (bi, i, r * nblk + which * N_GROUPS + g))

    def prev(which):
        return pl.BlockSpec((1, N_BACK, d),
                            lambda bi, r, i: (bi, jnp.maximum(i * qb - 1, 0), r * nblk + which * N_GROUPS + g))

    o, lse = pl.pallas_call(
        functools.partial(_attn_prompt_kernel, qb),
        out_shape=(jax.ShapeDtypeStruct((b, l, dil * d), BF16),
                   jax.ShapeDtypeStruct((b, l, dil * LANES), F32)),
        grid=(b, dil, nb),
        in_specs=[cur(0), cur(1), prev(1), cur(2), prev(2),
                  pl.BlockSpec(bias.shape, lambda bi, r, i: (0, 0, 0, 0))],
        out_specs=(pl.BlockSpec((1, rows, d), lambda bi, r, i: (bi, i, r)),
                   pl.BlockSpec((1, rows, LANES), lambda bi, r, i: (bi, i, r))),
        compiler_params=_params(("parallel", "parallel", "arbitrary"), 40),
        name="attn_prompt_g%d" % g,
    )(x, x, x, x, x, bias)
    return o.reshape(b * t_len, d), lse.reshape(b * t_len, LANES)


def _merge_out_kernel(o0_ref, o1_ref, o2_ref, l0_ref, l1_ref, l2_ref, gexp_ref, x_ref, w_ref, y_ref):
    lses = (l0_ref[...], l1_ref[...], l2_ref[...])
    outs = (o0_ref, o1_ref, o2_ref)
    mx = jnp.maximum(jnp.maximum(lses[0], lses[1]), lses[2])
    es = [jnp.exp(l - mx) for l in lses]
    inv = 1.0 / (es[0] + es[1] + es[2])
    acc = None
    for e, o_ref in zip(es, outs):
        hi, mid, lo = _split3(e * inv)
        wx = _dot(hi, gexp_ref[...]) + _dot(mid, gexp_ref[...]) + _dot(lo, gexp_ref[...])
        term = wx * o_ref[...].astype(F32)
        acc = term if acc is None else acc + term
    y_ref[...] = x_ref[...] + _dot(acc.astype(BF16), w_ref[...])


def _merge_out(outs, lses, gexp, x, w, tm):
    m, d = x.shape
    row = pl.BlockSpec((tm, d), lambda i: (i, 0))
    stat = pl.BlockSpec((tm, LANES), lambda i: (i, 0))
    return pl.pallas_call(
        _merge_out_kernel,
        out_shape=jax.ShapeDtypeStruct((m, d), F32),
        grid=(m // tm,),
        in_specs=[row, row, row, stat, stat, stat,
                  pl.BlockSpec(gexp.shape, lambda i: (0, 0)), row,
                  pl.BlockSpec((d, d), lambda i: (0, 0))],
        out_specs=row,
        compiler_params=_params(("parallel",), 40),
        name="merge_out",
    )(*outs, *lses, gexp, x, w)


def _attn_sample_kernel(ts, qkv_ref, c0_ref, c1_ref, c2_ref, bc_ref, bn_ref, o_ref):
    d = ATT_HEADS * ATT_HEAD_DIM
    caches = (c0_ref, c1_ref, c2_ref)
    rowh = lax.broadcasted_iota(jnp.int32, (ATT_HEADS, d), 0)
    laneh = lax.broadcasted_iota(jnp.int32, (ATT_HEADS, d), 1) >> (ATT_HEAD_DIM.bit_length() - 1)
    hmask = rowh == laneh
    zpad = jnp.zeros((N_BACK - ts, d), F32)
    knew, vnew = [], []
    for g in range(N_GROUPS):
        kcol = (N_GROUPS + g) * d
        vcol = (2 * N_GROUPS + g) * d
        knew.append(jnp.concatenate([qkv_ref[0, :, kcol:kcol + d], zpad], axis=0).astype(BF16))
        vnew.append(jnp.concatenate([qkv_ref[0, :, vcol:vcol + d], zpad], axis=0).astype(BF16))
    for s in range(ts):
        ms, ls, os_ = [], [], []
        for g in range(N_GROUPS):
            dil = GROUPS[g][1]
            r, nq = s % dil, s // dil
            qrow = qkv_ref[0, s:s + 1, g * d:(g + 1) * d]
            qrows = jnp.where(hmask, jnp.broadcast_to(qrow, (ATT_HEADS, d)), 0.0).astype(BF16)
            c_ref = caches[g]
            kc = c_ref[0, :, r * 2 * d:r * 2 * d + d].astype(BF16)
            vc = c_ref[0, :, r * 2 * d + d:(r + 1) * 2 * d].astype(BF16)
            sc = _dot_nt(qrows, kc) + bc_ref[g, nq]
            sn = _dot_nt(qrows, knew[g]) + bn_ref[g, s]
            mx = jnp.maximum(jnp.max(sc, axis=-1, keepdims=True), jnp.max(sn, axis=-1, keepdims=True))
            pc = jnp.exp(sc - mx)
            pn = jnp.exp(sn - mx)
            ms.append(mx)
            ls.append(jnp.sum(pc, axis=-1, keepdims=True) + jnp.sum(pn, axis=-1, keepdims=True))
            os_.append(_dot(pc.astype(BF16), vc) + _dot(pn.astype(BF16), vnew[g]))
        mx = jnp.maximum(jnp.maximum(ms[0], ms[1]), ms[2])
        es = [jnp.exp(m - mx) for m in ms]
        inv = 1.0 / (es[0] * ls[0] + es[1] * ls[1] + es[2] * ls[2])
        om = (es[0] * inv) * os_[0] + (es[1] * inv) * os_[1] + (es[2] * inv) * os_[2]
        o_ref[0, s:s + 1, :] = jnp.sum(jnp.where(hmask, om, 0.0), axis=0, keepdims=True)


def _attn_sample(qkv32, caches, bc, bn):
    b, ts, ncol = qkv32.shape
    d = ATT_HEADS * ATT_HEAD_DIM
    views, specs = [], []
    for g, (window, dil) in enumerate(GROUPS):
        c = caches[g]
        assert c.shape[1] == window, "window buffers are expected to be full"
        views.append(c.reshape(b, N_BACK, dil * 2 * d))
        used = min(dil, ts)
        specs.append(pl.BlockSpec((1, N_BACK, used * 2 * d), lambda bi: (bi, 0, 0)))
    return pl.pallas_call(
        functools.partial(_attn_sample_kernel, ts),
        out_shape=jax.ShapeDtypeStruct((b, ts, d), F32),
        grid=(b,),
        in_specs=[pl.BlockSpec((1, ts, ncol), lambda bi: (bi, 0, 0))] + specs + [
            pl.BlockSpec(bc.shape, lambda bi: (0, 0, 0, 0)),
            pl.BlockSpec(bn.shape, lambda bi: (0, 0, 0, 0))],
        out_specs=pl.BlockSpec((1, ts, d), lambda bi: (bi, 0, 0)),
        compiler_params=_params(("parallel",), 56),
        name="attn_sample",
    )(qkv32, *views, bc, bn)


def _matmul_res_kernel(a_ref, x_ref, w_ref, y_ref):
    y_ref[...] = x_ref[...] + _dot(a_ref[...].astype(BF16), w_ref[...])


def _matmul_res(a, x, w, tm):
    m, d = x.shape
    row = pl.BlockSpec((tm, d), lambda i: (i, 0))
    return pl.pallas_call(
        _matmul_res_kernel,
        out_shape=jax.ShapeDtypeStruct((m, d), F32),
        grid=(m // tm,),
        in_specs=[row, row, pl.BlockSpec((d, d), lambda i: (0, 0))],
        out_specs=row,
        compiler_params=_params(("parallel",), 40),
        name="matmul_res",
    )(a, x, w)


def _alibi_slopes():
    n = N_GROUPS * ATT_HEADS
    return (2.0 ** (-8.0 * np.arange(1, n + 1) / n)).reshape(N_GROUPS, ATT_HEADS)


def _prompt_bias(g):
    slopes = _alibi_slopes()[g] * GROUPS[g][1]
    dist = np.arange(N_BACK)[:, None] - np.arange(2 * N_BACK)[None, :] + N_BACK
    band = (dist >= 0) & (dist <= N_BACK)
    has_prev = np.arange(2 * N_BACK)[None, :] >= N_BACK
    out = np.empty((2, ATT_HEADS, N_BACK, 2 * N_BACK), np.float32)
    for k, valid in enumerate((band & has_prev, band)):
        out[k] = np.where(valid[None], -slopes[:, None, None] * dist[None], NEG)
    return jnp.asarray(out)


def _sample_bias(ts):
    slopes = _alibi_slopes()
    bc = np.full((N_GROUPS, ts, ATT_HEADS, N_BACK), NEG, np.float32)
    bn = np.full((N_GROUPS, ts, ATT_HEADS, N_BACK), NEG, np.float32)
    j = np.arange(N_BACK)
    for g, (_, dil) in enumerate(GROUPS):
        for nq in range(ts):
            steps = N_BACK + nq - j
            bc[g, nq] = np.where(j[None] >= nq, -slopes[g][:, None] * dil * steps[None], NEG)
        for s in range(ts):
            back = s - j
            valid = (back >= 0) & (back % dil == 0) & (j < ts)
            bn[g, s] = np.where(valid[None], -slopes[g][:, None] * back[None], NEG)
    return jnp.asarray(bc), jnp.asarray(bn)


def _head_group_matrices():
    d = ATT_HEADS * ATT_HEAD_DIM
    gsum = (np.arange(d)[:, None] // ATT_HEAD_DIM == np.arange(LANES)[None, :]).astype(np.float32)
    return jnp.asarray(gsum, BF16), jnp.asarray(gsum.T, BF16)


def _tile(m, want):
    return want if m % want == 0 else m


def kernel(x_prompt, x_sample, state_hgrn, cache_kv_w128, cache_kv_w512, cache_kv_w2048, hg_lb_logits, hg_w_q, hg_w_f, hg_w_i, hg_w_g, hg_w_o, hg_norm_o, att_w_qkv, att_w_o, att_q_norm, att_k_norm, norm_mix, norm_ffn, ffn_w_up, ffn_w_down):
    b, t_len, d = x_prompt.shape
    bs, ts, _ = x_sample.shape
    depth = norm_mix.shape[0]
    caches_all = (cache_kv_w128, cache_kv_w512, cache_kv_w2048)
    gsum, gexp = _head_group_matrices()
    bc, bn = _sample_bias(ts)

    yp = x_prompt.reshape(b * t_len, d)
    ys = x_sample.reshape(bs * ts, d)
    tm_p = _tile(b * t_len, 512)
    tm_s = _tile(bs * ts, 256)
    hg_p, hg_s = [], []
    kv_p = [[] for _ in GROUPS]
    kv_s = [[] for _ in GROUPS]
    for layer in range(depth):
        a = layer // 2
        nw = norm_mix[layer][None]
        if layer % 2 == 0:
            w4 = jnp.stack([hg_w_q[a], hg_w_f[a], hg_w_i[a], hg_w_g[a]]).astype(BF16)
            wo = hg_w_o[a].astype(BF16)
            gn = hg_norm_o[a][None]
            zeros = jnp.zeros((b,) + state_hgrn.shape[2:], F32)
            new = []
            for y, s0, nb_, tl, tm, tc in ((yp, zeros, b, t_len, tm_p, _tile(t_len, 512)),
                                            (ys, state_hgrn[a], bs, ts, tm_s, ts)):
                qfig = _hgrn_proj(y, nw, hg_lb_logits, w4, layer, tm)
                o, st = _hgrn_rec(qfig, s0, nb_, tl, tc, 128 if tc >= 128 else 16)
                new.append((_hgrn_out(o, qfig, gn, y, wo, tm), st))
            (yp, sp), (ys, ss) = new
            hg_p.append(sp)
            hg_s.append(ss)
        else:
            wqkv = att_w_qkv[a].astype(BF16)
            wo = att_w_o[a].astype(BF16)
            qkn = jnp.stack([jnp.tile(att_q_norm[a], ATT_HEADS), jnp.tile(att_k_norm[a], ATT_HEADS)])[:, None, :]
            qkv16, kv32 = _qkv(yp, nw, wqkv, qkn, gsum, gexp, tm_p, N_GROUPS)
            outs, lses = [], []
            for g, (window, dil) in enumerate(GROUPS):
                qb = 2 if (t_len // dil) % (2 * N_BACK) == 0 else 1
                o, lse = _attn_prompt(qkv16.reshape(b, t_len, -1), _prompt_bias(g), g, b, t_len, qb)
                outs.append(o)
                lses.append(lse)
                keep = min(window, t_len)
                kv = kv32.reshape(b, t_len, 2, N_GROUPS, ATT_HEADS, ATT_HEAD_DIM)
                kv_p[g].append(kv[:, t_len - keep:, :, g])
            yp = _merge_out(outs, lses, gexp, yp, wo, tm_p)
            _, qkv32 = _qkv(ys, nw, wqkv, qkn, gsum, gexp, tm_s, 0)
            qkv32 = qkv32.reshape(bs, ts, -1)
            om = _attn_sample(qkv32, [c[a] for c in caches_all], bc, bn)
            ys = _matmul_res(om.reshape(bs * ts, d), ys, wo, tm_s)
            kv = qkv32.reshape(bs, ts, 3, N_GROUPS, ATT_HEADS, ATT_HEAD_DIM)
            for g in range(N_GROUPS):
                kv_s[g].append(kv[:, :, 1:, g])
        nf = norm_ffn[layer][None]
        wu = ffn_w_up[layer].astype(BF16)
        wd = ffn_w_down[layer].astype(BF16)
        yp = _ffn(yp, nf, wu, wd, tm_p, 1024)
        ys = _ffn(ys, nf, wu, wd, tm_s, 1024)
    return (yp.reshape(b, t_len, d), ys.reshape(bs, ts, d), jnp.stack(hg_p), jnp.stack(hg_s),
            jnp.stack(kv_p[0]), jnp.stack(kv_s[0]),
            jnp.stack(kv_p[1]), jnp.stack(kv_s[1]),
            jnp.stack(kv_p[2]), jnp.stack(kv_s[2]))
```

```python
import functools

import numpy as np
import jax
import jax.numpy as jnp
from jax import lax
from jax.experimental import pallas as pl
from jax.experimental.pallas import tpu as pltpu

F32 = jnp.float32
BF16 = jnp.bfloat16

RMS_EPS = 1e-6
HG_HEADS = 8
HG_DK = 128
GROUPS = ((128, 1), (512, 4), (2048, 16))
N_GROUPS = len(GROUPS)
ATT_HEADS = 16
ATT_HEAD_DIM = 64
ATT_SCALE = ATT_HEAD_DIM ** -0.5
N_BACK = 128
NEG = -1e30

LANES = 128
MIB = 1024 * 1024


def _dot(a, b):
    return jnp.dot(a, b, preferred_element_type=F32)


def _dot_nt(a, b):
    return lax.dot_general(a, b, (((1,), (1,)), ((), ())), preferred_element_type=F32)


def _dot_tn(a, b):
    return lax.dot_general(a, b, (((0,), (0,)), ((), ())), preferred_element_type=F32)


def _split3(x):
    hi = x.astype(BF16)
    r1 = x - hi.astype(F32)
    mid = r1.astype(BF16)
    lo = (r1 - mid.astype(F32)).astype(BF16)
    return hi, mid, lo


def _rms_rows(x, w):
    ms = jnp.mean(x * x, axis=-1, keepdims=True)
    return x * lax.rsqrt(ms + RMS_EPS) * w


def _params(sem, vmem_mib):
    return pltpu.CompilerParams(dimension_semantics=sem, vmem_limit_bytes=vmem_mib * MIB)


def _hgrn_proj_kernel(layer, x_ref, nw_ref, lbl_ref, w_ref, o_ref, xn_ref):
    j = pl.program_id(1)

    @pl.when(j == 0)
    def _():
        xn_ref[...] = _rms_rows(x_ref[...], nw_ref[...]).astype(BF16)

    y = _dot(xn_ref[...], w_ref[0])

    @pl.when((j == 0) | (j == 3))
    def _():
        o_ref[0] = y * jax.nn.sigmoid(y)

    @pl.when(j == 1)
    def _():
        lg = lbl_ref[...]
        e = jnp.exp(lg - jnp.max(lg, axis=0, keepdims=True))
        lb = jnp.sum(e[:layer + 1], axis=0, keepdims=True) / jnp.sum(e, axis=0, keepdims=True)
        o_ref[0] = lb + (1.0 - lb) * jax.nn.sigmoid(y)

    @pl.when(j == 2)
    def _():
        o_ref[0] = y


def _hgrn_proj(x, nw, lb_logits, w4, layer, tm):
    m, d = x.shape
    return pl.pallas_call(
        functools.partial(_hgrn_proj_kernel, layer),
        out_shape=jax.ShapeDtypeStruct((4, m, d), F32),
        grid=(m // tm, 4),
        in_specs=[
            pl.BlockSpec((tm, d), lambda i, j: (i, 0)),
            pl.BlockSpec((1, d), lambda i, j: (0, 0)),
            pl.BlockSpec(lb_logits.shape, lambda i, j: (0, 0)),
            pl.BlockSpec((1, d, d), lambda i, j: (j, 0, 0)),
        ],
        out_specs=pl.BlockSpec((1, tm, d), lambda i, j: (j, i, 0)),
        scratch_shapes=[pltpu.VMEM((tm, d), BF16)],
        compiler_params=_params(("parallel", "arbitrary"), 40),
        name="hgrn_proj",
    )(x, nw, lb_logits, w4)


def _hgrn_rec_kernel(tc, c, q_ref, f_ref, v_ref, s0_ref, o_ref, so_ref, st_ref):
    t = pl.program_id(1)
    dk = HG_DK

    @pl.when(t == 0)
    def _():
        for h in range(HG_HEADS):
            st_ref[h] = s0_ref[0, h].T

    row = lax.broadcasted_iota(jnp.int32, (c, c), 0)
    col = lax.broadcasted_iota(jnp.int32, (c, c), 1)
    tril = jnp.where(row >= col, 1.0, 0.0).astype(BF16)
    rowl = lax.broadcasted_iota(jnp.int32, (c, dk), 0)
    n0 = min(c, 16)
    same_block = {}
    n = c // 2
    while n >= n0:
        shift = n.bit_length() - 1
        same_block[n] = (row >> shift) == (col >> shift)
        n //= 2

    def block_ref_rows(ch, n, pick):
        parts = [jnp.broadcast_to(ch[j * n + pick:j * n + pick + 1, :], (n, dk)) for j in range(c // n)]
        return parts[0] if len(parts) == 1 else jnp.concatenate(parts, axis=0)

    def do_chunk(qc, fc, vc, store):
        hi, mid, lo = _split3(jnp.log(fc))
        cum = _dot(tril, hi) + _dot(tril, mid) + _dot(tril, lo)
        for h in range(HG_HEADS):
            sl = slice(h * dk, (h + 1) * dk)
            qh, kh, vh, ch = qc[:, sl], 1.0 - fc[:, sl], vc[:, sl], cum[:, sl]
            s_t = st_ref[h]
            o = _dot_nt((qh * jnp.exp(ch)).astype(BF16), s_t.astype(BF16))
            attn = jnp.zeros((c, c), F32)
            n = c
            while n > n0:
                half = n // 2
                b = block_ref_rows(ch, n, half - 1)
                upper = (rowl & (n - 1)) >= half
                e = jnp.exp(jnp.where(upper, ch - b, b - ch))
                qt = jnp.where(upper, qh * e, 0.0).astype(BF16)
                kt = jnp.where(upper, 0.0, kh * e).astype(BF16)
                a = _dot_nt(qt, kt)
                if n < c:
                    a = jnp.where(same_block[n], a, 0.0)
                attn = attn + a
                n = half
            dq = ch - block_ref_rows(ch, n0, n0 // 2 - 1)
            a = _dot_nt((qh * jnp.exp(dq)).astype(BF16), (kh * jnp.exp(-dq)).astype(BF16))
            keep = col <= row
            if n0 < c:
                keep = keep & same_block[n0]
            attn = attn + jnp.where(keep, a, 0.0)
            o = o + _dot(attn.astype(BF16), vh.astype(BF16))
            store(sl, o)
            last = ch[c - 1:c, :]
            kd = kh * jnp.exp(last - ch)
            st_ref[h] = s_t * jnp.exp(last) + _dot_tn(vh.astype(BF16), kd.astype(BF16))

    if tc >= c:
        def body(ci, carry):
            rows = pl.ds(pl.multiple_of(ci * c, c), c)

            def store(sl, o):
                o_ref[rows, sl] = o

            do_chunk(q_ref[0, rows, :], f_ref[0, rows, :], v_ref[0, rows, :], store)
            return carry

        lax.fori_loop(0, tc // c, body, 0)
    else:
        pad = c - tc
        z = jnp.zeros((pad, q_ref.shape[-1]), F32)

        def store(sl, o):
            o_ref[:, sl] = o[:tc]

        do_chunk(jnp.concatenate([q_ref[0], z], axis=0),
                 jnp.concatenate([f_ref[0], z + 1.0], axis=0),
                 jnp.concatenate([v_ref[0], z], axis=0), store)

    @pl.when(t == pl.num_programs(1) - 1)
    def _():
        for h in range(HG_HEADS):
            so_ref[0, h] = st_ref[h].T


def _hgrn_rec(qfig, s0, b, t_len, tc, c):
    _, m, d = qfig.shape
    nt = t_len // tc
    h, dk, dv = s0.shape[1:]

    def row_spec(which):
        return pl.BlockSpec((1, tc, d), lambda bi, ti: (which, bi * nt + ti, 0))

    return pl.pallas_call(
        functools.partial(_hgrn_rec_kernel, tc, c),
        out_shape=(jax.ShapeDtypeStruct((m, d), F32), jax.ShapeDtypeStruct(s0.shape, F32)),
        grid=(b, nt),
        in_specs=[row_spec(0), row_spec(1), row_spec(2),
                  pl.BlockSpec((1, h, dk, dv), lambda bi, ti: (bi, 0, 0, 0))],
        out_specs=(pl.BlockSpec((tc, d), lambda bi, ti: (bi * nt + ti, 0)),
                   pl.BlockSpec((1, h, dk, dv), lambda bi, ti: (bi, 0, 0, 0))),
        scratch_shapes=[pltpu.VMEM((h, dv, dk), F32)],
        compiler_params=_params(("parallel", "arbitrary"), 48),
        name="hgrn_rec",
    )(qfig, qfig, qfig, s0)


def _hgrn_out_kernel(o_ref, g_ref, gn_ref, x_ref, w_ref, y_ref):
    a = _rms_rows(o_ref[...], gn_ref[...]) * g_ref[0]
    y_ref[...] = x_ref[...] + _dot(a.astype(BF16), w_ref[...])


def _hgrn_out(o, qfig, gn, x, w, tm):
    m, d = x.shape
    return pl.pallas_call(
        _hgrn_out_kernel,
        out_shape=jax.ShapeDtypeStruct((m, d), F32),
        grid=(m // tm,),
        in_specs=[
            pl.BlockSpec((tm, d), lambda i: (i, 0)),
            pl.BlockSpec((1, tm, d), lambda i: (3, i, 0)),
            pl.BlockSpec((1, d), lambda i: (0, 0)),
            pl.BlockSpec((tm, d), lambda i: (i, 0)),
            pl.BlockSpec((d, d), lambda i: (0, 0)),
        ],
        out_specs=pl.BlockSpec((tm, d), lambda i: (i, 0)),
        compiler_params=_params(("parallel",), 40),
        name="hgrn_out",
    )(o, qfig, gn, x, w)


def _ffn_kernel(x_ref, nw_ref, wu_ref, wd_ref, y_ref, xn_ref):
    j = pl.program_id(1)

    @pl.when(j == 0)
    def _():
        x = x_ref[...]
        xn_ref[...] = _rms_rows(x, nw_ref[...]).astype(BF16)
        y_ref[...] = x

    h = jnp.maximum(_dot(xn_ref[...], wu_ref[...]), 0.0)
    y_ref[...] += _dot((h * h).astype(BF16), wd_ref[...])


def _ffn(x, nw, wu, wd, tm, tf):
    m, d = x.shape
    ff = wu.shape[1]
    return pl.pallas_call(
        _ffn_kernel,
        out_shape=jax.ShapeDtypeStruct((m, d), F32),
        grid=(m // tm, ff // tf),
        in_specs=[
            pl.BlockSpec((tm, d), lambda i, j: (i, 0)),
            pl.BlockSpec((1, d), lambda i, j: (0, 0)),
            pl.BlockSpec((d, tf), lambda i, j: (0, j)),
            pl.BlockSpec((tf, d), lambda i, j: (j, 0)),
        ],
        out_specs=pl.BlockSpec((tm, d), lambda i, j: (i, 0)),
        scratch_shapes=[pltpu.VMEM((tm, d), BF16)],
        compiler_params=_params(("parallel", "arbitrary"), 48),
        name="ffn",
    )(x, nw, wu, wd)


def _qk_norm_rows(y, qkn, gsum, gexp):
    ss = _dot((y * y).astype(BF16), gsum)
    r = lax.rsqrt(ss * (1.0 / ATT_HEAD_DIM) + RMS_EPS)
    hi, mid, lo = _split3(r)
    rx = _dot(hi, gexp) + _dot(mid, gexp) + _dot(lo, gexp)
    return y * rx * qkn


def _qkv_perm_kernel(tm, x_ref, nw_ref, w_ref, qkn_ref, gsum_ref, gexp_ref,
                     o0_ref, o1_ref, o2_ref, xs_ref, xp_ref):
    j = pl.program_id(1)
    outs = (o0_ref, o1_ref, o2_ref)
    d = x_ref.shape[-1]
    nslab = d // LANES

    @pl.when(j == 0)
    def _():
        xn = _rms_rows(x_ref[...], nw_ref[...])
        xp_ref[0] = xn.astype(BF16)
        for c in range(nslab):
            xs_ref[c] = xn[:, c * LANES:(c + 1) * LANES]
        for g in range(1, N_GROUPS):
            dil = GROUPS[g][1]
            n = tm // dil
            for c in range(nslab):
                for r in range(dil):
                    xp_ref[g, r * n:(r + 1) * n, c * LANES:(c + 1) * LANES] = (
                        xs_ref.at[c][pl.ds(r, n, stride=dil), :].astype(BF16))

    grp = j // 3
    which = j % 3
    y = _dot(xp_ref[grp], w_ref[...])

    def emit(val):
        for g in range(N_GROUPS):
            dil = GROUPS[g][1]

            @pl.when(grp == g)
            def _():
                outs[g][0, 0] = val.astype(BF16).reshape(dil, tm // dil, d)

    @pl.when(which < 2)
    def _():
        yn = _qk_norm_rows(y, qkn_ref[0], gsum_ref[...], gexp_ref[...])
        emit(jnp.where(which == 0, yn * ATT_SCALE, yn))

    @pl.when(which == 2)
    def _():
        emit(y)


def _qkv_perm(x, nw, w, qkn, gsum, gexp, b, t_len, tm):
    m, d = x.shape
    nt = t_len // tm
    out_shapes, out_specs = [], []
    for g, (_, dil) in enumerate(GROUPS):
        out_shapes.append(jax.ShapeDtypeStruct((3, b, dil, t_len // dil, d), BF16))
        out_specs.append(pl.BlockSpec(
            (1, 1, dil, tm // dil, d),
            lambda i, j, g=g: (jnp.clip(j - 3 * g, 0, 2), i // nt, 0, i % nt, 0)))
    return pl.pallas_call(
        functools.partial(_qkv_perm_kernel, tm),
        out_shape=tuple(out_shapes),
        grid=(m // tm, 3 * N_GROUPS),
        in_specs=[
            pl.BlockSpec((tm, d), lambda i, j: (i, 0)),
            pl.BlockSpec((1, d), lambda i, j: (0, 0)),
            pl.BlockSpec((d, d), lambda i, j: (0, (j % 3) * N_GROUPS + j // 3)),
            pl.BlockSpec((1, 1, d), lambda i, j: (jnp.minimum(j % 3, 1), 0, 0)),
            pl.BlockSpec(gsum.shape, lambda i, j: (0, 0)),
            pl.BlockSpec(gexp.shape, lambda i, j: (0, 0)),
        ],
        out_specs=tuple(out_specs),
        scratch_shapes=[pltpu.VMEM((d // LANES, tm, LANES), F32),
                        pltpu.VMEM((N_GROUPS, tm, d), BF16)],
        compiler_params=_params(("parallel", "arbitrary"), 48),
        name="qkv_perm",
    )(x, nw, w, qkn, gsum, gexp)


def _qkv_plain_kernel(x_ref, nw_ref, w_ref, qkn_ref, gsum_ref, gexp_ref, o_ref, xn_ref):
    j = pl.program_id(1)

    @pl.when(j == 0)
    def _():
        xn_ref[...] = _rms_rows(x_ref[...], nw_ref[...]).astype(BF16)

    y = _dot(xn_ref[...], w_ref[...])

    @pl.when(j < 2 * N_GROUPS)
    def _():
        yn = _qk_norm_rows(y, qkn_ref[0], gsum_ref[...], gexp_ref[...])
        o_ref[...] = jnp.where(j < N_GROUPS, yn * ATT_SCALE, yn)

    @pl.when(j >= 2 * N_GROUPS)
    def _():
        o_ref[...] = y


def _qkv_plain(x, nw, w, qkn, gsum, gexp, tm):
    m, d = x.shape
    n = w.shape[1]
    return pl.pallas_call(
        _qkv_plain_kernel,
        out_shape=jax.ShapeDtypeStruct((m, n), F32),
        grid=(m // tm, n // d),
        in_specs=[
            pl.BlockSpec((tm, d), lambda i, j: (i, 0)),
            pl.BlockSpec((1, d), lambda i, j: (0, 0)),
            pl.BlockSpec((d, d), lambda i, j: (0, j)),
            pl.BlockSpec((1, 1, d), lambda i, j: (jnp.minimum(j // N_GROUPS, 1), 0, 0)),
            pl.BlockSpec(gsum.shape, lambda i, j: (0, 0)),
            pl.BlockSpec(gexp.shape, lambda i, j: (0, 0)),
        ],
        out_specs=pl.BlockSpec((tm, d), lambda i, j: (i, j)),
        scratch_shapes=[pltpu.VMEM((tm, d), BF16)],
        compiler_params=_params(("parallel", "arbitrary"), 40),
        name="qkv_plain",
    )(x, nw, w, qkn, gsum, gexp)


def _kv_tail_kernel(x_ref, nw_ref, wt_ref, kn_ref, o_ref):
    kv = pl.program_id(2)
    xn = _rms_rows(x_ref[...], nw_ref[...]).astype(BF16)
    yt = _dot_nt(wt_ref[0], xn)

    @pl.when(kv == 0)
    def _():
        y3 = yt.reshape(ATT_HEADS, ATT_HEAD_DIM, yt.shape[-1])
        ms = jnp.mean(y3 * y3, axis=1, keepdims=True)
        o_ref[0, 0] = (y3 * lax.rsqrt(ms + RMS_EPS)).reshape(yt.shape) * kn_ref[...]

    @pl.when(kv == 1)
    def _():
        o_ref[0, 0] = yt


def _kv_tail(x, nw, wt, kn_col, b, t_len, keep, tt):
    m, d = x.shape
    first = (t_len - keep) // tt
    per_seq = t_len // tt
    return pl.pallas_call(
        _kv_tail_kernel,
        out_shape=jax.ShapeDtypeStruct((b, 2, d, keep), F32),
        grid=(b, keep // tt, 2),
        in_specs=[
            pl.BlockSpec((tt, d), lambda bi, ti, kv: (bi * per_seq + first + ti, 0)),
            pl.BlockSpec((1, d), lambda bi, ti, kv: (0, 0)),
            pl.BlockSpec((1, d, d), lambda bi, ti, kv: (kv, 0, 0)),
            pl.BlockSpec((d, 1), lambda bi, ti, kv: (0, 0)),
        ],
        out_specs=pl.BlockSpec((1, 1, d, tt), lambda bi, ti, kv: (bi, kv, 0, ti)),
        compiler_params=_params(("parallel", "parallel", "arbitrary"), 40),
        name="kv_tail",
    )(x, nw, wt, kn_col)


def _attn_prompt_kernel(qb, q_ref, kc_ref, kp_ref, vc_ref, vp_ref, bias_ref, o_ref, lse_ref):
    i = pl.program_id(2)
    first = jnp.where(i == 0, 0, 1)
    lane = lax.broadcasted_iota(jnp.int32, (N_BACK, LANES), 1)
    lo_half = lane < ATT_HEAD_DIM
    for u in range(qb):
        rows = slice(u * N_BACK, (u + 1) * N_BACK)
        lse_acc = jnp.zeros((N_BACK, LANES), F32)
        for pr in range(ATT_HEADS // 2):
            sl = slice(pr * LANES, (pr + 1) * LANES)
            qp = q_ref[0, 0, 0, rows, sl]
            if u == 0:
                kprev, vprev = kp_ref[0, 0, 0, :, sl], vp_ref[0, 0, 0, :, sl]
            else:
                prev = slice((u - 1) * N_BACK, u * N_BACK)
                kprev, vprev = kc_ref[0, 0, 0, prev, sl], vc_ref[0, 0, 0, prev, sl]
            kk = jnp.concatenate([kprev, kc_ref[0, 0, 0, rows, sl]], axis=0)
            vv = jnp.concatenate([vprev, vc_ref[0, 0, 0, rows, sl]], axis=0)
            outs = []
            for a in range(2):
                h = 2 * pr + a
                msk = lo_half if a == 0 else jnp.logical_not(lo_half)
                qa = jnp.where(msk, qp, jnp.zeros_like(qp))
                s = _dot_nt(qa, kk) + (bias_ref[first, h] if u == 0 else bias_ref[1, h])
                mx = jnp.max(s, axis=-1, keepdims=True)
                p = jnp.exp(s - mx)
                l = jnp.sum(p, axis=-1, keepdims=True)
                outs.append(_dot(p.astype(BF16), vv) * (1.0 / l))
                lse_acc = jnp.where(lane == h, mx + jnp.log(l), lse_acc)
            o_ref[0, 0, rows, sl] = jnp.where(lo_half, outs[0], outs[1]).astype(BF16)
        lse_ref[0, 0, rows, :] = lse_acc


def _attn_prompt(qkv_g, bias, qb):
    _, b, dil, l, d = qkv_g.shape
    rows = qb * N_BACK

    def cur(which):
        return pl.BlockSpec((1, 1, 1, rows, d), lambda bi, r, i: (which, bi, r, i, 0))

    def prev(which):
        return pl.BlockSpec((1, 1, 1, N_BACK, d),
                            lambda bi, r, i: (which, bi, r, jnp.maximum(i * qb - 1, 0), 0))

    return pl.pallas_call(
        functools.partial(_attn_prompt_kernel, qb),
        out_shape=(jax.ShapeDtypeStruct((b, dil, l, d), BF16),
                   jax.ShapeDtypeStruct((b, dil, l, LANES), F32)),
        grid=(b, dil, l // rows),
        in_specs=[cur(0), cur(1), prev(1), cur(2), prev(2),
                  pl.BlockSpec(bias.shape, lambda bi, r, i: (0, 0, 0, 0))],
        out_specs=(pl.BlockSpec((1, 1, rows, d), lambda bi, r, i: (bi, r, i, 0)),
                   pl.BlockSpec((1, 1, rows, LANES), lambda bi, r, i: (bi, r, i, 0))),
        compiler_params=_params(("parallel", "parallel", "arbitrary"), 40),
        name="attn_prompt_d%d" % dil,
    )(qkv_g, qkv_g, qkv_g, qkv_g, qkv_g, bias)


def _merge_out_kernel(tm, o0_ref, o1_ref, o2_ref, l0_ref, l1_ref, l2_ref, gexp_ref, x_ref, w_ref,
                      y_ref, os_ref, ls_ref):
    d = x_ref.shape[-1]
    nslab = d // LANES

    def natural(o_ref, l_ref, dil):
        if dil == 1:
            return o_ref[0, 0].astype(F32), l_ref[0, 0]
        n = tm // dil
        for r in range(dil):
            ls_ref[pl.ds(r, n, stride=dil), :] = l_ref[0, r]
            for c in range(nslab):
                os_ref.at[c][pl.ds(r, n, stride=dil), :] = o_ref[0, r, :, c * LANES:(c + 1) * LANES].astype(F32)
        return jnp.concatenate([os_ref[c] for c in range(nslab)], axis=1), ls_ref[...]

    pairs = [natural(o_ref, l_ref, GROUPS[g][1])
             for g, (o_ref, l_ref) in enumerate(((o0_ref, l0_ref), (o1_ref, l1_ref), (o2_ref, l2_ref)))]
    lses = [p[1] for p in pairs]
    mx = jnp.maximum(jnp.maximum(lses[0], lses[1]), lses[2])
    es = [jnp.exp(l - mx) for l in lses]
    inv = 1.0 / (es[0] + es[1] + es[2])
    acc = None
    for e, (o, _) in zip(es, pairs):
        hi, mid, lo = _split3(e * inv)
        wx = _dot(hi, gexp_ref[...]) + _dot(mid, gexp_ref[...]) + _dot(lo, gexp_ref[...])
        acc = wx * o if acc is None else acc + wx * o
    y_ref[...] = x_ref[...] + _dot(acc.astype(BF16), w_ref[...])


def _merge_out(outs, lses, gexp, x, w, t_len, tm):
    m, d = x.shape
    nt = t_len // tm
    row = pl.BlockSpec((tm, d), lambda i: (i, 0))
    o_specs, l_specs = [], []
    for _, dil in GROUPS:
        o_specs.append(pl.BlockSpec((1, dil, tm // dil, d), lambda i: (i // nt, 0, i % nt, 0)))
        l_specs.append(pl.BlockSpec((1, dil, tm // dil, LANES), lambda i: (i // nt, 0, i % nt, 0)))
    return pl.pallas_call(
        functools.partial(_merge_out_kernel, tm),
        out_shape=jax.ShapeDtypeStruct((m, d), F32),
        grid=(m // tm,),
        in_specs=o_specs + l_specs + [pl.BlockSpec(gexp.shape, lambda i: (0, 0)), row,
                                      pl.BlockSpec((d, d), lambda i: (0, 0))],
        out_specs=row,
        scratch_shapes=[pltpu.VMEM((d // LANES, tm, LANES), F32), pltpu.VMEM((tm, LANES), F32)],
        compiler_params=_params(("parallel",), 40),
        name="merge_out",
    )(*outs, *lses, gexp, x, w)


SAMPLE_PAIRS = 2


def _attn_sample_kernel(ts, *refs):
    ng = N_GROUPS
    q_refs, k_refs, v_refs = refs[0:ng], refs[ng:2 * ng], refs[2 * ng:3 * ng]
    c_refs = refs[3 * ng:4 * ng]
    bc_refs = refs[4 * ng:5 * ng]
    bn_ref, o_ref = refs[5 * ng], refs[5 * ng + 1]
    npair = SAMPLE_PAIRS
    lane = lax.broadcasted_iota(jnp.int32, (ts, LANES), 1)
    lo_half = lane < ATT_HEAD_DIM
    zpad = jnp.zeros((LANES - npair * ts, LANES), F32)

    tk, vs = [], []
    for g in range(ng):
        kst = jnp.concatenate([k_refs[g][0, :, p * LANES:(p + 1) * LANES] for p in range(npair)] + [zpad], axis=0)
        vst = jnp.concatenate([v_refs[g][0, :, p * LANES:(p + 1) * LANES] for p in range(npair)] + [zpad], axis=0)
        tk.append(kst.T.astype(BF16))
        vs.append(vst.astype(BF16))

    for p in range(npair):
        sl = slice(p * LANES, (p + 1) * LANES)
        ms, ls, os_ = [], [], []
        for g in range(ng):
            window = c_refs[g].shape[-1]
            qp = q_refs[g][0, :, sl]
            q2 = jnp.concatenate([jnp.where(lo_half, qp, 0.0), jnp.where(lo_half, 0.0, qp)], axis=0).astype(BF16)
            kt = c_refs[g][0, 0, 2 * p:2 * p + 2].reshape(LANES, window).astype(BF16)
            vt = c_refs[g][0, 1, 2 * p:2 * p + 2].reshape(LANES, window).astype(BF16)
            sc = _dot(q2, kt) + bc_refs[g][p]
            sn = _dot(q2, tk[g]) + bn_ref[g, p]
            mx = jnp.maximum(jnp.max(sc, axis=-1, keepdims=True), jnp.max(sn, axis=-1, keepdims=True))
            pc = jnp.exp(sc - mx)
            pn = jnp.exp(sn - mx)
            ms.append(mx)
            ls.append(jnp.sum(pc, axis=-1, keepdims=True) + jnp.sum(pn, axis=-1, keepdims=True))
            os_.append(_dot_nt(pc.astype(BF16), vt) + _dot(pn.astype(BF16), vs[g]))
        mx = jnp.maximum(jnp.maximum(ms[0], ms[1]), ms[2])
        es = [jnp.exp(m - mx) for m in ms]
        inv = 1.0 / (es[0] * ls[0] + es[1] * ls[1] + es[2] * ls[2])
        om = (es[0] * inv) * os_[0] + (es[1] * inv) * os_[1] + (es[2] * inv) * os_[2]
        o_ref[0, :, sl] = jnp.where(lo_half, om[:ts], om[ts:])


def _attn_sample(qkv32, caches_t, bcs, bn):
    b, ts, _ = qkv32.shape
    d = ATT_HEADS * ATT_HEAD_DIM
    width = SAMPLE_PAIRS * LANES
    per_blk = d // width
    hb = 2 * SAMPLE_PAIRS

    def new_rows(which, g):
        return pl.BlockSpec((1, ts, width), lambda bi, hq: (bi, 0, (which * N_GROUPS + g) * per_blk + hq))

    in_specs = [new_rows(w, g) for w in range(3) for g in range(N_GROUPS)]
    for c in caches_t:
        in_specs.append(pl.BlockSpec((1, 2, hb, ATT_HEAD_DIM, c.shape[-1]), lambda bi, hq: (bi, 0, hq, 0, 0)))
    for t in bcs:
        in_specs.append(pl.BlockSpec((SAMPLE_PAIRS,) + t.shape[1:], lambda bi, hq: (hq, 0, 0)))
    in_specs.append(pl.BlockSpec((N_GROUPS, SAMPLE_PAIRS) + bn.shape[2:], lambda bi, hq: (0, hq, 0, 0)))
    return pl.pallas_call(
        functools.partial(_attn_sample_kernel, ts),
        out_shape=jax.ShapeDtypeStruct((b, ts, d), F32),
        grid=(b, per_blk),
        in_specs=in_specs,
        out_specs=pl.BlockSpec((1, ts, width), lambda bi, hq: (bi, 0, hq)),
        compiler_params=_params(("parallel", "arbitrary"), 48),
        name="attn_sample",
    )(*([qkv32] * (3 * N_GROUPS)), *caches_t, *bcs, bn)


def _matmul_res_kernel(a_ref, x_ref, w_ref, y_ref):
    y_ref[...] = x_ref[...] + _dot(a_ref[...].astype(BF16), w_ref[...])


def _matmul_res(a, x, w, tm):
    m, d = x.shape
    row = pl.BlockSpec((tm, d), lambda i: (i, 0))
    return pl.pallas_call(
        _matmul_res_kernel,
        out_shape=jax.ShapeDtypeStruct((m, d), F32),
        grid=(m // tm,),
        in_specs=[row, row, pl.BlockSpec((d, d), lambda i: (0, 0))],
        out_specs=row,
        compiler_params=_params(("parallel",), 40),
        name="matmul_res",
    )(a, x, w)


def _alibi_slopes():
    n = N_GROUPS * ATT_HEADS
    return (2.0 ** (-8.0 * np.arange(1, n + 1) / n)).reshape(N_GROUPS, ATT_HEADS)


def _prompt_bias(g):
    slopes = _alibi_slopes()[g] * GROUPS[g][1]
    dist = np.arange(N_BACK)[:, None] - np.arange(2 * N_BACK)[None, :] + N_BACK
    band = (dist >= 0) & (dist <= N_BACK)
    has_prev = np.arange(2 * N_BACK)[None, :] >= N_BACK
    out = np.empty((2, ATT_HEADS, N_BACK, 2 * N_BACK), np.float32)
    for k, valid in enumerate((band & has_prev, band)):
        out[k] = np.where(valid[None], -slopes[:, None, None] * dist[None], NEG)
    return jnp.asarray(out)


def _sample_bias(ts):
    slopes = _alibi_slopes()
    npairs = ATT_HEADS // 2
    bcs = []
    bn = np.full((N_GROUPS, npairs, 2 * ts, LANES), NEG, np.float32)
    for g, (window, dil) in enumerate(GROUPS):
        pos = np.arange(window)
        bc = np.full((npairs, 2 * ts, window), NEG, np.float32)
        for h in range(ATT_HEADS):
            for s in range(ts):
                row = (h % 2) * ts + s
                back = window + s - pos
                valid = (back % dil == 0) & (back <= window)
                bc[h // 2, row] = np.where(valid, -slopes[g, h] * back, NEG)
                lane0 = ((h // 2) % SAMPLE_PAIRS) * ts
                for s2 in range(s + 1):
                    if (s - s2) % dil == 0:
                        bn[g, h // 2, row, lane0 + s2] = -slopes[g, h] * (s - s2)
        bcs.append(jnp.asarray(bc))
    return bcs, jnp.asarray(bn)


def _head_group_matrices():
    d = ATT_HEADS * ATT_HEAD_DIM
    gsum = (np.arange(d)[:, None] // ATT_HEAD_DIM == np.arange(LANES)[None, :]).astype(np.float32)
    return jnp.asarray(gsum, BF16), jnp.asarray(gsum.T, BF16)


def _tile(m, want):
    return want if m % want == 0 else m


def kernel(x_prompt, x_sample, state_hgrn, cache_kv_w128, cache_kv_w512, cache_kv_w2048, hg_lb_logits, hg_w_q, hg_w_f, hg_w_i, hg_w_g, hg_w_o, hg_norm_o, att_w_qkv, att_w_o, att_q_norm, att_k_norm, norm_mix, norm_ffn, ffn_w_up, ffn_w_down):
    b, t_len, d = x_prompt.shape
    bs, ts, _ = x_sample.shape
    depth = norm_mix.shape[0]
    caches_all = (cache_kv_w128, cache_kv_w512, cache_kv_w2048)
    gsum, gexp = _head_group_matrices()
    bcs, bn = _sample_bias(ts)

    yp = x_prompt.reshape(b * t_len, d)
    ys = x_sample.reshape(bs * ts, d)
    tm_p = _tile(b * t_len, 512)
    tm_s = _tile(bs * ts, 256)
    hg_p, hg_s = [], []
    kv_p = [[] for _ in GROUPS]
    kv_s = [[] for _ in GROUPS]
    for layer in range(depth):
        a = layer // 2
        nw = norm_mix[layer][None]
        if layer % 2 == 0:
            w4 = jnp.stack([hg_w_q[a], hg_w_f[a], hg_w_i[a], hg_w_g[a]]).astype(BF16)
            wo = hg_w_o[a].astype(BF16)
            gn = hg_norm_o[a][None]
            zeros = jnp.zeros((b,) + state_hgrn.shape[2:], F32)
            new = []
            for y, s0, nb_, tl, tm, tc in ((yp, zeros, b, t_len, tm_p, _tile(t_len, 512)),
                                            (ys, state_hgrn[a], bs, ts, tm_s, ts)):
                qfig = _hgrn_proj(y, nw, hg_lb_logits, w4, layer, tm)
                o, st = _hgrn_rec(qfig, s0, nb_, tl, tc, 128 if tc >= 128 else 16)
                new.append((_hgrn_out(o, qfig, gn, y, wo, tm), st))
            (yp, sp), (ys, ss) = new
            hg_p.append(sp)
            hg_s.append(ss)
        else:
            wqkv = att_w_qkv[a].astype(BF16)
            wo = att_w_o[a].astype(BF16)
            qn_row = jnp.tile(att_q_norm[a], ATT_HEADS)
            kn_row = jnp.tile(att_k_norm[a], ATT_HEADS)
            qkn = jnp.stack([qn_row, kn_row])[:, None, :]
            qkv_groups = _qkv_perm(yp, nw, wqkv, qkn, gsum, gexp, b, t_len, tm_p)
            outs, lses = [], []
            for g, (window, dil) in enumerate(GROUPS):
                qb = 2 if (t_len // dil) % (2 * N_BACK) == 0 else 1
                o, lse = _attn_prompt(qkv_groups[g], _prompt_bias(g), qb)
                outs.append(o)
                lses.append(lse)
                keep = min(window, t_len)
                wt = jnp.stack([wqkv[:, (N_GROUPS + g) * d:(N_GROUPS + g + 1) * d].T,
                                wqkv[:, (2 * N_GROUPS + g) * d:(2 * N_GROUPS + g + 1) * d].T])
                kvt = _kv_tail(yp, nw, wt, kn_row[:, None], b, t_len, keep, _tile(keep, 512))
                kvt = kvt.reshape(b, 2, ATT_HEADS, ATT_HEAD_DIM, keep)
                kv_p[g].append(jnp.transpose(kvt, (0, 4, 1, 2, 3)))
            yp = _merge_out(outs, lses, gexp, yp, wo, t_len, tm_p)
            qkv32 = _qkv_plain(ys, nw, wqkv, qkn, gsum, gexp, tm_s).reshape(bs, ts, -1)
            caches_t = [jnp.transpose(c[a], (0, 2, 3, 4, 1)) for c in caches_all]
            om = _attn_sample(qkv32, caches_t, bcs, bn)
            ys = _matmul_res(om.reshape(bs * ts, d), ys, wo, tm_s)
            kv = qkv32.reshape(bs, ts, 3, N_GROUPS, ATT_HEADS, ATT_HEAD_DIM)
            for g in range(N_GROUPS):
                kv_s[g].append(kv[:, :, 1:, g])
        nf = norm_ffn[layer][None]
        wu = ffn_w_up[layer].astype(BF16)
        wd = ffn_w_down[layer].astype(BF16)
        yp = _ffn(yp, nf, wu, wd, tm_p, 1024)
        ys = _ffn(ys, nf, wu, wd, tm_s, 1024)
    return (yp.reshape(b, t_len, d), ys.reshape(bs, ts, d), jnp.stack(hg_p), jnp.stack(hg_s),
            jnp.stack(kv_p[0]), jnp.stack(kv_s[0]),
            jnp.stack(kv_p[1]), jnp.stack(kv_s[1]),
            jnp.stack(kv_p[2]), jnp.stack(kv_s[2]))
```

```python
import functools

import numpy as np
import jax
import jax.numpy as jnp
from jax import lax
from jax.experimental import pallas as pl
from jax.experimental.pallas import tpu as pltpu

F32 = jnp.float32
BF16 = jnp.bfloat16

RMS_EPS = 1e-6
HG_HEADS = 8
HG_DK = 128
GROUPS = ((128, 1), (512, 4), (2048, 16))
N_GROUPS = len(GROUPS)
ATT_HEADS = 16
ATT_HEAD_DIM = 64
ATT_SCALE = ATT_HEAD_DIM ** -0.5
N_BACK = 128
NEG = -1e30
LOG2E = 1.4426950408889634

LANES = 128
MIB = 1024 * 1024


def _dot(a, b):
    return jnp.dot(a, b, preferred_element_type=F32)


def _dot_nt(a, b):
    return lax.dot_general(a, b, (((1,), (1,)), ((), ())), preferred_element_type=F32)


def _dot_tn(a, b):
    return lax.dot_general(a, b, (((0,), (0,)), ((), ())), preferred_element_type=F32)


def _split3(x):
    hi = x.astype(BF16)
    r1 = x - hi.astype(F32)
    mid = r1.astype(BF16)
    lo = (r1 - mid.astype(F32)).astype(BF16)
    return hi, mid, lo


def _rms_rows(x, w):
    ms = jnp.mean(x * x, axis=-1, keepdims=True)
    return x * lax.rsqrt(ms + RMS_EPS) * w


def _params(sem, vmem_mib):
    return pltpu.CompilerParams(dimension_semantics=sem, vmem_limit_bytes=vmem_mib * MIB)


def _hgrn_proj_kernel(layer, x_ref, nw_ref, lbl_ref, w_ref, o_ref, xn_ref):
    j = pl.program_id(1)

    @pl.when(j == 0)
    def _():
        xn_ref[...] = _rms_rows(x_ref[...], nw_ref[...]).astype(BF16)

    y = _dot(xn_ref[...], w_ref[0])
    lg = lbl_ref[...]
    e = jnp.exp(lg - jnp.max(lg, axis=0, keepdims=True))
    lb = jnp.sum(e[:layer + 1], axis=0, keepdims=True) / jnp.sum(e, axis=0, keepdims=True)
    sig = jax.nn.sigmoid(y)
    o_ref[0] = jnp.where(j == 2, y, jnp.where(j == 1, lb + (1.0 - lb) * sig, y * sig))


def _hgrn_proj(x, nw, lb_logits, w4, layer, tm):
    m, d = x.shape
    return pl.pallas_call(
        functools.partial(_hgrn_proj_kernel, layer),
        out_shape=jax.ShapeDtypeStruct((4, m, d), F32),
        grid=(m // tm, 4),
        in_specs=[
            pl.BlockSpec((tm, d), lambda i, j: (i, 0)),
            pl.BlockSpec((1, d), lambda i, j: (0, 0)),
            pl.BlockSpec(lb_logits.shape, lambda i, j: (0, 0)),
            pl.BlockSpec((1, d, d), lambda i, j: (j, 0, 0)),
        ],
        out_specs=pl.BlockSpec((1, tm, d), lambda i, j: (j, i, 0)),
        scratch_shapes=[pltpu.VMEM((tm, d), BF16)],
        compiler_params=_params(("parallel", "arbitrary"), 40),
        name="hgrn_proj",
    )(x, nw, lb_logits, w4)


def _hgrn_rec_kernel(tc, c, q_ref, f_ref, v_ref, s0_ref, o_ref, so_ref, st_ref):
    t = pl.program_id(1)
    dk = HG_DK

    @pl.when(t == 0)
    def _():
        for h in range(HG_HEADS):
            st_ref[h] = s0_ref[0, h].T

    row = lax.broadcasted_iota(jnp.int32, (c, c), 0)
    col = lax.broadcasted_iota(jnp.int32, (c, c), 1)
    tril = jnp.where(row >= col, 1.0, 0.0).astype(BF16)
    rowl = lax.broadcasted_iota(jnp.int32, (c, dk), 0)
    n0 = min(c, 16)
    same_block = {}
    n = c // 2
    while n >= n0:
        shift = n.bit_length() - 1
        same_block[n] = (row >> shift) == (col >> shift)
        n //= 2

    def block_ref_rows(ch, n, pick):
        parts = [jnp.broadcast_to(ch[j * n + pick:j * n + pick + 1, :], (n, dk)) for j in range(c // n)]
        return parts[0] if len(parts) == 1 else jnp.concatenate(parts, axis=0)

    def do_chunk(qc, fc, vc, store):
        hi, mid, lo = _split3(jnp.log(fc))
        cum = _dot(tril, hi) + _dot(tril, mid) + _dot(tril, lo)
        for h in range(HG_HEADS):
            sl = slice(h * dk, (h + 1) * dk)
            qh, kh, vh, ch = qc[:, sl], 1.0 - fc[:, sl], vc[:, sl], cum[:, sl]
            s_t = st_ref[h]
            o = _dot_nt((qh * jnp.exp(ch)).astype(BF16), s_t.astype(BF16))
            attn = jnp.zeros((c, c), F32)
            n = c
            while n > n0:
                half = n // 2
                b = block_ref_rows(ch, n, half - 1)
                upper = (rowl & (n - 1)) >= half
                e = jnp.exp(jnp.where(upper, ch - b, b - ch))
                qt = jnp.where(upper, qh * e, 0.0).astype(BF16)
                kt = jnp.where(upper, 0.0, kh * e).astype(BF16)
                a = _dot_nt(qt, kt)
                if n < c:
                    a = jnp.where(same_block[n], a, 0.0)
                attn = attn + a
                n = half
            dq = ch - block_ref_rows(ch, n0, n0 // 2 - 1)
            a = _dot_nt((qh * jnp.exp(dq)).astype(BF16), (kh * jnp.exp(-dq)).astype(BF16))
            keep = col <= row
            if n0 < c:
                keep = keep & same_block[n0]
            attn = attn + jnp.where(keep, a, 0.0)
            o = o + _dot(attn.astype(BF16), vh.astype(BF16))
            store(sl, o)
            last = ch[c - 1:c, :]
            kd = kh * jnp.exp(last - ch)
            st_ref[h] = s_t * jnp.exp(last) + _dot_tn(vh.astype(BF16), kd.astype(BF16))

    if tc >= c:
        def body(ci, carry):
            rows = pl.ds(pl.multiple_of(ci * c, c), c)

            def store(sl, o):
                o_ref[rows, sl] = o

            do_chunk(q_ref[0, rows, :], f_ref[0, rows, :], v_ref[0, rows, :], store)
            return carry

        lax.fori_loop(0, tc // c, body, 0)
    else:
        pad = c - tc
        z = jnp.zeros((pad, q_ref.shape[-1]), F32)

        def store(sl, o):
            o_ref[:, sl] = o[:tc]

        do_chunk(jnp.concatenate([q_ref[0], z], axis=0),
                 jnp.concatenate([f_ref[0], z + 1.0], axis=0),
                 jnp.concatenate([v_ref[0], z], axis=0), store)

    @pl.when(t == pl.num_programs(1) - 1)
    def _():
        for h in range(HG_HEADS):
            so_ref[0, h] = st_ref[h].T


def _hgrn_rec(qfig, s0, b, t_len, tc, c):
    _, m, d = qfig.shape
    nt = t_len // tc
    h, dk, dv = s0.shape[1:]

    def row_spec(which):
        return pl.BlockSpec((1, tc, d), lambda bi, ti: (which, bi * nt + ti, 0))

    return pl.pallas_call(
        functools.partial(_hgrn_rec_kernel, tc, c),
        out_shape=(jax.ShapeDtypeStruct((m, d), F32), jax.ShapeDtypeStruct(s0.shape, F32)),
        grid=(b, nt),
        in_specs=[row_spec(0), row_spec(1), row_spec(2),
                  pl.BlockSpec((1, h, dk, dv), lambda bi, ti: (bi, 0, 0, 0))],
        out_specs=(pl.BlockSpec((tc, d), lambda bi, ti: (bi * nt + ti, 0)),
                   pl.BlockSpec((1, h, dk, dv), lambda bi, ti: (bi, 0, 0, 0))),
        scratch_shapes=[pltpu.VMEM((h, dv, dk), F32)],
        compiler_params=_params(("parallel", "arbitrary"), 48),
        name="hgrn_rec",
    )(qfig, qfig, qfig, s0)


def _hgrn_out_kernel(o_ref, g_ref, gn_ref, x_ref, w_ref, y_ref):
    a = _rms_rows(o_ref[...], gn_ref[...]) * g_ref[0]
    y_ref[...] = x_ref[...] + _dot(a.astype(BF16), w_ref[...])


def _hgrn_out(o, qfig, gn, x, w, tm):
    m, d = x.shape
    return pl.pallas_call(
        _hgrn_out_kernel,
        out_shape=jax.ShapeDtypeStruct((m, d), F32),
        grid=(m // tm,),
        in_specs=[
            pl.BlockSpec((tm, d), lambda i: (i, 0)),
            pl.BlockSpec((1, tm, d), lambda i: (3, i, 0)),
            pl.BlockSpec((1, d), lambda i: (0, 0)),
            pl.BlockSpec((tm, d), lambda i: (i, 0)),
            pl.BlockSpec((d, d), lambda i: (0, 0)),
        ],
        out_specs=pl.BlockSpec((tm, d), lambda i: (i, 0)),
        compiler_params=_params(("parallel",), 40),
        name="hgrn_out",
    )(o, qfig, gn, x, w)


def _ffn_kernel(x_ref, nw_ref, wu_ref, wd_ref, y_ref, xn_ref):
    j = pl.program_id(1)

    @pl.when(j == 0)
    def _():
        x = x_ref[...]
        xn_ref[...] = _rms_rows(x, nw_ref[...]).astype(BF16)
        y_ref[...] = x

    h = jnp.maximum(_dot(xn_ref[...], wu_ref[...]), 0.0)
    y_ref[...] += _dot((h * h).astype(BF16), wd_ref[...])


def _ffn(x, nw, wu, wd, tm, tf):
    m, d = x.shape
    ff = wu.shape[1]
    return pl.pallas_call(
        _ffn_kernel,
        out_shape=jax.ShapeDtypeStruct((m, d), F32),
        grid=(m // tm, ff // tf),
        in_specs=[
            pl.BlockSpec((tm, d), lambda i, j: (i, 0)),
            pl.BlockSpec((1, d), lambda i, j: (0, 0)),
            pl.BlockSpec((d, tf), lambda i, j: (0, j)),
            pl.BlockSpec((tf, d), lambda i, j: (j, 0)),
        ],
        out_specs=pl.BlockSpec((tm, d), lambda i, j: (i, 0)),
        scratch_shapes=[pltpu.VMEM((tm, d), BF16)],
        compiler_params=_params(("parallel", "arbitrary"), 48),
        name="ffn",
    )(x, nw, wu, wd)


def _qk_norm_rows(y, qkn, gblk):
    y2 = (y * y).astype(BF16)
    w = gblk.shape[0]
    ss = jnp.concatenate([_dot(y2[:, c * w:(c + 1) * w], gblk) for c in range(y.shape[-1] // w)], axis=1)
    return y * lax.rsqrt(ss * (1.0 / ATT_HEAD_DIM) + RMS_EPS) * qkn


def _qkv_perm_kernel(tm, x_ref, nw_ref, w_ref, qkn_ref, gblk_ref, o0_ref, o1_ref, o2_ref, xs_ref, xp_ref):
    j = pl.program_id(1)
    outs = (o0_ref, o1_ref, o2_ref)
    d = x_ref.shape[-1]
    nslab = d // LANES

    @pl.when(j == 0)
    def _():
        xn = _rms_rows(x_ref[...], nw_ref[...])
        xp_ref[0] = xn.astype(BF16)
        for c in range(nslab):
            xs_ref[c] = xn[:, c * LANES:(c + 1) * LANES]
        for g in range(1, N_GROUPS):
            dil = GROUPS[g][1]
            n = tm // dil
            for c in range(nslab):
                for r in range(dil):
                    xp_ref[g, r * n:(r + 1) * n, c * LANES:(c + 1) * LANES] = (
                        xs_ref.at[c][pl.ds(r, n, stride=dil), :].astype(BF16))

    grp = j // 3
    which = j % 3
    y = _dot(xp_ref[grp], w_ref[...])
    val = jnp.where(which < 2, _qk_norm_rows(y, qkn_ref[0], gblk_ref[...]), y).astype(BF16)
    for g in range(N_GROUPS):
        dil = GROUPS[g][1]

        @pl.when(grp == g)
        def _():
            outs[g][0, 0] = val.reshape(dil, tm // dil, d)


def _qkv_perm(x, nw, w, qkn, gblk, b, t_len, tm):
    m, d = x.shape
    nt = t_len // tm
    out_shapes, out_specs = [], []
    for g, (_, dil) in enumerate(GROUPS):
        out_shapes.append(jax.ShapeDtypeStruct((3, b, dil, t_len // dil, d), BF16))
        out_specs.append(pl.BlockSpec(
            (1, 1, dil, tm // dil, d),
            lambda i, j, g=g: (jnp.clip(j - 3 * g, 0, 2), i // nt, 0, i % nt, 0)))
    return pl.pallas_call(
        functools.partial(_qkv_perm_kernel, tm),
        out_shape=tuple(out_shapes),
        grid=(m // tm, 3 * N_GROUPS),
        in_specs=[
            pl.BlockSpec((tm, d), lambda i, j: (i, 0)),
            pl.BlockSpec((1, d), lambda i, j: (0, 0)),
            pl.BlockSpec((d, d), lambda i, j: (0, (j % 3) * N_GROUPS + j // 3)),
            pl.BlockSpec((1, 1, d), lambda i, j: (jnp.minimum(j % 3, 1), 0, 0)),
            pl.BlockSpec(gblk.shape, lambda i, j: (0, 0)),
        ],
        out_specs=tuple(out_specs),
        scratch_shapes=[pltpu.VMEM((d // LANES, tm, LANES), F32),
                        pltpu.VMEM((N_GROUPS, tm, d), BF16)],
        compiler_params=_params(("parallel", "arbitrary"), 48),
        name="qkv_perm",
    )(x, nw, w, qkn, gblk)


def _qkv_plain_kernel(x_ref, nw_ref, w_ref, qkn_ref, gblk_ref, o_ref, xn_ref):
    j = pl.program_id(1)

    @pl.when(j == 0)
    def _():
        xn_ref[...] = _rms_rows(x_ref[...], nw_ref[...]).astype(BF16)

    y = _dot(xn_ref[...], w_ref[...])

    @pl.when(j < 2 * N_GROUPS)
    def _():
        o_ref[...] = _qk_norm_rows(y, qkn_ref[0], gblk_ref[...])

    @pl.when(j >= 2 * N_GROUPS)
    def _():
        o_ref[...] = y


def _qkv_plain(x, nw, w, qkn, gblk, tm):
    m, d = x.shape
    n = w.shape[1]
    return pl.pallas_call(
        _qkv_plain_kernel,
        out_shape=jax.ShapeDtypeStruct((m, n), F32),
        grid=(m // tm, n // d),
        in_specs=[
            pl.BlockSpec((tm, d), lambda i, j: (i, 0)),
            pl.BlockSpec((1, d), lambda i, j: (0, 0)),
            pl.BlockSpec((d, d), lambda i, j: (0, j)),
            pl.BlockSpec((1, 1, d), lambda i, j: (jnp.minimum(j // N_GROUPS, 1), 0, 0)),
            pl.BlockSpec(gblk.shape, lambda i, j: (0, 0)),
        ],
        out_specs=pl.BlockSpec((tm, d), lambda i, j: (i, j)),
        scratch_shapes=[pltpu.VMEM((tm, d), BF16)],
        compiler_params=_params(("parallel", "arbitrary"), 40),
        name="qkv_plain",
    )(x, nw, w, qkn, gblk)


def _kv_tail_kernel(x_ref, nw_ref, wt_ref, kn_ref, o_ref):
    kv = pl.program_id(2)
    xn = _rms_rows(x_ref[...], nw_ref[...]).astype(BF16)
    yt = _dot_nt(wt_ref[0], xn)

    @pl.when(kv == 0)
    def _():
        y3 = yt.reshape(ATT_HEADS, ATT_HEAD_DIM, yt.shape[-1])
        ms = jnp.mean(y3 * y3, axis=1, keepdims=True)
        o_ref[0, 0] = (y3 * lax.rsqrt(ms + RMS_EPS)).reshape(yt.shape) * kn_ref[...]

    @pl.when(kv == 1)
    def _():
        o_ref[0, 0] = yt


def _kv_tail(x, nw, wt, kn_col, b, t_len, keep, tt):
    m, d = x.shape
    first = (t_len - keep) // tt
    per_seq = t_len // tt
    return pl.pallas_call(
        _kv_tail_kernel,
        out_shape=jax.ShapeDtypeStruct((b, 2, d, keep), F32),
        grid=(b, keep // tt, 2),
        in_specs=[
            pl.BlockSpec((tt, d), lambda bi, ti, kv: (bi * per_seq + first + ti, 0)),
            pl.BlockSpec((1, d), lambda bi, ti, kv: (0, 0)),
            pl.BlockSpec((1, d, d), lambda bi, ti, kv: (kv, 0, 0)),
            pl.BlockSpec((d, 1), lambda bi, ti, kv: (0, 0)),
        ],
        out_specs=pl.BlockSpec((1, 1, d, tt), lambda bi, ti, kv: (bi, kv, 0, ti)),
        compiler_params=_params(("parallel", "parallel", "arbitrary"), 40),
        name="kv_tail",
    )(x, nw, wt, kn_col)


def _attn_prompt_kernel(qb, q_ref, kc_ref, kp_ref, vc_ref, vp_ref, bias_ref, o_ref, m_ref, l_ref):
    i = pl.program_id(2)
    first = jnp.where(i == 0, 0, 1)
    lane = lax.broadcasted_iota(jnp.int32, (N_BACK, LANES), 1)
    lo_half = lane < ATT_HEAD_DIM
    for u in range(qb):
        rows = slice(u * N_BACK, (u + 1) * N_BACK)
        m_acc = jnp.zeros((N_BACK, LANES), F32)
        l_acc = jnp.ones((N_BACK, LANES), F32)
        for pr in range(ATT_HEADS // 2):
            sl = slice(pr * LANES, (pr + 1) * LANES)
            qp = q_ref[0, 0, 0, rows, sl]
            if u == 0:
                kprev, vprev = kp_ref[0, 0, 0, :, sl], vp_ref[0, 0, 0, :, sl]
            else:
                prev = slice((u - 1) * N_BACK, u * N_BACK)
                kprev, vprev = kc_ref[0, 0, 0, prev, sl], vc_ref[0, 0, 0, prev, sl]
            kk = jnp.concatenate([kprev, kc_ref[0, 0, 0, rows, sl]], axis=0)
            vv = jnp.concatenate([vprev, vc_ref[0, 0, 0, rows, sl]], axis=0)
            zq = jnp.zeros_like(qp)
            q2 = jnp.concatenate([jnp.where(lo_half, qp, zq), jnp.where(lo_half, zq, qp)], axis=0)
            s = _dot_nt(q2, kk) + (bias_ref[first, pr] if u == 0 else bias_ref[1, pr])
            mx = jnp.max(s, axis=-1, keepdims=True)
            p = jnp.exp2(s - mx)
            l = jnp.sum(p, axis=-1, keepdims=True)
            o2 = _dot(p.astype(BF16), vv)
            o_ref[0, 0, rows, sl] = jnp.where(lo_half, o2[:N_BACK], o2[N_BACK:]).astype(BF16)
            m_acc = jnp.where(lane == 2 * pr, mx[:N_BACK], jnp.where(lane == 2 * pr + 1, mx[N_BACK:], m_acc))
            l_acc = jnp.where(lane == 2 * pr, l[:N_BACK], jnp.where(lane == 2 * pr + 1, l[N_BACK:], l_acc))
        m_ref[0, 0, rows, :] = m_acc
        l_ref[0, 0, rows, :] = l_acc


def _attn_prompt(qkv_g, bias, qb):
    _, b, dil, l, d = qkv_g.shape
    rows = qb * N_BACK

    def cur(which):
        return pl.BlockSpec((1, 1, 1, rows, d), lambda bi, r, i: (which, bi, r, i, 0))

    def prev(which):
        return pl.BlockSpec((1, 1, 1, N_BACK, d),
                            lambda bi, r, i: (which, bi, r, jnp.maximum(i * qb - 1, 0), 0))

    return pl.pallas_call(
        functools.partial(_attn_prompt_kernel, qb),
        out_shape=(jax.ShapeDtypeStruct((b, dil, l, d), BF16),
                   jax.ShapeDtypeStruct((b, dil, l, LANES), F32),
                   jax.ShapeDtypeStruct((b, dil, l, LANES), F32)),
        grid=(b, dil, l // rows),
        in_specs=[cur(0), cur(1), prev(1), cur(2), prev(2),
                  pl.BlockSpec(bias.shape, lambda bi, r, i: (0, 0, 0, 0))],
        out_specs=(pl.BlockSpec((1, 1, rows, d), lambda bi, r, i: (bi, r, i, 0)),
                   pl.BlockSpec((1, 1, rows, LANES), lambda bi, r, i: (bi, r, i, 0)),
                   pl.BlockSpec((1, 1, rows, LANES), lambda bi, r, i: (bi, r, i, 0))),
        compiler_params=_params(("parallel", "parallel", "arbitrary"), 40),
        name="attn_prompt_d%d" % dil,
    )(qkv_g, qkv_g, qkv_g, qkv_g, qkv_g, bias)


def _merge_out_kernel(tm, o0_ref, o1_ref, o2_ref, m0_ref, m1_ref, m2_ref, l0_ref, l1_ref, l2_ref,
                      gexp_ref, x_ref, w_ref, y_ref, os_ref, ms_ref, ls_ref):
    d = x_ref.shape[-1]
    nslab = d // LANES

    def natural(o_ref, m_ref, l_ref, dil):
        if dil == 1:
            return o_ref[0, 0].astype(F32), m_ref[0, 0], l_ref[0, 0]
        n = tm // dil
        for r in range(dil):
            ms_ref[pl.ds(r, n, stride=dil), :] = m_ref[0, r]
            ls_ref[pl.ds(r, n, stride=dil), :] = l_ref[0, r]
            for c in range(nslab):
                os_ref.at[c][pl.ds(r, n, stride=dil), :] = o_ref[0, r, :, c * LANES:(c + 1) * LANES].astype(F32)
        return jnp.concatenate([os_ref[c] for c in range(nslab)], axis=1), ms_ref[...], ls_ref[...]

    trip = [natural(o_ref, m_ref, l_ref, GROUPS[g][1]) for g, (o_ref, m_ref, l_ref) in enumerate(
        ((o0_ref, m0_ref, l0_ref), (o1_ref, m1_ref, l1_ref), (o2_ref, m2_ref, l2_ref)))]
    mx = jnp.maximum(jnp.maximum(trip[0][1], trip[1][1]), trip[2][1])
    es = [jnp.exp2(t[1] - mx) for t in trip]
    inv = 1.0 / (es[0] * trip[0][2] + es[1] * trip[1][2] + es[2] * trip[2][2])
    acc = None
    for e, (o, _, _) in zip(es, trip):
        wt = e * inv
        hi = wt.astype(BF16)
        lo = (wt - hi.astype(F32)).astype(BF16)
        wx = _dot(jnp.concatenate([hi, lo], axis=1), gexp_ref[...])
        acc = wx * o if acc is None else acc + wx * o
    y_ref[...] = x_ref[...] + _dot(acc.astype(BF16), w_ref[...])


def _merge_out(outs, ms, ls, gexp, x, w, t_len, tm):
    m, d = x.shape
    nt = t_len // tm
    row = pl.BlockSpec((tm, d), lambda i: (i, 0))
    o_specs, s_specs = [], []
    for _, dil in GROUPS:
        o_specs.append(pl.BlockSpec((1, dil, tm // dil, d), lambda i: (i // nt, 0, i % nt, 0)))
        s_specs.append(pl.BlockSpec((1, dil, tm // dil, LANES), lambda i: (i // nt, 0, i % nt, 0)))
    return pl.pallas_call(
        functools.partial(_merge_out_kernel, tm),
        out_shape=jax.ShapeDtypeStruct((m, d), F32),
        grid=(m // tm,),
        in_specs=o_specs + s_specs + s_specs + [pl.BlockSpec(gexp.shape, lambda i: (0, 0)), row,
                                                pl.BlockSpec((d, d), lambda i: (0, 0))],
        out_specs=row,
        scratch_shapes=[pltpu.VMEM((d // LANES, tm, LANES), F32), pltpu.VMEM((tm, LANES), F32),
                        pltpu.VMEM((tm, LANES), F32)],
        compiler_params=_params(("parallel",), 40),
        name="merge_out",
    )(*outs, *ms, *ls, gexp, x, w)


SAMPLE_PAIRS = 2


def _attn_sample_kernel(ts, *refs):
    ng = N_GROUPS
    q_refs, k_refs, v_refs = refs[0:ng], refs[ng:2 * ng], refs[2 * ng:3 * ng]
    c_refs = refs[3 * ng:4 * ng]
    bc_refs = refs[4 * ng:5 * ng]
    bn_ref, o_ref = refs[5 * ng], refs[5 * ng + 1]
    npair = SAMPLE_PAIRS
    lane = lax.broadcasted_iota(jnp.int32, (ts, LANES), 1)
    lo_half = lane < ATT_HEAD_DIM
    zpad = jnp.zeros((LANES - npair * ts, LANES), F32)

    tk, vs = [], []
    for g in range(ng):
        kst = jnp.concatenate([k_refs[g][0, :, p * LANES:(p + 1) * LANES] for p in range(npair)] + [zpad], axis=0)
        vst = jnp.concatenate([v_refs[g][0, :, p * LANES:(p + 1) * LANES] for p in range(npair)] + [zpad], axis=0)
        tk.append(kst.T.astype(BF16))
        vs.append(vst.astype(BF16))

    for p in range(npair):
        sl = slice(p * LANES, (p + 1) * LANES)
        ms, ls, os_ = [], [], []
        for g in range(ng):
            window = c_refs[g].shape[-1]
            qp = q_refs[g][0, :, sl]
            q2 = jnp.concatenate([jnp.where(lo_half, qp, 0.0), jnp.where(lo_half, 0.0, qp)], axis=0).astype(BF16)
            kt = c_refs[g][0, 0, 2 * p:2 * p + 2].reshape(LANES, window).astype(BF16)
            vt = c_refs[g][0, 1, 2 * p:2 * p + 2].reshape(LANES, window).astype(BF16)
            sc = _dot(q2, kt) + bc_refs[g][p]
            sn = _dot(q2, tk[g]) + bn_ref[g, p]
            mx = jnp.maximum(jnp.max(sc, axis=-1, keepdims=True), jnp.max(sn, axis=-1, keepdims=True))
            pc = jnp.exp2(sc - mx)
            pn = jnp.exp2(sn - mx)
            ms.append(mx)
            ls.append(jnp.sum(pc, axis=-1, keepdims=True) + jnp.sum(pn, axis=-1, keepdims=True))
            os_.append(_dot_nt(pc.astype(BF16), vt) + _dot(pn.astype(BF16), vs[g]))
        mx = jnp.maximum(jnp.maximum(ms[0], ms[1]), ms[2])
        es = [jnp.exp2(m - mx) for m in ms]
        inv = 1.0 / (es[0] * ls[0] + es[1] * ls[1] + es[2] * ls[2])
        om = (es[0] * inv) * os_[0] + (es[1] * inv) * os_[1] + (es[2] * inv) * os_[2]
        o_ref[0, :, sl] = jnp.where(lo_half, om[:ts], om[ts:])


def _attn_sample(qkv32, caches_t, bcs, bn):
    b, ts, _ = qkv32.shape
    d = ATT_HEADS * ATT_HEAD_DIM
    width = SAMPLE_PAIRS * LANES
    per_blk = d // width
    hb = 2 * SAMPLE_PAIRS

    def new_rows(which, g):
        return pl.BlockSpec((1, ts, width), lambda bi, hq: (bi, 0, (which * N_GROUPS + g) * per_blk + hq))

    in_specs = [new_rows(w, g) for w in range(3) for g in range(N_GROUPS)]
    for c in caches_t:
        in_specs.append(pl.BlockSpec((1, 2, hb, ATT_HEAD_DIM, c.shape[-1]), lambda bi, hq: (bi, 0, hq, 0, 0)))
    for t in bcs:
        in_specs.append(pl.BlockSpec((SAMPLE_PAIRS,) + t.shape[1:], lambda bi, hq: (hq, 0, 0)))
    in_specs.append(pl.BlockSpec((N_GROUPS, SAMPLE_PAIRS) + bn.shape[2:], lambda bi, hq: (0, hq, 0, 0)))
    return pl.pallas_call(
        functools.partial(_attn_sample_kernel, ts),
        out_shape=jax.ShapeDtypeStruct((b, ts, d), F32),
        grid=(b, per_blk),
        in_specs=in_specs,
        out_specs=pl.BlockSpec((1, ts, width), lambda bi, hq: (bi, 0, hq)),
        compiler_params=_params(("parallel", "arbitrary"), 48),
        name="attn_sample",
    )(*([qkv32] * (3 * N_GROUPS)), *caches_t, *bcs, bn)


def _matmul_res_kernel(a_ref, x_ref, w_ref, y_ref):
    y_ref[...] = x_ref[...] + _dot(a_ref[...].astype(BF16), w_ref[...])


def _matmul_res(a, x, w, tm):
    m, d = x.shape
    row = pl.BlockSpec((tm, d), lambda i: (i, 0))
    return pl.pallas_call(
        _matmul_res_kernel,
        out_shape=jax.ShapeDtypeStruct((m, d), F32),
        grid=(m // tm,),
        in_specs=[row, row, pl.BlockSpec((d, d), lambda i: (0, 0))],
        out_specs=row,
        compiler_params=_params(("parallel",), 40),
        name="matmul_res",
    )(a, x, w)


def _alibi_slopes():
    n = N_GROUPS * ATT_HEADS
    return LOG2E * (2.0 ** (-8.0 * np.arange(1, n + 1) / n)).reshape(N_GROUPS, ATT_HEADS)


def _prompt_bias(g):
    slopes = _alibi_slopes()[g] * GROUPS[g][1]
    dist = np.arange(N_BACK)[:, None] - np.arange(2 * N_BACK)[None, :] + N_BACK
    band = (dist >= 0) & (dist <= N_BACK)
    has_prev = np.arange(2 * N_BACK)[None, :] >= N_BACK
    out = np.empty((2, ATT_HEADS, N_BACK, 2 * N_BACK), np.float32)
    for k, valid in enumerate((band & has_prev, band)):
        out[k] = np.where(valid[None], -slopes[:, None, None] * dist[None], NEG)
    return jnp.asarray(out.reshape(2, ATT_HEADS // 2, 2 * N_BACK, 2 * N_BACK))


def _sample_bias(ts):
    slopes = _alibi_slopes()
    npairs = ATT_HEADS // 2
    bcs = []
    bn = np.full((N_GROUPS, npairs, 2 * ts, LANES), NEG, np.float32)
    for g, (window, dil) in enumerate(GROUPS):
        pos = np.arange(window)
        bc = np.full((npairs, 2 * ts, window), NEG, np.float32)
        for h in range(ATT_HEADS):
            for s in range(ts):
                row = (h % 2) * ts + s
                back = window + s - pos
                valid = (back % dil == 0) & (back <= window)
                bc[h // 2, row] = np.where(valid, -slopes[g, h] * back, NEG)
                lane0 = ((h // 2) % SAMPLE_PAIRS) * ts
                for s2 in range(s + 1):
                    if (s - s2) % dil == 0:
                        bn[g, h // 2, row, lane0 + s2] = -slopes[g, h] * (s - s2)
        bcs.append(jnp.asarray(bc))
    return bcs, jnp.asarray(bn)


def _head_group_matrices():
    d = ATT_HEADS * ATT_HEAD_DIM
    i = np.arange(2 * LANES)
    gblk = (i[:, None] // ATT_HEAD_DIM == i[None, :] // ATT_HEAD_DIM).astype(np.float32)
    gexp = (np.arange(LANES)[:, None] == np.arange(d)[None, :] // ATT_HEAD_DIM).astype(np.float32)
    return jnp.asarray(gblk, BF16), jnp.asarray(np.concatenate([gexp, gexp], axis=0), BF16)


def _tile(m, want):
    return want if m % want == 0 else m


def kernel(x_prompt, x_sample, state_hgrn, cache_kv_w128, cache_kv_w512, cache_kv_w2048, hg_lb_logits, hg_w_q, hg_w_f, hg_w_i, hg_w_g, hg_w_o, hg_norm_o, att_w_qkv, att_w_o, att_q_norm, att_k_norm, norm_mix, norm_ffn, ffn_w_up, ffn_w_down):
    b, t_len, d = x_prompt.shape
    bs, ts, _ = x_sample.shape
    depth = norm_mix.shape[0]
    caches_all = (cache_kv_w128, cache_kv_w512, cache_kv_w2048)
    gblk, gexp = _head_group_matrices()
    bcs, bn = _sample_bias(ts)

    yp = x_prompt.reshape(b * t_len, d)
    ys = x_sample.reshape(bs * ts, d)
    tm_p = _tile(b * t_len, 512)
    tm_s = _tile(bs * ts, 256)
    hg_p, hg_s = [], []
    kv_p = [[] for _ in GROUPS]
    kv_s = [[] for _ in GROUPS]
    for layer in range(depth):
        a = layer // 2
        nw = norm_mix[layer][None]
        if layer % 2 == 0:
            w4 = jnp.stack([hg_w_q[a], hg_w_f[a], hg_w_i[a], hg_w_g[a]]).astype(BF16)
            wo = hg_w_o[a].astype(BF16)
            gn = hg_norm_o[a][None]
            zeros = jnp.zeros((b,) + state_hgrn.shape[2:], F32)
            new = []
            for y, s0, nb_, tl, tm, tc in ((yp, zeros, b, t_len, tm_p, _tile(t_len, 512)),
                                            (ys, state_hgrn[a], bs, ts, tm_s, ts)):
                qfig = _hgrn_proj(y, nw, hg_lb_logits, w4, layer, tm)
                o, st = _hgrn_rec(qfig, s0, nb_, tl, tc, 128 if tc >= 128 else 16)
                new.append((_hgrn_out(o, qfig, gn, y, wo, tm), st))
            (yp, sp), (ys, ss) = new
            hg_p.append(sp)
            hg_s.append(ss)
        else:
            wqkv = att_w_qkv[a].astype(BF16)
            wo = att_w_o[a].astype(BF16)
            qn_row = jnp.tile(att_q_norm[a], ATT_HEADS)
            kn_row = jnp.tile(att_k_norm[a], ATT_HEADS)
            qkn = jnp.stack([qn_row * (ATT_SCALE * LOG2E), kn_row])[:, None, :]
            qkv_groups = _qkv_perm(yp, nw, wqkv, qkn, gblk, b, t_len, tm_p)
            outs, ms, ls = [], [], []
            for g, (window, dil) in enumerate(GROUPS):
                qb = 2 if (t_len // dil) % (2 * N_BACK) == 0 else 1
                o, mrow, lrow = _attn_prompt(qkv_groups[g], _prompt_bias(g), qb)
                outs.append(o)
                ms.append(mrow)
                ls.append(lrow)
                keep = min(window, t_len)
                wt = jnp.stack([wqkv[:, (N_GROUPS + g) * d:(N_GROUPS + g + 1) * d].T,
                                wqkv[:, (2 * N_GROUPS + g) * d:(2 * N_GROUPS + g + 1) * d].T])
                kvt = _kv_tail(yp, nw, wt, kn_row[:, None], b, t_len, keep, _tile(keep, 512))
                kvt = kvt.reshape(b, 2, ATT_HEADS, ATT_HEAD_DIM, keep)
                kv_p[g].append(jnp.transpose(kvt, (0, 4, 1, 2, 3)))
            yp = _merge_out(outs, ms, ls, gexp, yp, wo, t_len, tm_p)
            qkv32 = _qkv_plain(ys, nw, wqkv, qkn, gblk, tm_s).reshape(bs, ts, -1)
            caches_t = [jnp.transpose(c[a], (0, 2, 3, 4, 1)) for c in caches_all]
            om = _attn_sample(qkv32, caches_t, bcs, bn)
            ys = _matmul_res(om.reshape(bs * ts, d), ys, wo, tm_s)
            kv = qkv32.reshape(bs, ts, 3, N_GROUPS, ATT_HEADS, ATT_HEAD_DIM)
            for g in range(N_GROUPS):
                kv_s[g].append(kv[:, :, 1:, g])
        nf = norm_ffn[layer][None]
        wu = ffn_w_up[layer].astype(BF16)
        wd = ffn_w_down[layer].astype(BF16)
        yp = _ffn(yp, nf, wu, wd, tm_p, 1024)
        ys = _ffn(ys, nf, wu, wd, tm_s, 1024)
    return (yp.reshape(b, t_len, d), ys.reshape(bs, ts, d), jnp.stack(hg_p), jnp.stack(hg_s),
            jnp.stack(kv_p[0]), jnp.stack(kv_s[0]),
            jnp.stack(kv_p[1]), jnp.stack(kv_s[1]),
            jnp.stack(kv_p[2]), jnp.stack(kv_s[2]))
```

```python
import functools

import numpy as np
import jax
import jax.numpy as jnp
from jax import lax
from jax.experimental import pallas as pl
from jax.experimental.pallas import tpu as pltpu

F32 = jnp.float32
BF16 = jnp.bfloat16

RMS_EPS = 1e-6
HG_HEADS = 8
HG_DK = 128
GROUPS = ((128, 1), (512, 4), (2048, 16))
N_GROUPS = len(GROUPS)
ATT_HEADS = 16
ATT_HEAD_DIM = 64
ATT_SCALE = ATT_HEAD_DIM ** -0.5
N_BACK = 128
NEG = -1e30
LOG2E = 1.4426950408889634

LANES = 128
MIB = 1024 * 1024


def _dot(a, b):
    return jnp.dot(a, b, preferred_element_type=F32)


def _dot_nt(a, b):
    return lax.dot_general(a, b, (((1,), (1,)), ((), ())), preferred_element_type=F32)


def _dot_tn(a, b):
    return lax.dot_general(a, b, (((0,), (0,)), ((), ())), preferred_element_type=F32)


def _split3(x):
    hi = x.astype(BF16)
    r1 = x - hi.astype(F32)
    mid = r1.astype(BF16)
    lo = (r1 - mid.astype(F32)).astype(BF16)
    return hi, mid, lo


def _rms_rows(x, w):
    ms = jnp.mean(x * x, axis=-1, keepdims=True)
    return x * lax.rsqrt(ms + RMS_EPS) * w


def _params(sem, vmem_mib):
    return pltpu.CompilerParams(dimension_semantics=sem, vmem_limit_bytes=vmem_mib * MIB)


def _hgrn_proj_kernel(layer, x_ref, nw_ref, lbl_ref, w_ref, f_ref, o_ref, xn_ref):
    j = pl.program_id(1)

    @pl.when(j == 0)
    def _():
        xn_ref[...] = _rms_rows(x_ref[...], nw_ref[...]).astype(BF16)

    y = _dot(xn_ref[...], w_ref[j])
    lg = lbl_ref[...]
    e = jnp.exp(lg - jnp.max(lg, axis=0, keepdims=True))
    lb = jnp.sum(e[:layer + 1], axis=0, keepdims=True) / jnp.sum(e, axis=0, keepdims=True)
    sig = jax.nn.sigmoid(y)
    val = jnp.where(j == 2, y, jnp.where(j == 0, lb + (1.0 - lb) * sig, y * sig))
    o_ref[0] = val.astype(o_ref.dtype)

    @pl.when(j == 0)
    def _():
        f_ref[...] = val


def _hgrn_proj(x, nw, lb_logits, w4, layer, tm, qig_dtype):
    m, d = x.shape
    return pl.pallas_call(
        functools.partial(_hgrn_proj_kernel, layer),
        out_shape=(jax.ShapeDtypeStruct((m, d), F32), jax.ShapeDtypeStruct((3, m, d), qig_dtype)),
        grid=(m // tm, 4),
        in_specs=[
            pl.BlockSpec((tm, d), lambda i, j: (i, 0)),
            pl.BlockSpec((1, d), lambda i, j: (0, 0)),
            pl.BlockSpec(lb_logits.shape, lambda i, j: (0, 0)),
            pl.BlockSpec((4, d, d), lambda i, j: (0, 0, 0)),
        ],
        out_specs=(pl.BlockSpec((tm, d), lambda i, j: (i, 0)),
                   pl.BlockSpec((1, tm, d), lambda i, j: (jnp.maximum(j - 1, 0), i, 0))),
        scratch_shapes=[pltpu.VMEM((tm, d), BF16)],
        compiler_params=_params(("parallel", "arbitrary"), 48),
        name="hgrn_proj",
    )(x, nw, lb_logits, w4)


def _hgrn_rec_kernel(tc, c, q_ref, f_ref, v_ref, s0_ref, o_ref, so_ref, st_ref):
    t = pl.program_id(1)
    dk = HG_DK

    @pl.when(t == 0)
    def _():
        for h in range(HG_HEADS):
            st_ref[h] = s0_ref[0, h].T

    row = lax.broadcasted_iota(jnp.int32, (c, c), 0)
    col = lax.broadcasted_iota(jnp.int32, (c, c), 1)
    tril = jnp.where(row >= col, 1.0, 0.0).astype(BF16)
    rowl = lax.broadcasted_iota(jnp.int32, (c, dk), 0)
    n0 = min(c, 16)
    same_block = {}
    n = c // 2
    while n >= n0:
        shift = n.bit_length() - 1
        same_block[n] = (row >> shift) == (col >> shift)
        n //= 2

    def block_ref_rows(ch, n, pick):
        parts = [jnp.broadcast_to(ch[j * n + pick:j * n + pick + 1, :], (n, dk)) for j in range(c // n)]
        return parts[0] if len(parts) == 1 else jnp.concatenate(parts, axis=0)

    def do_chunk(qc, fc, vc, store):
        hi, mid, lo = _split3(jnp.log(fc))
        cum = _dot(tril, hi) + _dot(tril, mid) + _dot(tril, lo)
        for h in range(HG_HEADS):
            sl = slice(h * dk, (h + 1) * dk)
            qh, kh, vh, ch = qc[:, sl], 1.0 - fc[:, sl], vc[:, sl], cum[:, sl]
            s_t = st_ref[h]
            o = _dot_nt((qh * jnp.exp(ch)).astype(BF16), s_t.astype(BF16))
            attn = jnp.zeros((c, c), F32)
            n = c
            while n > n0:
                half = n // 2
                b = block_ref_rows(ch, n, half - 1)
                upper = (rowl & (n - 1)) >= half
                e = jnp.exp(jnp.where(upper, ch - b, b - ch))
                qt = jnp.where(upper, qh * e, 0.0).astype(BF16)
                kt = jnp.where(upper, 0.0, kh * e).astype(BF16)
                a = _dot_nt(qt, kt)
                if n < c:
                    a = jnp.where(same_block[n], a, 0.0)
                attn = attn + a
                n = half
            dq = ch - block_ref_rows(ch, n0, n0 // 2 - 1)
            a = _dot_nt((qh * jnp.exp(dq)).astype(BF16), (kh * jnp.exp(-dq)).astype(BF16))
            keep = col <= row
            if n0 < c:
                keep = keep & same_block[n0]
            attn = attn + jnp.where(keep, a, 0.0)
            o = o + _dot(attn.astype(BF16), vh.astype(BF16))
            store(sl, o)
            last = ch[c - 1:c, :]
            kd = kh * jnp.exp(last - ch)
            st_ref[h] = s_t * jnp.exp(last) + _dot_tn(vh.astype(BF16), kd.astype(BF16))

    if tc >= c:
        def body(ci, carry):
            rows = pl.ds(pl.multiple_of(ci * c, c), c)

            def store(sl, o):
                o_ref[rows, sl] = o

            do_chunk(q_ref[0, rows, :].astype(F32), f_ref[rows, :], v_ref[0, rows, :].astype(F32), store)
            return carry

        lax.fori_loop(0, tc // c, body, 0)
    else:
        pad = c - tc
        z = jnp.zeros((pad, q_ref.shape[-1]), F32)

        def store(sl, o):
            o_ref[:, sl] = o[:tc]

        do_chunk(jnp.concatenate([q_ref[0].astype(F32), z], axis=0),
                 jnp.concatenate([f_ref[...], z + 1.0], axis=0),
                 jnp.concatenate([v_ref[0].astype(F32), z], axis=0), store)

    @pl.when(t == pl.num_programs(1) - 1)
    def _():
        for h in range(HG_HEADS):
            so_ref[0, h] = st_ref[h].T


def _hgrn_rec(f, qig, s0, b, t_len, tc, c):
    m, d = f.shape
    nt = t_len // tc
    h, dk, dv = s0.shape[1:]

    def row_spec(which):
        return pl.BlockSpec((1, tc, d), lambda bi, ti: (which, bi * nt + ti, 0))

    return pl.pallas_call(
        functools.partial(_hgrn_rec_kernel, tc, c),
        out_shape=(jax.ShapeDtypeStruct((m, d), F32), jax.ShapeDtypeStruct(s0.shape, F32)),
        grid=(b, nt),
        in_specs=[row_spec(0), pl.BlockSpec((tc, d), lambda bi, ti: (bi * nt + ti, 0)), row_spec(1),
                  pl.BlockSpec((1, h, dk, dv), lambda bi, ti: (bi, 0, 0, 0))],
        out_specs=(pl.BlockSpec((tc, d), lambda bi, ti: (bi * nt + ti, 0)),
                   pl.BlockSpec((1, h, dk, dv), lambda bi, ti: (bi, 0, 0, 0))),
        scratch_shapes=[pltpu.VMEM((h, dv, dk), F32)],
        compiler_params=_params(("parallel", "arbitrary"), 48),
        name="hgrn_rec",
    )(qig, f, qig, s0)


def _hgrn_out_kernel(o_ref, g_ref, gn_ref, x_ref, w_ref, y_ref):
    a = _rms_rows(o_ref[...], gn_ref[...]) * g_ref[0].astype(F32)
    y_ref[...] = x_ref[...] + _dot(a.astype(BF16), w_ref[...])


def _hgrn_out(o, qig, gn, x, w, tm):
    m, d = x.shape
    return pl.pallas_call(
        _hgrn_out_kernel,
        out_shape=jax.ShapeDtypeStruct((m, d), F32),
        grid=(m // tm,),
        in_specs=[
            pl.BlockSpec((tm, d), lambda i: (i, 0)),
            pl.BlockSpec((1, tm, d), lambda i: (2, i, 0)),
            pl.BlockSpec((1, d), lambda i: (0, 0)),
            pl.BlockSpec((tm, d), lambda i: (i, 0)),
            pl.BlockSpec((d, d), lambda i: (0, 0)),
        ],
        out_specs=pl.BlockSpec((tm, d), lambda i: (i, 0)),
        compiler_params=_params(("parallel",), 40),
        name="hgrn_out",
    )(o, qig, gn, x, w)


def _ffn_kernel(x_ref, nw_ref, wu_ref, wd_ref, y_ref, xn_ref):
    j = pl.program_id(1)

    @pl.when(j == 0)
    def _():
        x = x_ref[...]
        xn_ref[...] = _rms_rows(x, nw_ref[...]).astype(BF16)
        y_ref[...] = x

    h = jnp.maximum(_dot(xn_ref[...], wu_ref[...]), 0.0)
    y_ref[...] += _dot((h * h).astype(BF16), wd_ref[...])


def _ffn(x, nw, wu, wd, tm, tf):
    m, d = x.shape
    ff = wu.shape[1]
    return pl.pallas_call(
        _ffn_kernel,
        out_shape=jax.ShapeDtypeStruct((m, d), F32),
        grid=(m // tm, ff // tf),
        in_specs=[
            pl.BlockSpec((tm, d), lambda i, j: (i, 0)),
            pl.BlockSpec((1, d), lambda i, j: (0, 0)),
            pl.BlockSpec((d, tf), lambda i, j: (0, j)),
            pl.BlockSpec((tf, d), lambda i, j: (j, 0)),
        ],
        out_specs=pl.BlockSpec((tm, d), lambda i, j: (i, 0)),
        scratch_shapes=[pltpu.VMEM((tm, d), BF16)],
        compiler_params=_params(("parallel", "arbitrary"), 48),
        name="ffn",
    )(x, nw, wu, wd)


def _qk_norm_rows(y, qkn, gblk):
    y2 = (y * y).astype(BF16)
    w = gblk.shape[0]
    ss = jnp.concatenate([_dot(y2[:, c * w:(c + 1) * w], gblk) for c in range(y.shape[-1] // w)], axis=1)
    return y * lax.rsqrt(ss * (1.0 / ATT_HEAD_DIM) + RMS_EPS) * qkn


def _qkv_perm_kernel(tm, x_ref, nw_ref, w_ref, qkn_ref, gblk_ref, o0_ref, o1_ref, o2_ref, xs_ref, xp_ref):
    j = pl.program_id(1)
    outs = (o0_ref, o1_ref, o2_ref)
    d = x_ref.shape[-1]
    nslab = d // LANES

    @pl.when(j == 0)
    def _():
        xn = _rms_rows(x_ref[...], nw_ref[...])
        xp_ref[0] = xn.astype(BF16)
        for c in range(nslab):
            xs_ref[c] = xn[:, c * LANES:(c + 1) * LANES]
        for g in range(1, N_GROUPS):
            dil = GROUPS[g][1]
            n = tm // dil
            for c in range(nslab):
                for r in range(dil):
                    xp_ref[g, r * n:(r + 1) * n, c * LANES:(c + 1) * LANES] = (
                        xs_ref.at[c][pl.ds(r, n, stride=dil), :].astype(BF16))

    grp = j // 3
    which = j % 3
    y = _dot(xp_ref[grp], w_ref[...])
    val = jnp.where(which < 2, _qk_norm_rows(y, qkn_ref[0], gblk_ref[...]), y).astype(BF16)
    for g in range(N_GROUPS):
        dil = GROUPS[g][1]

        @pl.when(grp == g)
        def _():
            outs[g][0, 0] = val.reshape(dil, tm // dil, d)


def _qkv_perm(x, nw, w, qkn, gblk, b, t_len, tm):
    m, d = x.shape
    nt = t_len // tm
    out_shapes, out_specs = [], []
    for g, (_, dil) in enumerate(GROUPS):
        out_shapes.append(jax.ShapeDtypeStruct((3, b, dil, t_len // dil, d), BF16))
        out_specs.append(pl.BlockSpec(
            (1, 1, dil, tm // dil, d),
            lambda i, j, g=g: (jnp.clip(j - 3 * g, 0, 2), i // nt, 0, i % nt, 0)))
    return pl.pallas_call(
        functools.partial(_qkv_perm_kernel, tm),
        out_shape=tuple(out_shapes),
        grid=(m // tm, 3 * N_GROUPS),
        in_specs=[
            pl.BlockSpec((tm, d), lambda i, j: (i, 0)),
            pl.BlockSpec((1, d), lambda i, j: (0, 0)),
            pl.BlockSpec((d, d), lambda i, j: (0, (j % 3) * N_GROUPS + j // 3)),
            pl.BlockSpec((1, 1, d), lambda i, j: (jnp.minimum(j % 3, 1), 0, 0)),
            pl.BlockSpec(gblk.shape, lambda i, j: (0, 0)),
        ],
        out_specs=tuple(out_specs),
        scratch_shapes=[pltpu.VMEM((d // LANES, tm, LANES), F32),
                        pltpu.VMEM((N_GROUPS, tm, d), BF16)],
        compiler_params=_params(("parallel", "arbitrary"), 48),
        name="qkv_perm",
    )(x, nw, w, qkn, gblk)


def _qkv_plain_kernel(x_ref, nw_ref, w_ref, qkn_ref, gblk_ref, o_ref, xn_ref):
    j = pl.program_id(1)

    @pl.when(j == 0)
    def _():
        xn_ref[...] = _rms_rows(x_ref[...], nw_ref[...]).astype(BF16)

    y = _dot(xn_ref[...], w_ref[...])

    @pl.when(j < 2 * N_GROUPS)
    def _():
        o_ref[...] = _qk_norm_rows(y, qkn_ref[0], gblk_ref[...])

    @pl.when(j >= 2 * N_GROUPS)
    def _():
        o_ref[...] = y


def _qkv_plain(x, nw, w, qkn, gblk, tm):
    m, d = x.shape
    n = w.shape[1]
    return pl.pallas_call(
        _qkv_plain_kernel,
        out_shape=jax.ShapeDtypeStruct((m, n), F32),
        grid=(m // tm, n // d),
        in_specs=[
            pl.BlockSpec((tm, d), lambda i, j: (i, 0)),
            pl.BlockSpec((1, d), lambda i, j: (0, 0)),
            pl.BlockSpec((d, d), lambda i, j: (0, j)),
            pl.BlockSpec((1, 1, d), lambda i, j: (jnp.minimum(j // N_GROUPS, 1), 0, 0)),
            pl.BlockSpec(gblk.shape, lambda i, j: (0, 0)),
        ],
        out_specs=pl.BlockSpec((tm, d), lambda i, j: (i, j)),
        scratch_shapes=[pltpu.VMEM((tm, d), BF16)],
        compiler_params=_params(("parallel", "arbitrary"), 40),
        name="qkv_plain",
    )(x, nw, w, qkn, gblk)


def _kv_tail_kernel(x_ref, nw_ref, wt_ref, kn_ref, o_ref):
    kv = pl.program_id(2)
    xn = _rms_rows(x_ref[...], nw_ref[...]).astype(BF16)
    yt = _dot_nt(wt_ref[0], xn)

    @pl.when(kv == 0)
    def _():
        y3 = yt.reshape(ATT_HEADS, ATT_HEAD_DIM, yt.shape[-1])
        ms = jnp.mean(y3 * y3, axis=1, keepdims=True)
        o_ref[0, 0] = (y3 * lax.rsqrt(ms + RMS_EPS)).reshape(yt.shape) * kn_ref[...]

    @pl.when(kv == 1)
    def _():
        o_ref[0, 0] = yt


def _kv_tail(x, nw, wt, kn_col, b, t_len, keep, tt):
    m, d = x.shape
    first = (t_len - keep) // tt
    per_seq = t_len // tt
    return pl.pallas_call(
        _kv_tail_kernel,
        out_shape=jax.ShapeDtypeStruct((b, 2, d, keep), F32),
        grid=(b, keep // tt, 2),
        in_specs=[
            pl.BlockSpec((tt, d), lambda bi, ti, kv: (bi * per_seq + first + ti, 0)),
            pl.BlockSpec((1, d), lambda bi, ti, kv: (0, 0)),
            pl.BlockSpec((1, d, d), lambda bi, ti, kv: (kv, 0, 0)),
            pl.BlockSpec((d, 1), lambda bi, ti, kv: (0, 0)),
        ],
        out_specs=pl.BlockSpec((1, 1, d, tt), lambda bi, ti, kv: (bi, kv, 0, ti)),
        compiler_params=_params(("parallel", "parallel", "arbitrary"), 40),
        name="kv_tail",
    )(x, nw, wt, kn_col)


def _attn_prompt_kernel(qb, q_ref, kc_ref, kp_ref, vc_ref, vp_ref, bias_ref, o_ref, m_ref, l_ref):
    i = pl.program_id(2)
    first = jnp.where(i == 0, 0, 1)
    lane = lax.broadcasted_iota(jnp.int32, (N_BACK, LANES), 1)
    lo_half = lane < ATT_HEAD_DIM
    for u in range(qb):
        rows = slice(u * N_BACK, (u + 1) * N_BACK)
        m_acc = jnp.zeros((N_BACK, LANES), F32)
        l_acc = jnp.ones((N_BACK, LANES), F32)
        for pr in range(ATT_HEADS // 2):
            sl = slice(pr * LANES, (pr + 1) * LANES)
            qp = q_ref[0, 0, 0, rows, sl]
            if u == 0:
                kprev, vprev = kp_ref[0, 0, 0, :, sl], vp_ref[0, 0, 0, :, sl]
            else:
                prev = slice((u - 1) * N_BACK, u * N_BACK)
                kprev, vprev = kc_ref[0, 0, 0, prev, sl], vc_ref[0, 0, 0, prev, sl]
            kk = jnp.concatenate([kprev, kc_ref[0, 0, 0, rows, sl]], axis=0)
            vv = jnp.concatenate([vprev, vc_ref[0, 0, 0, rows, sl]], axis=0)
            zq = jnp.zeros_like(qp)
            q2 = jnp.concatenate([jnp.where(lo_half, qp, zq), jnp.where(lo_half, zq, qp)], axis=0)
            s = _dot_nt(q2, kk) + (bias_ref[first, pr] if u == 0 else bias_ref[1, pr])
            mx = jnp.max(s, axis=-1, keepdims=True)
            p = jnp.exp2(s - mx)
            l = jnp.sum(p, axis=-1, keepdims=True)
            o2 = _dot(p.astype(BF16), vv)
            o_ref[0, 0, rows, sl] = jnp.where(lo_half, o2[:N_BACK], o2[N_BACK:]).astype(BF16)
            m_acc = jnp.where(lane == 2 * pr, mx[:N_BACK], jnp.where(lane == 2 * pr + 1, mx[N_BACK:], m_acc))
            l_acc = jnp.where(lane == 2 * pr, l[:N_BACK], jnp.where(lane == 2 * pr + 1, l[N_BACK:], l_acc))
        m_ref[0, 0, rows, :] = m_acc
        l_ref[0, 0, rows, :] = l_acc


def _attn_prompt(qkv_g, bias, qb):
    _, b, dil, l, d = qkv_g.shape
    rows = qb * N_BACK

    def cur(which):
        return pl.BlockSpec((1, 1, 1, rows, d), lambda bi, r, i: (which, bi, r, i, 0))

    def prev(which):
        return pl.BlockSpec((1, 1, 1, N_BACK, d),
                            lambda bi, r, i: (which, bi, r, jnp.maximum(i * qb - 1, 0), 0))

    return pl.pallas_call(
        functools.partial(_attn_prompt_kernel, qb),
        out_shape=(jax.ShapeDtypeStruct((b, dil, l, d), BF16),
                   jax.ShapeDtypeStruct((b, dil, l, LANES), F32),
                   jax.ShapeDtypeStruct((b, dil, l, LANES), F32)),
        grid=(b, dil, l // rows),
        in_specs=[cur(0), cur(1), prev(1), cur(2), prev(2),
                  pl.BlockSpec(bias.shape, lambda bi, r, i: (0, 0, 0, 0))],
        out_specs=(pl.BlockSpec((1, 1, rows, d), lambda bi, r, i: (bi, r, i, 0)),
                   pl.BlockSpec((1, 1, rows, LANES), lambda bi, r, i: (bi, r, i, 0)),
                   pl.BlockSpec((1, 1, rows, LANES), lambda bi, r, i: (bi, r, i, 0))),
        compiler_params=_params(("parallel", "parallel", "arbitrary"), 40),
        name="attn_prompt_d%d" % dil,
    )(qkv_g, qkv_g, qkv_g, qkv_g, qkv_g, bias)


def _merge_out_kernel(tm, o0_ref, o1_ref, o2_ref, m0_ref, m1_ref, m2_ref, l0_ref, l1_ref, l2_ref,
                      gexp_ref, x_ref, w_ref, y_ref, os_ref, ms_ref, ls_ref):
    d = x_ref.shape[-1]
    nslab = d // LANES

    def natural(o_ref, m_ref, l_ref, dil):
        if dil == 1:
            return o_ref[0, 0].astype(F32), m_ref[0, 0], l_ref[0, 0]
        n = tm // dil
        for r in range(dil):
            ms_ref[pl.ds(r, n, stride=dil), :] = m_ref[0, r]
            ls_ref[pl.ds(r, n, stride=dil), :] = l_ref[0, r]
            for c in range(nslab):
                os_ref.at[c][pl.ds(r, n, stride=dil), :] = o_ref[0, r, :, c * LANES:(c + 1) * LANES].astype(F32)
        return jnp.concatenate([os_ref[c] for c in range(nslab)], axis=1), ms_ref[...], ls_ref[...]

    trip = [natural(o_ref, m_ref, l_ref, GROUPS[g][1]) for g, (o_ref, m_ref, l_ref) in enumerate(
        ((o0_ref, m0_ref, l0_ref), (o1_ref, m1_ref, l1_ref), (o2_ref, m2_ref, l2_ref)))]
    mx = jnp.maximum(jnp.maximum(trip[0][1], trip[1][1]), trip[2][1])
    es = [jnp.exp2(t[1] - mx) for t in trip]
    inv = 1.0 / (es[0] * trip[0][2] + es[1] * trip[1][2] + es[2] * trip[2][2])
    acc = None
    for e, (o, _, _) in zip(es, trip):
        wt = e * inv
        hi = wt.astype(BF16)
        lo = (wt - hi.astype(F32)).astype(BF16)
        wx = _dot(jnp.concatenate([hi, lo], axis=1), gexp_ref[...])
        acc = wx * o if acc is None else acc + wx * o
    y_ref[...] = x_ref[...] + _dot(acc.astype(BF16), w_ref[...])


def _merge_out(outs, ms, ls, gexp, x, w, t_len, tm):
    m, d = x.shape
    nt = t_len // tm
    row = pl.BlockSpec((tm, d), lambda i: (i, 0))
    o_specs, s_specs = [], []
    for _, dil in GROUPS:
        o_specs.append(pl.BlockSpec((1, dil, tm // dil, d), lambda i: (i // nt, 0, i % nt, 0)))
        s_specs.append(pl.BlockSpec((1, dil, tm // dil, LANES), lambda i: (i // nt, 0, i % nt, 0)))
    return pl.pallas_call(
        functools.partial(_merge_out_kernel, tm),
        out_shape=jax.ShapeDtypeStruct((m, d), F32),
        grid=(m // tm,),
        in_specs=o_specs + s_specs + s_specs + [pl.BlockSpec(gexp.shape, lambda i: (0, 0)), row,
                                                pl.BlockSpec((d, d), lambda i: (0, 0))],
        out_specs=row,
        scratch_shapes=[pltpu.VMEM((d // LANES, tm, LANES), F32), pltpu.VMEM((tm, LANES), F32),
                        pltpu.VMEM((tm, LANES), F32)],
        compiler_params=_params(("parallel",), 40),
        name="merge_out",
    )(*outs, *ms, *ls, gexp, x, w)


SAMPLE_PAIRS = 4


def _attn_sample_kernel(ts, *refs):
    ng = N_GROUPS
    q_refs, k_refs, v_refs = refs[0:ng], refs[ng:2 * ng], refs[2 * ng:3 * ng]
    c_refs = refs[3 * ng:4 * ng]
    bias_ref, o_ref = refs[4 * ng], refs[4 * ng + 1]
    npair = SAMPLE_PAIRS
    lane = lax.broadcasted_iota(jnp.int32, (ts, LANES), 1)
    lo_half = lane < ATT_HEAD_DIM
    zpad = jnp.zeros((LANES - npair * ts, LANES), F32)

    tk, vs = [], []
    for g in range(ng):
        kst = jnp.concatenate([k_refs[g][0, :, p * LANES:(p + 1) * LANES] for p in range(npair)] + [zpad], axis=0)
        vst = jnp.concatenate([v_refs[g][0, :, p * LANES:(p + 1) * LANES] for p in range(npair)] + [zpad], axis=0)
        tk.append(kst.T.astype(BF16))
        vs.append(vst.astype(BF16))

    rows = []
    for p in range(npair):
        sl = slice(p * LANES, (p + 1) * LANES)
        old, new = [], []
        for g in range(ng):
            qp = q_refs[g][0, :, sl]
            q2 = jnp.concatenate([jnp.where(lo_half, qp, 0.0), jnp.where(lo_half, 0.0, qp)], axis=0).astype(BF16)
            kt = c_refs[g][0, 0, 2 * p:2 * p + 2].reshape(LANES, c_refs[g].shape[-1]).astype(BF16)
            old.append(_dot(q2, kt))
            new.append(_dot(q2, tk[g]))
        rows.append(jnp.concatenate(old + new, axis=1))
    s = jnp.concatenate(rows, axis=0) + bias_ref[...].reshape(npair * 2 * ts, -1)
    mx = jnp.max(s, axis=-1, keepdims=True)
    pr = jnp.exp2(s - mx)
    inv = 1.0 / jnp.sum(pr, axis=-1, keepdims=True)
    prb = pr.astype(BF16)

    for p in range(npair):
        sl = slice(p * LANES, (p + 1) * LANES)
        r0 = p * 2 * ts
        acc = jnp.zeros((2 * ts, LANES), F32)
        off = 0
        for g in range(ng):
            window = c_refs[g].shape[-1]
            vt = c_refs[g][0, 1, 2 * p:2 * p + 2].reshape(LANES, window).astype(BF16)
            acc = acc + _dot_nt(prb[r0:r0 + 2 * ts, off:off + window], vt)
            off += window
        for g in range(ng):
            acc = acc + _dot(prb[r0:r0 + 2 * ts, off:off + LANES], vs[g])
            off += LANES
        om = acc * inv[r0:r0 + 2 * ts]
        o_ref[0, :, sl] = jnp.where(lo_half, om[:ts], om[ts:])


def _attn_sample(qkv32, caches_t, bias):
    b, ts, _ = qkv32.shape
    d = ATT_HEADS * ATT_HEAD_DIM
    width = SAMPLE_PAIRS * LANES
    per_blk = d // width
    hb = 2 * SAMPLE_PAIRS

    def new_rows(which, g):
        return pl.BlockSpec((1, ts, width), lambda bi, hq: (bi, 0, (which * N_GROUPS + g) * per_blk + hq))

    in_specs = [new_rows(w, g) for w in range(3) for g in range(N_GROUPS)]
    for c in caches_t:
        in_specs.append(pl.BlockSpec((1, 2, hb, ATT_HEAD_DIM, c.shape[-1]), lambda bi, hq: (bi, 0, hq, 0, 0)))
    in_specs.append(pl.BlockSpec((SAMPLE_PAIRS,) + bias.shape[1:], lambda bi, hq: (hq, 0, 0)))
    return pl.pallas_call(
        functools.partial(_attn_sample_kernel, ts),
        out_shape=jax.ShapeDtypeStruct((b, ts, d), F32),
        grid=(b, per_blk),
        in_specs=in_specs,
        out_specs=pl.BlockSpec((1, ts, width), lambda bi, hq: (bi, 0, hq)),
        compiler_params=_params(("parallel", "arbitrary"), 56),
        name="attn_sample",
    )(*([qkv32] * (3 * N_GROUPS)), *caches_t, bias)


def _matmul_res_kernel(a_ref, x_ref, w_ref, y_ref):
    y_ref[...] = x_ref[...] + _dot(a_ref[...].astype(BF16), w_ref[...])


def _matmul_res(a, x, w, tm):
    m, d = x.shape
    row = pl.BlockSpec((tm, d), lambda i: (i, 0))
    return pl.pallas_call(
        _matmul_res_kernel,
        out_shape=jax.ShapeDtypeStruct((m, d), F32),
        grid=(m // tm,),
        in_specs=[row, row, pl.BlockSpec((d, d), lambda i: (0, 0))],
        out_specs=row,
        compiler_params=_params(("parallel",), 40),
        name="matmul_res",
    )(a, x, w)


def _alibi_slopes():
    n = N_GROUPS * ATT_HEADS
    return LOG2E * (2.0 ** (-8.0 * np.arange(1, n + 1) / n)).reshape(N_GROUPS, ATT_HEADS)


def _prompt_bias(g):
    slopes = _alibi_slopes()[g] * GROUPS[g][1]
    dist = np.arange(N_BACK)[:, None] - np.arange(2 * N_BACK)[None, :] + N_BACK
    band = (dist >= 0) & (dist <= N_BACK)
    has_prev = np.arange(2 * N_BACK)[None, :] >= N_BACK
    out = np.empty((2, ATT_HEADS, N_BACK, 2 * N_BACK), np.float32)
    for k, valid in enumerate((band & has_prev, band)):
        out[k] = np.where(valid[None], -slopes[:, None, None] * dist[None], NEG)
    return jnp.asarray(out.reshape(2, ATT_HEADS // 2, 2 * N_BACK, 2 * N_BACK))


def _sample_bias(ts):
    slopes = _alibi_slopes()
    npairs = ATT_HEADS // 2
    old, new = [], []
    for g, (window, dil) in enumerate(GROUPS):
        pos = np.arange(window)
        bc = np.full((npairs, 2 * ts, window), NEG, np.float32)
        bn = np.full((npairs, 2 * ts, LANES), NEG, np.float32)
        for h in range(ATT_HEADS):
            for s in range(ts):
                row = (h % 2) * ts + s
                back = window + s - pos
                valid = (back % dil == 0) & (back <= window)
                bc[h // 2, row] = np.where(valid, -slopes[g, h] * back, NEG)
                lane0 = ((h // 2) % SAMPLE_PAIRS) * ts
                for s2 in range(s + 1):
                    if (s - s2) % dil == 0:
                        bn[h // 2, row, lane0 + s2] = -slopes[g, h] * (s - s2)
        old.append(bc)
        new.append(bn)
    return jnp.asarray(np.concatenate(old + new, axis=-1))


def _head_group_matrices():
    d = ATT_HEADS * ATT_HEAD_DIM
    i = np.arange(2 * LANES)
    gblk = (i[:, None] // ATT_HEAD_DIM == i[None, :] // ATT_HEAD_DIM).astype(np.float32)
    gexp = (np.arange(LANES)[:, None] == np.arange(d)[None, :] // ATT_HEAD_DIM).astype(np.float32)
    return jnp.asarray(gblk, BF16), jnp.asarray(np.concatenate([gexp, gexp], axis=0), BF16)


def _tile(m, want):
    return want if m % want == 0 else m


def kernel(x_prompt, x_sample, state_hgrn, cache_kv_w128, cache_kv_w512, cache_kv_w2048, hg_lb_logits, hg_w_q, hg_w_f, hg_w_i, hg_w_g, hg_w_o, hg_norm_o, att_w_qkv, att_w_o, att_q_norm, att_k_norm, norm_mix, norm_ffn, ffn_w_up, ffn_w_down):
    b, t_len, d = x_prompt.shape
    bs, ts, _ = x_sample.shape
    depth = norm_mix.shape[0]
    caches_all = (cache_kv_w128, cache_kv_w512, cache_kv_w2048)
    gblk, gexp = _head_group_matrices()
    sample_bias = _sample_bias(ts)

    yp = x_prompt.reshape(b * t_len, d)
    ys = x_sample.reshape(bs * ts, d)
    tm_p = _tile(b * t_len, 512)
    tm_big = _tile(t_len, 1024)
    tm_s = _tile(bs * ts, 256)
    hg_p, hg_s = [], []
    kv_p = [[] for _ in GROUPS]
    kv_s = [[] for _ in GROUPS]
    for layer in range(depth):
        a = layer // 2
        nw = norm_mix[layer][None]
        if layer % 2 == 0:
            w4 = jnp.stack([hg_w_f[a], hg_w_q[a], hg_w_i[a], hg_w_g[a]]).astype(BF16)
            wo = hg_w_o[a].astype(BF16)
            gn = hg_norm_o[a][None]
            zeros = jnp.zeros((b,) + state_hgrn.shape[2:], F32)
            new = []
            for y, s0, nb_, tl, tm, tc in ((yp, zeros, b, t_len, tm_p, _tile(t_len, 512)),
                                            (ys, state_hgrn[a], bs, ts, tm_s, ts)):
                f, qig = _hgrn_proj(y, nw, hg_lb_logits, w4, layer, tm, BF16 if tc % 16 == 0 else F32)
                o, st = _hgrn_rec(f, qig, s0, nb_, tl, tc, 128 if tc >= 128 else 16)
                new.append((_hgrn_out(o, qig, gn, y, wo, tm), st))
            (yp, sp), (ys, ss) = new
            hg_p.append(sp)
            hg_s.append(ss)
        else:
            wqkv = att_w_qkv[a].astype(BF16)
            wo = att_w_o[a].astype(BF16)
            qn_row = jnp.tile(att_q_norm[a], ATT_HEADS)
            kn_row = jnp.tile(att_k_norm[a], ATT_HEADS)
            qkn = jnp.stack([qn_row * (ATT_SCALE * LOG2E), kn_row])[:, None, :]
            qkv_groups = _qkv_perm(yp, nw, wqkv, qkn, gblk, b, t_len, tm_big)
            outs, ms, ls = [], [], []
            for g, (window, dil) in enumerate(GROUPS):
                qb = 2 if (t_len // dil) % (2 * N_BACK) == 0 else 1
                o, mrow, lrow = _attn_prompt(qkv_groups[g], _prompt_bias(g), qb)
                outs.append(o)
                ms.append(mrow)
                ls.append(lrow)
                keep = min(window, t_len)
                wt = jnp.stack([wqkv[:, (N_GROUPS + g) * d:(N_GROUPS + g + 1) * d].T,
                                wqkv[:, (2 * N_GROUPS + g) * d:(2 * N_GROUPS + g + 1) * d].T])
                kvt = _kv_tail(yp, nw, wt, kn_row[:, None], b, t_len, keep, _tile(keep, 512))
                kvt = kvt.reshape(b, 2, ATT_HEADS, ATT_HEAD_DIM, keep)
                kv_p[g].append(jnp.transpose(kvt, (0, 4, 1, 2, 3)))
            yp = _merge_out(outs, ms, ls, gexp, yp, wo, t_len, tm_p)
            qkv32 = _qkv_plain(ys, nw, wqkv, qkn, gblk, tm_s).reshape(bs, ts, -1)
            caches_t = [jnp.transpose(c[a], (0, 2, 3, 4, 1)) for c in caches_all]
            om = _attn_sample(qkv32, caches_t, sample_bias)
            ys = _matmul_res(om.reshape(bs * ts, d), ys, wo, tm_s)
            kv = qkv32.reshape(bs, ts, 3, N_GROUPS, ATT_HEADS, ATT_HEAD_DIM)
            for g in range(N_GROUPS):
                kv_s[g].append(kv[:, :, 1:, g])
        nf = norm_ffn[layer][None]
        wu = ffn_w_up[layer].astype(BF16)
        wd = ffn_w_down[layer].astype(BF16)
        yp = _ffn(yp, nf, wu, wd, tm_big, 1024)
        ys = _ffn(ys, nf, wu, wd, tm_s, 1024)
    return (yp.reshape(b, t_len, d), ys.reshape(bs, ts, d), jnp.stack(hg_p), jnp.stack(hg_s),
            jnp.stack(kv_p[0]), jnp.stack(kv_s[0]),
            jnp.stack(kv_p[1]), jnp.stack(kv_s[1]),
            jnp.stack(kv_p[2]), jnp.stack(kv_s[2]))
```

```python
import functools

import numpy as np
import jax
import jax.numpy as jnp
from jax import lax
from jax.experimental import pallas as pl
from jax.experimental.pallas import tpu as pltpu

F32 = jnp.float32
BF16 = jnp.bfloat16

RMS_EPS = 1e-6
HG_HEADS = 8
HG_DK = 128
GROUPS = ((128, 1), (512, 4), (2048, 16))
N_GROUPS = len(GROUPS)
ATT_HEADS = 16
ATT_HEAD_DIM = 64
ATT_SCALE = ATT_HEAD_DIM ** -0.5
N_BACK = 128
NEG = -1e30
LOG2E = 1.4426950408889634

LANES = 128
MIB = 1024 * 1024


def _dot(a, b):
    return jnp.dot(a, b, preferred_element_type=F32)


def _dot_nt(a, b):
    return lax.dot_general(a, b, (((1,), (1,)), ((), ())), preferred_element_type=F32)


def _dot_tn(a, b):
    return lax.dot_general(a, b, (((0,), (0,)), ((), ())), preferred_element_type=F32)


def _split3(x):
    hi = x.astype(BF16)
    r1 = x - hi.astype(F32)
    mid = r1.astype(BF16)
    lo = (r1 - mid.astype(F32)).astype(BF16)
    return hi, mid, lo


def _rms_rows(x, w):
    ms = jnp.mean(x * x, axis=-1, keepdims=True)
    return x * lax.rsqrt(ms + RMS_EPS) * w


def _params(sem, vmem_mib):
    return pltpu.CompilerParams(dimension_semantics=sem, vmem_limit_bytes=vmem_mib * MIB)


def _hgrn_proj_kernel(layer, x_ref, nw_ref, lbl_ref, w_ref, f_ref, o_ref, xn_ref):
    j = pl.program_id(1)

    @pl.when(j == 0)
    def _():
        xn_ref[...] = _rms_rows(x_ref[...], nw_ref[...]).astype(BF16)

    y = _dot(xn_ref[...], w_ref[j])
    lg = lbl_ref[...]
    e = jnp.exp(lg - jnp.max(lg, axis=0, keepdims=True))
    lb = jnp.sum(e[:layer + 1], axis=0, keepdims=True) / jnp.sum(e, axis=0, keepdims=True)
    sig = jax.nn.sigmoid(y)
    val = jnp.where(j == 2, y, jnp.where(j == 0, lb + (1.0 - lb) * sig, y * sig))
    o_ref[0] = val.astype(o_ref.dtype)

    @pl.when(j == 0)
    def _():
        f_ref[...] = val


def _hgrn_proj(x, nw, lb_logits, w4, layer, tm, qig_dtype):
    m, d = x.shape
    return pl.pallas_call(
        functools.partial(_hgrn_proj_kernel, layer),
        out_shape=(jax.ShapeDtypeStruct((m, d), F32), jax.ShapeDtypeStruct((3, m, d), qig_dtype)),
        grid=(m // tm, 4),
        in_specs=[
            pl.BlockSpec((tm, d), lambda i, j: (i, 0)),
            pl.BlockSpec((1, d), lambda i, j: (0, 0)),
            pl.BlockSpec(lb_logits.shape, lambda i, j: (0, 0)),
            pl.BlockSpec((4, d, d), lambda i, j: (0, 0, 0)),
        ],
        out_specs=(pl.BlockSpec((tm, d), lambda i, j: (i, 0)),
                   pl.BlockSpec((1, tm, d), lambda i, j: (jnp.maximum(j - 1, 0), i, 0))),
        scratch_shapes=[pltpu.VMEM((tm, d), BF16)],
        compiler_params=_params(("parallel", "arbitrary"), 48),
        name="hgrn_proj",
    )(x, nw, lb_logits, w4)


def _hgrn_rec_kernel(tc, c, q_ref, f_ref, v_ref, s0_ref, o_ref, so_ref, st_ref):
    t = pl.program_id(1)
    dk = HG_DK

    @pl.when(t == 0)
    def _():
        for h in range(HG_HEADS):
            st_ref[h] = s0_ref[0, h].T

    row = lax.broadcasted_iota(jnp.int32, (c, c), 0)
    col = lax.broadcasted_iota(jnp.int32, (c, c), 1)
    tril = jnp.where(row >= col, 1.0, 0.0).astype(BF16)
    rowl = lax.broadcasted_iota(jnp.int32, (c, dk), 0)
    n0 = min(c, 16)
    same_block = {}
    n = c // 2
    while n >= n0:
        shift = n.bit_length() - 1
        same_block[n] = (row >> shift) == (col >> shift)
        n //= 2

    def block_ref_rows(ch, n, pick):
        parts = [jnp.broadcast_to(ch[j * n + pick:j * n + pick + 1, :], (n, dk)) for j in range(c // n)]
        return parts[0] if len(parts) == 1 else jnp.concatenate(parts, axis=0)

    def do_chunk(qc, fc, vc, store):
        hi, mid, lo = _split3(jnp.log(fc))
        cum = _dot(tril, hi) + _dot(tril, mid) + _dot(tril, lo)
        for h in range(HG_HEADS):
            sl = slice(h * dk, (h + 1) * dk)
            qh, kh, vh, ch = qc[:, sl], 1.0 - fc[:, sl], vc[:, sl], cum[:, sl]
            s_t = st_ref[h]
            o = _dot_nt((qh * jnp.exp(ch)).astype(BF16), s_t.astype(BF16))
            attn = jnp.zeros((c, c), F32)
            n = c
            while n > n0:
                half = n // 2
                b = block_ref_rows(ch, n, half - 1)
                upper = (rowl & (n - 1)) >= half
                e = jnp.exp(jnp.where(upper, ch - b, b - ch))
                qt = jnp.where(upper, qh * e, 0.0).astype(BF16)
                kt = jnp.where(upper, 0.0, kh * e).astype(BF16)
                a = _dot_nt(qt, kt)
                if n < c:
                    a = jnp.where(same_block[n], a, 0.0)
                attn = attn + a
                n = half
            dq = ch - block_ref_rows(ch, n0, n0 // 2 - 1)
            a = _dot_nt((qh * jnp.exp(dq)).astype(BF16), (kh * jnp.exp(-dq)).astype(BF16))
            keep = col <= row
            if n0 < c:
                keep = keep & same_block[n0]
            attn = attn + jnp.where(keep, a, 0.0)
            o = o + _dot(attn.astype(BF16), vh.astype(BF16))
            store(sl, o)
            last = ch[c - 1:c, :]
            kd = kh * jnp.exp(last - ch)
            st_ref[h] = s_t * jnp.exp(last) + _dot_tn(vh.astype(BF16), kd.astype(BF16))

    if tc >= c:
        def body(ci, carry):
            rows = pl.ds(pl.multiple_of(ci * c, c), c)

            def store(sl, o):
                o_ref[rows, sl] = o

            do_chunk(q_ref[0, rows, :].astype(F32), f_ref[rows, :], v_ref[0, rows, :].astype(F32), store)
            return carry

        lax.fori_loop(0, tc // c, body, 0)
    else:
        pad = c - tc
        z = jnp.zeros((pad, q_ref.shape[-1]), F32)

        def store(sl, o):
            o_ref[:, sl] = o[:tc]

        do_chunk(jnp.concatenate([q_ref[0].astype(F32), z], axis=0),
                 jnp.concatenate([f_ref[...], z + 1.0], axis=0),
                 jnp.concatenate([v_ref[0].astype(F32), z], axis=0), store)

    @pl.when(t == pl.num_programs(1) - 1)
    def _():
        for h in range(HG_HEADS):
            so_ref[0, h] = st_ref[h].T


def _hgrn_rec(f, qig, s0, b, t_len, tc, c):
    m, d = f.shape
    nt = t_len // tc
    h, dk, dv = s0.shape[1:]

    def row_spec(which):
        return pl.BlockSpec((1, tc, d), lambda bi, ti: (which, bi * nt + ti, 0))

    return pl.pallas_call(
        functools.partial(_hgrn_rec_kernel, tc, c),
        out_shape=(jax.ShapeDtypeStruct((m, d), F32), jax.ShapeDtypeStruct(s0.shape, F32)),
        grid=(b, nt),
        in_specs=[row_spec(0), pl.BlockSpec((tc, d), lambda bi, ti: (bi * nt + ti, 0)), row_spec(1),
                  pl.BlockSpec((1, h, dk, dv), lambda bi, ti: (bi, 0, 0, 0))],
        out_specs=(pl.BlockSpec((tc, d), lambda bi, ti: (bi * nt + ti, 0)),
                   pl.BlockSpec((1, h, dk, dv), lambda bi, ti: (bi, 0, 0, 0))),
        scratch_shapes=[pltpu.VMEM((h, dv, dk), F32)],
        compiler_params=_params(("parallel", "arbitrary"), 48),
        name="hgrn_rec",
    )(qig, f, qig, s0)


def _hgrn_out_kernel(o_ref, g_ref, gn_ref, x_ref, w_ref, y_ref):
    a = _rms_rows(o_ref[...], gn_ref[...]) * g_ref[0].astype(F32)
    y_ref[...] = x_ref[...] + _dot(a.astype(BF16), w_ref[...])


def _hgrn_out(o, qig, gn, x, w, tm):
    m, d = x.shape
    return pl.pallas_call(
        _hgrn_out_kernel,
        out_shape=jax.ShapeDtypeStruct((m, d), F32),
        grid=(m // tm,),
        in_specs=[
            pl.BlockSpec((tm, d), lambda i: (i, 0)),
            pl.BlockSpec((1, tm, d), lambda i: (2, i, 0)),
            pl.BlockSpec((1, d), lambda i: (0, 0)),
            pl.BlockSpec((tm, d), lambda i: (i, 0)),
            pl.BlockSpec((d, d), lambda i: (0, 0)),
        ],
        out_specs=pl.BlockSpec((tm, d), lambda i: (i, 0)),
        compiler_params=_params(("parallel",), 40),
        name="hgrn_out",
    )(o, qig, gn, x, w)


def _ffn_kernel(x_ref, nw_ref, wu_ref, wd_ref, y_ref, xn_ref):
    j = pl.program_id(1)

    @pl.when(j == 0)
    def _():
        x = x_ref[...]
        xn_ref[...] = _rms_rows(x, nw_ref[...]).astype(BF16)
        y_ref[...] = x

    h = jnp.maximum(_dot(xn_ref[...], wu_ref[0].astype(BF16)), 0.0)
    y_ref[...] += _dot((h * h).astype(BF16), wd_ref[0].astype(BF16))


def _ffn(x, nw, wu, wd, layer, tm, tf):
    m, d = x.shape
    ff = wu.shape[-1]
    return pl.pallas_call(
        _ffn_kernel,
        out_shape=jax.ShapeDtypeStruct((m, d), F32),
        grid=(m // tm, ff // tf),
        in_specs=[
            pl.BlockSpec((tm, d), lambda i, j: (i, 0)),
            pl.BlockSpec((1, d), lambda i, j: (0, 0)),
            pl.BlockSpec((1, d, tf), lambda i, j: (layer, 0, j)),
            pl.BlockSpec((1, tf, d), lambda i, j: (layer, j, 0)),
        ],
        out_specs=pl.BlockSpec((tm, d), lambda i, j: (i, 0)),
        scratch_shapes=[pltpu.VMEM((tm, d), BF16)],
        compiler_params=_params(("parallel", "arbitrary"), 48),
        name="ffn",
    )(x, nw, wu, wd)


def _qk_norm_rows(y, qkn, gblk):
    y2 = (y * y).astype(BF16)
    w = gblk.shape[0]
    ss = jnp.concatenate([_dot(y2[:, c * w:(c + 1) * w], gblk) for c in range(y.shape[-1] // w)], axis=1)
    return y * lax.rsqrt(ss * (1.0 / ATT_HEAD_DIM) + RMS_EPS) * qkn


def _qkv_perm_kernel(tm, x_ref, nw_ref, w_ref, qkn_ref, gblk_ref, o0_ref, o1_ref, o2_ref, xs_ref, xp_ref):
    j = pl.program_id(1)
    outs = (o0_ref, o1_ref, o2_ref)
    d = x_ref.shape[-1]
    nslab = d // LANES

    @pl.when(j == 0)
    def _():
        xn = _rms_rows(x_ref[...], nw_ref[...])
        xp_ref[0] = xn.astype(BF16)
        for c in range(nslab):
            xs_ref[c] = xn[:, c * LANES:(c + 1) * LANES]
        for g in range(1, N_GROUPS):
            dil = GROUPS[g][1]
            n = tm // dil
            for c in range(nslab):
                for r in range(dil):
                    xp_ref[g, r * n:(r + 1) * n, c * LANES:(c + 1) * LANES] = (
                        xs_ref.at[c][pl.ds(r, n, stride=dil), :].astype(BF16))

    grp = j // 3
    which = j % 3
    y = _dot(xp_ref[grp], w_ref[0].astype(BF16))
    val = jnp.where(which < 2, _qk_norm_rows(y, qkn_ref[0], gblk_ref[...]), y).astype(BF16)
    for g in range(N_GROUPS):
        dil = GROUPS[g][1]

        @pl.when(grp == g)
        def _():
            outs[g][0, 0] = val.reshape(dil, tm // dil, d)


def _qkv_perm(x, nw, w, layer, qkn, gblk, b, t_len, tm):
    m, d = x.shape
    nt = t_len // tm
    out_shapes, out_specs = [], []
    for g, (_, dil) in enumerate(GROUPS):
        out_shapes.append(jax.ShapeDtypeStruct((3, b, dil, t_len // dil, d), BF16))
        out_specs.append(pl.BlockSpec(
            (1, 1, dil, tm // dil, d),
            lambda i, j, g=g: (jnp.clip(j - 3 * g, 0, 2), i // nt, 0, i % nt, 0)))
    return pl.pallas_call(
        functools.partial(_qkv_perm_kernel, tm),
        out_shape=tuple(out_shapes),
        grid=(m // tm, 3 * N_GROUPS),
        in_specs=[
            pl.BlockSpec((tm, d), lambda i, j: (i, 0)),
            pl.BlockSpec((1, d), lambda i, j: (0, 0)),
            pl.BlockSpec((1, d, d), lambda i, j: (layer, 0, (j % 3) * N_GROUPS + j // 3)),
            pl.BlockSpec((1, 1, d), lambda i, j: (jnp.minimum(j % 3, 1), 0, 0)),
            pl.BlockSpec(gblk.shape, lambda i, j: (0, 0)),
        ],
        out_specs=tuple(out_specs),
        scratch_shapes=[pltpu.VMEM((d // LANES, tm, LANES), F32),
                        pltpu.VMEM((N_GROUPS, tm, d), BF16)],
        compiler_params=_params(("parallel", "arbitrary"), 56),
        name="qkv_perm",
    )(x, nw, w, qkn, gblk)


def _qkv_plain_kernel(x_ref, nw_ref, w_ref, qkn_ref, gblk_ref, o_ref, xn_ref):
    j = pl.program_id(1)

    @pl.when(j == 0)
    def _():
        xn_ref[...] = _rms_rows(x_ref[...], nw_ref[...]).astype(BF16)

    y = _dot(xn_ref[...], w_ref[0].astype(BF16))

    @pl.when(j < 2 * N_GROUPS)
    def _():
        o_ref[...] = _qk_norm_rows(y, qkn_ref[0], gblk_ref[...])

    @pl.when(j >= 2 * N_GROUPS)
    def _():
        o_ref[...] = y


def _qkv_plain(x, nw, w, layer, qkn, gblk, tm):
    m, d = x.shape
    n = w.shape[-1]
    return pl.pallas_call(
        _qkv_plain_kernel,
        out_shape=jax.ShapeDtypeStruct((m, n), F32),
        grid=(m // tm, n // d),
        in_specs=[
            pl.BlockSpec((tm, d), lambda i, j: (i, 0)),
            pl.BlockSpec((1, d), lambda i, j: (0, 0)),
            pl.BlockSpec((1, d, d), lambda i, j: (layer, 0, j)),
            pl.BlockSpec((1, 1, d), lambda i, j: (jnp.minimum(j // N_GROUPS, 1), 0, 0)),
            pl.BlockSpec(gblk.shape, lambda i, j: (0, 0)),
        ],
        out_specs=pl.BlockSpec((tm, d), lambda i, j: (i, j)),
        scratch_shapes=[pltpu.VMEM((tm, d), BF16)],
        compiler_params=_params(("parallel", "arbitrary"), 40),
        name="qkv_plain",
    )(x, nw, w, qkn, gblk)


def _kv_tail_kernel(x_ref, nw_ref, wt_ref, kn_ref, o_ref, xn_ref):
    kv = pl.program_id(2)

    @pl.when(kv == 0)
    def _():
        xn_ref[...] = _rms_rows(x_ref[...], nw_ref[...]).astype(BF16)

    yt = _dot_nt(wt_ref[0], xn_ref[...])

    @pl.when(kv == 0)
    def _():
        y3 = yt.reshape(ATT_HEADS, ATT_HEAD_DIM, yt.shape[-1])
        ms = jnp.mean(y3 * y3, axis=1, keepdims=True)
        o_ref[0, 0] = (y3 * lax.rsqrt(ms + RMS_EPS)).reshape(yt.shape) * kn_ref[...]

    @pl.when(kv == 1)
    def _():
        o_ref[0, 0] = yt


def _kv_tail(x, nw, wt, kn_col, b, t_len, keep, tt):
    m, d = x.shape
    first = (t_len - keep) // tt
    per_seq = t_len // tt
    return pl.pallas_call(
        _kv_tail_kernel,
        out_shape=jax.ShapeDtypeStruct((b, 2, d, keep), F32),
        grid=(b, keep // tt, 2),
        in_specs=[
            pl.BlockSpec((tt, d), lambda bi, ti, kv: (bi * per_seq + first + ti, 0)),
            pl.BlockSpec((1, d), lambda bi, ti, kv: (0, 0)),
            pl.BlockSpec((1, d, d), lambda bi, ti, kv: (kv, 0, 0)),
            pl.BlockSpec((d, 1), lambda bi, ti, kv: (0, 0)),
        ],
        out_specs=pl.BlockSpec((1, 1, d, tt), lambda bi, ti, kv: (bi, kv, 0, ti)),
        scratch_shapes=[pltpu.VMEM((tt, d), BF16)],
        compiler_params=_params(("parallel", "parallel", "arbitrary"), 40),
        name="kv_tail",
    )(x, nw, wt, kn_col)


def _attn_prompt_kernel(qb, q_ref, kc_ref, kp_ref, vc_ref, vp_ref, bias_ref, o_ref, m_ref, l_ref):
    i = pl.program_id(2)
    first = jnp.where(i == 0, 0, 1)
    lane = lax.broadcasted_iota(jnp.int32, (N_BACK, LANES), 1)
    lo_half = lane < ATT_HEAD_DIM
    for u in range(qb):
        rows = slice(u * N_BACK, (u + 1) * N_BACK)
        m_acc = jnp.zeros((N_BACK, LANES), F32)
        l_acc = jnp.ones((N_BACK, LANES), F32)
        for pr in range(ATT_HEADS // 2):
            sl = slice(pr * LANES, (pr + 1) * LANES)
            qp = q_ref[0, 0, 0, rows, sl]
            if u == 0:
                kprev, vprev = kp_ref[0, 0, 0, :, sl], vp_ref[0, 0, 0, :, sl]
            else:
                prev = slice((u - 1) * N_BACK, u * N_BACK)
                kprev, vprev = kc_ref[0, 0, 0, prev, sl], vc_ref[0, 0, 0, prev, sl]
            kk = jnp.concatenate([kprev, kc_ref[0, 0, 0, rows, sl]], axis=0)
            vv = jnp.concatenate([vprev, vc_ref[0, 0, 0, rows, sl]], axis=0)
            zq = jnp.zeros_like(qp)
            q2 = jnp.concatenate([jnp.where(lo_half, qp, zq), jnp.where(lo_half, zq, qp)], axis=0)
            s = _dot_nt(q2, kk) + (bias_ref[first, pr] if u == 0 else bias_ref[1, pr])
            mx = jnp.max(s, axis=-1, keepdims=True)
            p = jnp.exp2(s - mx)
            l = jnp.sum(p, axis=-1, keepdims=True)
            o2 = _dot(p.astype(BF16), vv)
            o_ref[0, 0, rows, sl] = jnp.where(lo_half, o2[:N_BACK], o2[N_BACK:]).astype(BF16)
            m_acc = jnp.where(lane == 2 * pr, mx[:N_BACK], jnp.where(lane == 2 * pr + 1, mx[N_BACK:], m_acc))
            l_acc = jnp.where(lane == 2 * pr, l[:N_BACK], jnp.where(lane == 2 * pr + 1, l[N_BACK:], l_acc))
        m_ref[0, 0, rows, :] = m_acc
        l_ref[0, 0, rows, :] = l_acc


def _attn_prompt(qkv_g, bias, qb):
    _, b, dil, l, d = qkv_g.shape
    rows = qb * N_BACK

    def cur(which):
        return pl.BlockSpec((1, 1, 1, rows, d), lambda bi, r, i: (which, bi, r, i, 0))

    def prev(which):
        return pl.BlockSpec((1, 1, 1, N_BACK, d),
                            lambda bi, r, i: (which, bi, r, jnp.maximum(i * qb - 1, 0), 0))

    return pl.pallas_call(
        functools.partial(_attn_prompt_kernel, qb),
        out_shape=(jax.ShapeDtypeStruct((b, dil, l, d), BF16),
                   jax.ShapeDtypeStruct((b, dil, l, LANES), F32),
                   jax.ShapeDtypeStruct((b, dil, l, LANES), F32)),
        grid=(b, dil, l // rows),
        in_specs=[cur(0), cur(1), prev(1), cur(2), prev(2),
                  pl.BlockSpec(bias.shape, lambda bi, r, i: (0, 0, 0, 0))],
        out_specs=(pl.BlockSpec((1, 1, rows, d), lambda bi, r, i: (bi, r, i, 0)),
                   pl.BlockSpec((1, 1, rows, LANES), lambda bi, r, i: (bi, r, i, 0)),
                   pl.BlockSpec((1, 1, rows, LANES), lambda bi, r, i: (bi, r, i, 0))),
        compiler_params=_params(("parallel", "parallel", "arbitrary"), 40),
        name="attn_prompt_d%d" % dil,
    )(qkv_g, qkv_g, qkv_g, qkv_g, qkv_g, bias)


def _merge_out_kernel(tm, o0_ref, o1_ref, o2_ref, m0_ref, m1_ref, m2_ref, l0_ref, l1_ref, l2_ref,
                      gexp_ref, x_ref, w_ref, y_ref, os_ref, ms_ref, ls_ref):
    d = x_ref.shape[-1]
    nslab = d // LANES

    def natural(o_ref, m_ref, l_ref, dil):
        if dil == 1:
            return o_ref[0, 0].astype(F32), m_ref[0, 0], l_ref[0, 0]
        n = tm // dil
        for r in range(dil):
            ms_ref[pl.ds(r, n, stride=dil), :] = m_ref[0, r]
            ls_ref[pl.ds(r, n, stride=dil), :] = l_ref[0, r]
            for c in range(nslab):
                os_ref.at[c][pl.ds(r, n, stride=dil), :] = o_ref[0, r, :, c * LANES:(c + 1) * LANES].astype(F32)
        return jnp.concatenate([os_ref[c] for c in range(nslab)], axis=1), ms_ref[...], ls_ref[...]

    trip = [natural(o_ref, m_ref, l_ref, GROUPS[g][1]) for g, (o_ref, m_ref, l_ref) in enumerate(
        ((o0_ref, m0_ref, l0_ref), (o1_ref, m1_ref, l1_ref), (o2_ref, m2_ref, l2_ref)))]
    mx = jnp.maximum(jnp.maximum(trip[0][1], trip[1][1]), trip[2][1])
    es = [jnp.exp2(t[1] - mx) for t in trip]
    inv = 1.0 / (es[0] * trip[0][2] + es[1] * trip[1][2] + es[2] * trip[2][2])
    acc = None
    for e, (o, _, _) in zip(es, trip):
        wt = e * inv
        hi = wt.astype(BF16)
        lo = (wt - hi.astype(F32)).astype(BF16)
        wx = _dot(jnp.concatenate([hi, lo], axis=1), gexp_ref[...])
        acc = wx * o if acc is None else acc + wx * o
    y_ref[...] = x_ref[...] + _dot(acc.astype(BF16), w_ref[...])


def _merge_out(outs, ms, ls, gexp, x, w, t_len, tm):
    m, d = x.shape
    nt = t_len // tm
    row = pl.BlockSpec((tm, d), lambda i: (i, 0))
    o_specs, s_specs = [], []
    for _, dil in GROUPS:
        o_specs.append(pl.BlockSpec((1, dil, tm // dil, d), lambda i: (i // nt, 0, i % nt, 0)))
        s_specs.append(pl.BlockSpec((1, dil, tm // dil, LANES), lambda i: (i // nt, 0, i % nt, 0)))
    return pl.pallas_call(
        functools.partial(_merge_out_kernel, tm),
        out_shape=jax.ShapeDtypeStruct((m, d), F32),
        grid=(m // tm,),
        in_specs=o_specs + s_specs + s_specs + [pl.BlockSpec(gexp.shape, lambda i: (0, 0)), row,
                                                pl.BlockSpec((d, d), lambda i: (0, 0))],
        out_specs=row,
        scratch_shapes=[pltpu.VMEM((d // LANES, tm, LANES), F32), pltpu.VMEM((tm, LANES), F32),
                        pltpu.VMEM((tm, LANES), F32)],
        compiler_params=_params(("parallel",), 40),
        name="merge_out",
    )(*outs, *ms, *ls, gexp, x, w)


SAMPLE_PAIRS = 4


def _attn_sample_kernel(ts, *refs):
    ng = N_GROUPS
    q_refs, k_refs, v_refs = refs[0:ng], refs[ng:2 * ng], refs[2 * ng:3 * ng]
    c_refs = refs[3 * ng:4 * ng]
    bias_ref, o_ref = refs[4 * ng], refs[4 * ng + 1]
    npair = SAMPLE_PAIRS
    lane = lax.broadcasted_iota(jnp.int32, (ts, LANES), 1)
    lo_half = lane < ATT_HEAD_DIM
    zpad = jnp.zeros((LANES - npair * ts, LANES), F32)

    tk, vs = [], []
    for g in range(ng):
        kst = jnp.concatenate([k_refs[g][0, :, p * LANES:(p + 1) * LANES] for p in range(npair)] + [zpad], axis=0)
        vst = jnp.concatenate([v_refs[g][0, :, p * LANES:(p + 1) * LANES] for p in range(npair)] + [zpad], axis=0)
        tk.append(kst.T.astype(BF16))
        vs.append(vst.astype(BF16))

    rows = []
    for p in range(npair):
        sl = slice(p * LANES, (p + 1) * LANES)
        old, new = [], []
        for g in range(ng):
            qp = q_refs[g][0, :, sl]
            q2 = jnp.concatenate([jnp.where(lo_half, qp, 0.0), jnp.where(lo_half, 0.0, qp)], axis=0).astype(BF16)
            kt = c_refs[g][0, 0, 2 * p:2 * p + 2].reshape(LANES, c_refs[g].shape[-1]).astype(BF16)
            old.append(_dot(q2, kt))
            new.append(_dot(q2, tk[g]))
        rows.append(jnp.concatenate(old + new, axis=1))
    s = jnp.concatenate(rows, axis=0) + bias_ref[...].reshape(npair * 2 * ts, -1)
    mx = jnp.max(s, axis=-1, keepdims=True)
    pr = jnp.exp2(s - mx)
    inv = 1.0 / jnp.sum(pr, axis=-1, keepdims=True)
    prb = pr.astype(BF16)

    for p in range(npair):
        sl = slice(p * LANES, (p + 1) * LANES)
        r0 = p * 2 * ts
        acc = jnp.zeros((2 * ts, LANES), F32)
        off = 0
        for g in range(ng):
            window = c_refs[g].shape[-1]
            vt = c_refs[g][0, 1, 2 * p:2 * p + 2].reshape(LANES, window).astype(BF16)
            acc = acc + _dot_nt(prb[r0:r0 + 2 * ts, off:off + window], vt)
            off += window
        for g in range(ng):
            acc = acc + _dot(prb[r0:r0 + 2 * ts, off:off + LANES], vs[g])
            off += LANES
        om = acc * inv[r0:r0 + 2 * ts]
        o_ref[0, :, sl] = jnp.where(lo_half, om[:ts], om[ts:])


def _attn_sample(qkv32, caches_t, bias):
    b, ts, _ = qkv32.shape
    d = ATT_HEADS * ATT_HEAD_DIM
    width = SAMPLE_PAIRS * LANES
    per_blk = d // width
    hb = 2 * SAMPLE_PAIRS

    def new_rows(which, g):
        return pl.BlockSpec((1, ts, width), lambda bi, hq: (bi, 0, (which * N_GROUPS + g) * per_blk + hq))

    in_specs = [new_rows(w, g) for w in range(3) for g in range(N_GROUPS)]
    for c in caches_t:
        in_specs.append(pl.BlockSpec((1, 2, hb, ATT_HEAD_DIM, c.shape[-1]), lambda bi, hq: (bi, 0, hq, 0, 0)))
    in_specs.append(pl.BlockSpec((SAMPLE_PAIRS,) + bias.shape[1:], lambda bi, hq: (hq, 0, 0)))
    return pl.pallas_call(
        functools.partial(_attn_sample_kernel, ts),
        out_shape=jax.ShapeDtypeStruct((b, ts, d), F32),
        grid=(b, per_blk),
        in_specs=in_specs,
        out_specs=pl.BlockSpec((1, ts, width), lambda bi, hq: (bi, 0, hq)),
        compiler_params=_params(("parallel", "arbitrary"), 56),
        name="attn_sample",
    )(*([qkv32] * (3 * N_GROUPS)), *caches_t, bias)


def _matmul_res_kernel(a_ref, x_ref, w_ref, y_ref):
    y_ref[...] = x_ref[...] + _dot(a_ref[...].astype(BF16), w_ref[...])


def _matmul_res(a, x, w, tm):
    m, d = x.shape
    row = pl.BlockSpec((tm, d), lambda i: (i, 0))
    return pl.pallas_call(
        _matmul_res_kernel,
        out_shape=jax.ShapeDtypeStruct((m, d), F32),
        grid=(m // tm,),
        in_specs=[row, row, pl.BlockSpec((d, d), lambda i: (0, 0))],
        out_specs=row,
        compiler_params=_params(("parallel",), 40),
        name="matmul_res",
    )(a, x, w)


def _alibi_slopes():
    n = N_GROUPS * ATT_HEADS
    return LOG2E * (2.0 ** (-8.0 * np.arange(1, n + 1) / n)).reshape(N_GROUPS, ATT_HEADS)


def _prompt_bias(g):
    slopes = _alibi_slopes()[g] * GROUPS[g][1]
    dist = np.arange(N_BACK)[:, None] - np.arange(2 * N_BACK)[None, :] + N_BACK
    band = (dist >= 0) & (dist <= N_BACK)
    has_prev = np.arange(2 * N_BACK)[None, :] >= N_BACK
    out = np.empty((2, ATT_HEADS, N_BACK, 2 * N_BACK), np.float32)
    for k, valid in enumerate((band & has_prev, band)):
        out[k] = np.where(valid[None], -slopes[:, None, None] * dist[None], NEG)
    return jnp.asarray(out.reshape(2, ATT_HEADS // 2, 2 * N_BACK, 2 * N_BACK))


def _sample_bias(ts):
    slopes = _alibi_slopes()
    npairs = ATT_HEADS // 2
    old, new = [], []
    for g, (window, dil) in enumerate(GROUPS):
        pos = np.arange(window)
        bc = np.full((npairs, 2 * ts, window), NEG, np.float32)
        bn = np.full((npairs, 2 * ts, LANES), NEG, np.float32)
        for h in range(ATT_HEADS):
            for s in range(ts):
                row = (h % 2) * ts + s
                back = window + s - pos
                valid = (back % dil == 0) & (back <= window)
                bc[h // 2, row] = np.where(valid, -slopes[g, h] * back, NEG)
                lane0 = ((h // 2) % SAMPLE_PAIRS) * ts
                for s2 in range(s + 1):
                    if (s - s2) % dil == 0:
                        bn[h // 2, row, lane0 + s2] = -slopes[g, h] * (s - s2)
        old.append(bc)
        new.append(bn)
    return jnp.asarray(np.concatenate(old + new, axis=-1))


def _head_group_matrices():
    d = ATT_HEADS * ATT_HEAD_DIM
    i = np.arange(2 * LANES)
    gblk = (i[:, None] // ATT_HEAD_DIM == i[None, :] // ATT_HEAD_DIM).astype(np.float32)
    gexp = (np.arange(LANES)[:, None] == np.arange(d)[None, :] // ATT_HEAD_DIM).astype(np.float32)
    return jnp.asarray(gblk, BF16), jnp.asarray(np.concatenate([gexp, gexp], axis=0), BF16)


def _tile(m, want):
    return want if m % want == 0 else m


def kernel(x_prompt, x_sample, state_hgrn, cache_kv_w128, cache_kv_w512, cache_kv_w2048, hg_lb_logits, hg_w_q, hg_w_f, hg_w_i, hg_w_g, hg_w_o, hg_norm_o, att_w_qkv, att_w_o, att_q_norm, att_k_norm, norm_mix, norm_ffn, ffn_w_up, ffn_w_down):
    b, t_len, d = x_prompt.shape
    bs, ts, _ = x_sample.shape
    depth = norm_mix.shape[0]
    caches_all = (cache_kv_w128, cache_kv_w512, cache_kv_w2048)
    gblk, gexp = _head_group_matrices()
    sample_bias = _sample_bias(ts)

    yp = x_prompt.reshape(b * t_len, d)
    ys = x_sample.reshape(bs * ts, d)
    tm_p = _tile(b * t_len, 512)
    tm_big = _tile(t_len, 1024)
    tm_s = _tile(bs * ts, 256)
    hg_p, hg_s = [], []
    kv_p = [[] for _ in GROUPS]
    kv_s = [[] for _ in GROUPS]
    for layer in range(depth):
        a = layer // 2
        nw = norm_mix[layer][None]
        if layer % 2 == 0:
            w4 = jnp.stack([hg_w_f[a], hg_w_q[a], hg_w_i[a], hg_w_g[a]]).astype(BF16)
            wo = hg_w_o[a].astype(BF16)
            gn = hg_norm_o[a][None]
            zeros = jnp.zeros((b,) + state_hgrn.shape[2:], F32)
            new = []
            for y, s0, nb_, tl, tm, tc in ((yp, zeros, b, t_len, tm_p, _tile(t_len, 512)),
                                            (ys, state_hgrn[a], bs, ts, tm_s, ts)):
                f, qig = _hgrn_proj(y, nw, hg_lb_logits, w4, layer, tm, BF16 if tc % 16 == 0 else F32)
                o, st = _hgrn_rec(f, qig, s0, nb_, tl, tc, 128 if tc >= 128 else 16)
                new.append((_hgrn_out(o, qig, gn, y, wo, tm), st))
            (yp, sp), (ys, ss) = new
            hg_p.append(sp)
            hg_s.append(ss)
        else:
            wqkv = att_w_qkv[a]
            wo = att_w_o[a].astype(BF16)
            qn_row = jnp.tile(att_q_norm[a], ATT_HEADS)
            kn_row = jnp.tile(att_k_norm[a], ATT_HEADS)
            qkn = jnp.stack([qn_row * (ATT_SCALE * LOG2E), kn_row])[:, None, :]
            qkv_groups = _qkv_perm(yp, nw, att_w_qkv, a, qkn, gblk, b, t_len, tm_big)
            outs, ms, ls = [], [], []
            for g, (window, dil) in enumerate(GROUPS):
                qb = 2 if (t_len // dil) % (2 * N_BACK) == 0 else 1
                o, mrow, lrow = _attn_prompt(qkv_groups[g], _prompt_bias(g), qb)
                outs.append(o)
                ms.append(mrow)
                ls.append(lrow)
                keep = min(window, t_len)
                wt = jnp.stack([wqkv[:, (N_GROUPS + g) * d:(N_GROUPS + g + 1) * d].T,
                                wqkv[:, (2 * N_GROUPS + g) * d:(2 * N_GROUPS + g + 1) * d].T]).astype(BF16)
                kvt = _kv_tail(yp, nw, wt, kn_row[:, None], b, t_len, keep, _tile(keep, 512))
                kvt = kvt.reshape(b, 2, ATT_HEADS, ATT_HEAD_DIM, keep)
                kv_p[g].append(jnp.transpose(kvt, (0, 4, 1, 2, 3)))
            yp = _merge_out(outs, ms, ls, gexp, yp, wo, t_len, tm_p)
            qkv32 = _qkv_plain(ys, nw, att_w_qkv, a, qkn, gblk, tm_s).reshape(bs, ts, -1)
            caches_t = [jnp.transpose(c[a], (0, 2, 3, 4, 1)) for c in caches_all]
            om = _attn_sample(qkv32, caches_t, sample_bias)
            ys = _matmul_res(om.reshape(bs * ts, d), ys, wo, tm_s)
            kv = qkv32.reshape(bs, ts, 3, N_GROUPS, ATT_HEADS, ATT_HEAD_DIM)
            for g in range(N_GROUPS):
                kv_s[g].append(kv[:, :, 1:, g])
        nf = norm_ffn[layer][None]
        yp = _ffn(yp, nf, ffn_w_up, ffn_w_down, layer, tm_big, 512)
        ys = _ffn(ys, nf, ffn_w_up, ffn_w_down, layer, tm_s, 512)
    return (yp.reshape(b, t_len, d), ys.reshape(bs, ts, d), jnp.stack(hg_p), jnp.stack(hg_s),
            jnp.stack(kv_p[0]), jnp.stack(kv_s[0]),
            jnp.stack(kv_p[1]), jnp.stack(kv_s[1]),
            jnp.stack(kv_p[2]), jnp.stack(kv_s[2]))
```

```python
import functools

import numpy as np
import jax
import jax.numpy as jnp
from jax import lax
from jax.experimental import pallas as pl
from jax.experimental.pallas import tpu as pltpu

F32 = jnp.float32
BF16 = jnp.bfloat16

RMS_EPS = 1e-6
HG_HEADS = 8
HG_DK = 128
GROUPS = ((128, 1), (512, 4), (2048, 16))
N_GROUPS = len(GROUPS)
ATT_HEADS = 16
ATT_HEAD_DIM = 64
ATT_SCALE = ATT_HEAD_DIM ** -0.5
N_BACK = 128
NEG = -1e30
LOG2E = 1.4426950408889634

LANES = 128
MXU_WIDTH = 256
MIB = 1024 * 1024


def _dot(a, b):
    return jnp.dot(a, b, preferred_element_type=F32)


def _dot_nt(a, b):
    return lax.dot_general(a, b, (((1,), (1,)), ((), ())), preferred_element_type=F32)


def _dot_tn(a, b):
    return lax.dot_general(a, b, (((0,), (0,)), ((), ())), preferred_element_type=F32)


def _split3(x):
    hi = x.astype(BF16)
    r1 = x - hi.astype(F32)
    mid = r1.astype(BF16)
    lo = (r1 - mid.astype(F32)).astype(BF16)
    return hi, mid, lo


def _rms_rows(x, w):
    ms = jnp.mean(x * x, axis=-1, keepdims=True)
    return x * lax.rsqrt(ms + RMS_EPS) * w


def _params(sem, vmem_mib):
    return pltpu.CompilerParams(dimension_semantics=sem, vmem_limit_bytes=vmem_mib * MIB)


def _hgrn_proj_kernel(layer, x_ref, nw_ref, lbl_ref, w_ref, f_ref, o_ref):
    xn = _rms_rows(x_ref[...], nw_ref[...]).astype(BF16)
    lg = lbl_ref[...]
    e = jnp.exp(lg - jnp.max(lg, axis=0, keepdims=True))
    lb_row = jnp.sum(e[:layer + 1], axis=0, keepdims=True) / jnp.sum(e, axis=0, keepdims=True)
    for p in range(4):
        for c in range(0, x_ref.shape[-1], MXU_WIDTH):
            cols = slice(c, c + MXU_WIDTH)
            y = _dot(xn, w_ref[p, :, cols])
            if p == 1:
                o_ref[p, :, cols] = y.astype(o_ref.dtype)
            elif p == 3:
                lb = lb_row[:, cols]
                f_ref[:, cols] = lb + (1.0 - lb) * jax.nn.sigmoid(y)
            else:
                o_ref[p, :, cols] = (y * jax.nn.sigmoid(y)).astype(o_ref.dtype)


def _hgrn_proj(x, nw, lb_logits, w4, layer, tm, qig_dtype):
    m, d = x.shape
    return pl.pallas_call(
        functools.partial(_hgrn_proj_kernel, layer),
        out_shape=(jax.ShapeDtypeStruct((m, d), F32), jax.ShapeDtypeStruct((3, m, d), qig_dtype)),
        grid=(m // tm,),
        in_specs=[
            pl.BlockSpec((tm, d), lambda i: (i, 0)),
            pl.BlockSpec((1, d), lambda i: (0, 0)),
            pl.BlockSpec(lb_logits.shape, lambda i: (0, 0)),
            pl.BlockSpec((4, d, d), lambda i: (0, 0, 0)),
        ],
        out_specs=(pl.BlockSpec((tm, d), lambda i: (i, 0)),
                   pl.BlockSpec((3, tm, d), lambda i: (0, i, 0))),
        compiler_params=_params(("parallel",), 48),
        name="hgrn_proj",
    )(x, nw, lb_logits, w4)


def _hgrn_rec_kernel(tc, c, q_ref, f_ref, v_ref, s0_ref, o_ref, so_ref, st_ref):
    t = pl.program_id(1)
    dk = HG_DK

    @pl.when(t == 0)
    def _():
        for h in range(HG_HEADS):
            st_ref[h] = s0_ref[0, h].T

    row = lax.broadcasted_iota(jnp.int32, (c, c), 0)
    col = lax.broadcasted_iota(jnp.int32, (c, c), 1)
    tril = jnp.where(row >= col, 1.0, 0.0).astype(BF16)
    rowl = lax.broadcasted_iota(jnp.int32, (c, dk), 0)
    n0 = min(c, 16)
    same_block = {}
    n = c // 2
    while n >= n0:
        shift = n.bit_length() - 1
        same_block[n] = (row >> shift) == (col >> shift)
        n //= 2

    def block_ref_rows(ch, n, pick):
        parts = [jnp.broadcast_to(ch[j * n + pick:j * n + pick + 1, :], (n, dk)) for j in range(c // n)]
        return parts[0] if len(parts) == 1 else jnp.concatenate(parts, axis=0)

    def do_chunk(qc, fc, vc, store):
        hi, mid, lo = _split3(jnp.log(fc))
        cum = _dot(tril, hi) + _dot(tril, mid) + _dot(tril, lo)
        def first_stage(h):
            sl = slice(h * dk, (h + 1) * dk)
            qh, kh, ch = qc[:, sl], 1.0 - fc[:, sl], cum[:, sl]
            s_t = st_ref[h]
            o = _dot_nt((qh * jnp.exp(ch)).astype(BF16), s_t.astype(BF16))
            pieces = []
            n = c
            while n > n0:
                half = n // 2
                b = block_ref_rows(ch, n, half - 1)
                upper = (rowl & (n - 1)) >= half
                e = jnp.exp(jnp.where(upper, ch - b, b - ch))
                qt = jnp.where(upper, qh * e, 0.0).astype(BF16)
                kt = jnp.where(upper, 0.0, kh * e).astype(BF16)
                pieces.append((n, _dot_nt(qt, kt)))
                n = half
            dq = ch - block_ref_rows(ch, n0, n0 // 2 - 1)
            pieces.append((n0, _dot_nt((qh * jnp.exp(dq)).astype(BF16), (kh * jnp.exp(-dq)).astype(BF16))))
            return h, s_t, o, pieces

        def second_stage(h, s_t, o, pieces):
            sl = slice(h * dk, (h + 1) * dk)
            kh, vh, ch = 1.0 - fc[:, sl], vc[:, sl], cum[:, sl]
            attn = None
            for n, a in pieces:
                if n == n0:
                    keep = col <= row
                    if n0 < c:
                        keep = keep & same_block[n0]
                    a = jnp.where(keep, a, 0.0)
                elif n < c:
                    a = jnp.where(same_block[n], a, 0.0)
                attn = a if attn is None else attn + a
            store(sl, o + _dot(attn.astype(BF16), vh.astype(BF16)))
            last = ch[c - 1:c, :]
            kd = kh * jnp.exp(last - ch)
            st_ref[h] = s_t * jnp.exp(last) + _dot_tn(vh.astype(BF16), kd.astype(BF16))

        pending = None
        for h in range(HG_HEADS):
            nxt = first_stage(h)
            if pending is not None:
                second_stage(*pending)
            pending = nxt
        second_stage(*pending)

    if tc >= c:
        def body(ci, carry):
            rows = pl.ds(pl.multiple_of(ci * c, c), c)

            def store(sl, o):
                o_ref[rows, sl] = o

            do_chunk(q_ref[0, rows, :].astype(F32), f_ref[rows, :], v_ref[0, rows, :].astype(F32), store)
            return carry

        lax.fori_loop(0, tc // c, body, 0)
    else:
        pad = c - tc
        z = jnp.zeros((pad, q_ref.shape[-1]), F32)

        def store(sl, o):
            o_ref[:, sl] = o[:tc]

        do_chunk(jnp.concatenate([q_ref[0].astype(F32), z], axis=0),
                 jnp.concatenate([f_ref[...], z + 1.0], axis=0),
                 jnp.concatenate([v_ref[0].astype(F32), z], axis=0), store)

    @pl.when(t == pl.num_programs(1) - 1)
    def _():
        for h in range(HG_HEADS):
            so_ref[0, h] = st_ref[h].T


def _hgrn_rec(f, qig, s0, b, t_len, tc, c):
    m, d = f.shape
    nt = t_len // tc
    h, dk, dv = s0.shape[1:]

    def row_spec(which):
        return pl.BlockSpec((1, tc, d), lambda bi, ti: (which, bi * nt + ti, 0))

    return pl.pallas_call(
        functools.partial(_hgrn_rec_kernel, tc, c),
        out_shape=(jax.ShapeDtypeStruct((m, d), F32), jax.ShapeDtypeStruct(s0.shape, F32)),
        grid=(b, nt),
        in_specs=[row_spec(0), pl.BlockSpec((tc, d), lambda bi, ti: (bi * nt + ti, 0)), row_spec(1),
                  pl.BlockSpec((1, h, dk, dv), lambda bi, ti: (bi, 0, 0, 0))],
        out_specs=(pl.BlockSpec((tc, d), lambda bi, ti: (bi * nt + ti, 0)),
                   pl.BlockSpec((1, h, dk, dv), lambda bi, ti: (bi, 0, 0, 0))),
        scratch_shapes=[pltpu.VMEM((h, dv, dk), F32)],
        compiler_params=_params(("parallel", "arbitrary"), 48),
        name="hgrn_rec",
    )(qig, f, qig, s0)


def _hgrn_out_kernel(o_ref, g_ref, gn_ref, x_ref, w_ref, y_ref):
    a = _rms_rows(o_ref[...], gn_ref[...]) * g_ref[0].astype(F32)
    y_ref[...] = x_ref[...] + _dot(a.astype(BF16), w_ref[...])


def _hgrn_out(o, qig, gn, x, w, tm):
    m, d = x.shape
    return pl.pallas_call(
        _hgrn_out_kernel,
        out_shape=jax.ShapeDtypeStruct((m, d), F32),
        grid=(m // tm,),
        in_specs=[
            pl.BlockSpec((tm, d), lambda i: (i, 0)),
            pl.BlockSpec((1, tm, d), lambda i: (2, i, 0)),
            pl.BlockSpec((1, d), lambda i: (0, 0)),
            pl.BlockSpec((tm, d), lambda i: (i, 0)),
            pl.BlockSpec((d, d), lambda i: (0, 0)),
        ],
        out_specs=pl.BlockSpec((tm, d), lambda i: (i, 0)),
        compiler_params=_params(("parallel",), 40),
        name="hgrn_out",
    )(o, qig, gn, x, w)


def _ffn_kernel(x_ref, nw_ref, wu_ref, wd_ref, y_ref, xn_ref):
    j = pl.program_id(1)

    @pl.when(j == 0)
    def _():
        x = x_ref[...]
        xn_ref[...] = _rms_rows(x, nw_ref[...]).astype(BF16)
        y_ref[...] = x

    h = jnp.maximum(_dot(xn_ref[...], wu_ref[...]), 0.0)
    y_ref[...] += _dot((h * h).astype(BF16), wd_ref[...])


def _ffn(x, nw, wu, wd, tm, tf):
    m, d = x.shape
    ff = wu.shape[1]
    return pl.pallas_call(
        _ffn_kernel,
        out_shape=jax.ShapeDtypeStruct((m, d), F32),
        grid=(m // tm, ff // tf),
        in_specs=[
            pl.BlockSpec((tm, d), lambda i, j: (i, 0)),
            pl.BlockSpec((1, d), lambda i, j: (0, 0)),
            pl.BlockSpec((d, tf), lambda i, j: (0, j)),
            pl.BlockSpec((tf, d), lambda i, j: (j, 0)),
        ],
        out_specs=pl.BlockSpec((tm, d), lambda i, j: (i, 0)),
        scratch_shapes=[pltpu.VMEM((tm, d), BF16)],
        compiler_params=_params(("parallel", "arbitrary"), 48),
        name="ffn",
    )(x, nw, wu, wd)


def _qk_norm_rows(y, qkn, gblk):
    y2 = (y * y).astype(BF16)
    w = gblk.shape[0]
    ss = jnp.concatenate([_dot(y2[:, c * w:(c + 1) * w], gblk) for c in range(y.shape[-1] // w)], axis=1)
    return y * lax.rsqrt(ss * (1.0 / ATT_HEAD_DIM) + RMS_EPS) * qkn


def _qkv_perm_kernel(tm, x_ref, nw_ref, w_ref, qkn_ref, gblk_ref, o0_ref, o1_ref, o2_ref, xs_ref, xp_ref):
    outs = (o0_ref, o1_ref, o2_ref)
    d = x_ref.shape[-1]
    nslab = d // LANES
    xn = _rms_rows(x_ref[...], nw_ref[...])
    xp_ref[0] = xn.astype(BF16)
    for c in range(nslab):
        xs_ref[c] = xn[:, c * LANES:(c + 1) * LANES]
    for g in range(1, N_GROUPS):
        dil = GROUPS[g][1]
        n = tm // dil
        for c in range(nslab):
            for r in range(dil):
                xp_ref[g, r * n:(r + 1) * n, c * LANES:(c + 1) * LANES] = (
                    xs_ref.at[c][pl.ds(r, n, stride=dil), :].astype(BF16))
    gblk = gblk_ref[...]

    def finish(g, which, c, y):
        dil = GROUPS[g][1]
        if which < 2:
            ss = _dot((y * y).astype(BF16), gblk)
            y = y * lax.rsqrt(ss * (1.0 / ATT_HEAD_DIM) + RMS_EPS) * qkn_ref[which:which + 1, c:c + MXU_WIDTH]
        outs[g][which, 0, :, :, c:c + MXU_WIDTH] = y.astype(BF16).reshape(dil, tm // dil, MXU_WIDTH)

    pending = None
    for g in range(N_GROUPS):
        xg = xp_ref[g]
        for which in range(3):
            base = (which * N_GROUPS + g) * d
            for c in range(0, d, MXU_WIDTH):
                y = _dot(xg, w_ref[:, base + c:base + c + MXU_WIDTH])
                if pending is not None:
                    finish(*pending)
                pending = (g, which, c, y)
    finish(*pending)


def _qkv_perm(x, nw, w, qkn, gblk, b, t_len, tm):
    m, d = x.shape
    nt = t_len // tm
    out_shapes, out_specs = [], []
    for _, dil in GROUPS:
        out_shapes.append(jax.ShapeDtypeStruct((3, b, dil, t_len // dil, d), BF16))
        out_specs.append(pl.BlockSpec((3, 1, dil, tm // dil, d), lambda i: (0, i // nt, 0, i % nt, 0)))
    return pl.pallas_call(
        functools.partial(_qkv_perm_kernel, tm),
        out_shape=tuple(out_shapes),
        grid=(m // tm,),
        in_specs=[
            pl.BlockSpec((tm, d), lambda i: (i, 0)),
            pl.BlockSpec((1, d), lambda i: (0, 0)),
            pl.BlockSpec(w.shape, lambda i: (0, 0), pipeline_mode=pl.Buffered(1)),
            pl.BlockSpec(qkn.shape, lambda i: (0, 0)),
            pl.BlockSpec(gblk.shape, lambda i: (0, 0)),
        ],
        out_specs=tuple(out_specs),
        scratch_shapes=[pltpu.VMEM((d // LANES, tm, LANES), F32),
                        pltpu.VMEM((N_GROUPS, tm, d), BF16)],
        compiler_params=_params(("parallel",), 56),
        name="qkv_perm",
    )(x, nw, w, qkn, gblk)


def _qkv_plain_kernel(x_ref, nw_ref, w_ref, qkn_ref, gblk_ref, o_ref, xn_ref):
    j = pl.program_id(1)

    @pl.when(j == 0)
    def _():
        xn_ref[...] = _rms_rows(x_ref[...], nw_ref[...]).astype(BF16)

    y = _dot(xn_ref[...], w_ref[...])

    @pl.when(j < 2 * N_GROUPS)
    def _():
        gain = jnp.where(j < N_GROUPS, qkn_ref[0:1, :], qkn_ref[1:2, :])
        o_ref[...] = _qk_norm_rows(y, gain, gblk_ref[...])

    @pl.when(j >= 2 * N_GROUPS)
    def _():
        o_ref[...] = y


def _qkv_plain(x, nw, w, qkn, gblk, tm):
    m, d = x.shape
    n = w.shape[1]
    return pl.pallas_call(
        _qkv_plain_kernel,
        out_shape=jax.ShapeDtypeStruct((m, n), F32),
        grid=(m // tm, n // d),
        in_specs=[
            pl.BlockSpec((tm, d), lambda i, j: (i, 0)),
            pl.BlockSpec((1, d), lambda i, j: (0, 0)),
            pl.BlockSpec((d, d), lambda i, j: (0, j)),
            pl.BlockSpec(qkn.shape, lambda i, j: (0, 0)),
            pl.BlockSpec(gblk.shape, lambda i, j: (0, 0)),
        ],
        out_specs=pl.BlockSpec((tm, d), lambda i, j: (i, j)),
        scratch_shapes=[pltpu.VMEM((tm, d), BF16)],
        compiler_params=_params(("parallel", "arbitrary"), 40),
        name="qkv_plain",
    )(x, nw, w, qkn, gblk)


def _kv_tail_kernel(x_ref, nw_ref, wt_ref, kn_ref, o_ref, xn_ref):
    kv = pl.program_id(2)

    @pl.when(kv == 0)
    def _():
        xn_ref[...] = _rms_rows(x_ref[...], nw_ref[...]).astype(BF16)

    yt = _dot_nt(wt_ref[0], xn_ref[...])

    @pl.when(kv == 0)
    def _():
        y3 = yt.reshape(ATT_HEADS, ATT_HEAD_DIM, yt.shape[-1])
        ms = jnp.mean(y3 * y3, axis=1, keepdims=True)
        o_ref[0, 0] = (y3 * lax.rsqrt(ms + RMS_EPS)).reshape(yt.shape) * kn_ref[...]

    @pl.when(kv == 1)
    def _():
        o_ref[0, 0] = yt


def _kv_tail(x, nw, wt, kn_col, b, t_len, keep, tt):
    m, d = x.shape
    first = (t_len - keep) // tt
    per_seq = t_len // tt
    return pl.pallas_call(
        _kv_tail_kernel,
        out_shape=jax.ShapeDtypeStruct((b, 2, d, keep), F32),
        grid=(b, keep // tt, 2),
        in_specs=[
            pl.BlockSpec((tt, d), lambda bi, ti, kv: (bi * per_seq + first + ti, 0)),
            pl.BlockSpec((1, d), lambda bi, ti, kv: (0, 0)),
            pl.BlockSpec((1, d, d), lambda bi, ti, kv: (kv, 0, 0)),
            pl.BlockSpec((d, 1), lambda bi, ti, kv: (0, 0)),
        ],
        out_specs=pl.BlockSpec((1, 1, d, tt), lambda bi, ti, kv: (bi, kv, 0, ti)),
        scratch_shapes=[pltpu.VMEM((tt, d), BF16)],
        compiler_params=_params(("parallel", "parallel", "arbitrary"), 40),
        name="kv_tail",
    )(x, nw, wt, kn_col)


def _attn_prompt_kernel(qb, q_ref, kc_ref, kp_ref, vc_ref, vp_ref, bias_ref, o_ref, m_ref, l_ref):
    i = pl.program_id(2)
    first = jnp.where(i == 0, 0, 1)
    lane = lax.broadcasted_iota(jnp.int32, (N_BACK, LANES), 1)
    lo_half = lane < ATT_HEAD_DIM
    npair = ATT_HEADS // 2
    stats = {}

    def scores(u, pr):
        rows = slice(u * N_BACK, (u + 1) * N_BACK)
        sl = slice(pr * LANES, (pr + 1) * LANES)
        qp = q_ref[0, 0, 0, rows, sl]
        kprev = kp_ref[0, 0, 0, :, sl] if u == 0 else kc_ref[0, 0, 0, (u - 1) * N_BACK:u * N_BACK, sl]
        kk = jnp.concatenate([kprev, kc_ref[0, 0, 0, rows, sl]], axis=0)
        zq = jnp.zeros_like(qp)
        q2 = jnp.concatenate([jnp.where(lo_half, qp, zq), jnp.where(lo_half, zq, qp)], axis=0)
        return _dot_nt(q2, kk)

    def finish(u, pr, s):
        rows = slice(u * N_BACK, (u + 1) * N_BACK)
        sl = slice(pr * LANES, (pr + 1) * LANES)
        vprev = vp_ref[0, 0, 0, :, sl] if u == 0 else vc_ref[0, 0, 0, (u - 1) * N_BACK:u * N_BACK, sl]
        vv = jnp.concatenate([vprev, vc_ref[0, 0, 0, rows, sl]], axis=0)
        s = s + (bias_ref[first, pr] if u == 0 else bias_ref[1, pr])
        mx = jnp.max(s, axis=-1, keepdims=True)
        p = jnp.exp2(s - mx)
        l = jnp.sum(p, axis=-1, keepdims=True)
        o2 = _dot(p.astype(BF16), vv)
        o_ref[0, 0, rows, sl] = jnp.where(lo_half, o2[:N_BACK], o2[N_BACK:]).astype(BF16)
        m_acc, l_acc = stats.get(u, (jnp.zeros((N_BACK, LANES), F32), jnp.ones((N_BACK, LANES), F32)))
        m_acc = jnp.where(lane == 2 * pr, mx[:N_BACK], jnp.where(lane == 2 * pr + 1, mx[N_BACK:], m_acc))
        l_acc = jnp.where(lane == 2 * pr, l[:N_BACK], jnp.where(lane == 2 * pr + 1, l[N_BACK:], l_acc))
        stats[u] = (m_acc, l_acc)
        if pr == npair - 1:
            m_ref[0, 0, rows, :] = m_acc
            l_ref[0, 0, rows, :] = l_acc

    pending = None
    for u in range(qb):
        for pr in range(npair):
            s = scores(u, pr)
            if pending is not None:
                finish(*pending)
            pending = (u, pr, s)
    finish(*pending)


def _attn_prompt(qkv_g, bias, qb):
    _, b, dil, l, d = qkv_g.shape
    rows = qb * N_BACK

    def cur(which):
        return pl.BlockSpec((1, 1, 1, rows, d), lambda bi, r, i: (which, bi, r, i, 0))

    def prev(which):
        return pl.BlockSpec((1, 1, 1, N_BACK, d),
                            lambda bi, r, i: (which, bi, r, jnp.maximum(i * qb - 1, 0), 0))

    return pl.pallas_call(
        functools.partial(_attn_prompt_kernel, qb),
        out_shape=(jax.ShapeDtypeStruct((b, dil, l, d), BF16),
                   jax.ShapeDtypeStruct((b, dil, l, LANES), F32),
                   jax.ShapeDtypeStruct((b, dil, l, LANES), F32)),
        grid=(b, dil, l // rows),
        in_specs=[cur(0), cur(1), prev(1), cur(2), prev(2),
                  pl.BlockSpec(bias.shape, lambda bi, r, i: (0, 0, 0, 0))],
        out_specs=(pl.BlockSpec((1, 1, rows, d), lambda bi, r, i: (bi, r, i, 0)),
                   pl.BlockSpec((1, 1, rows, LANES), lambda bi, r, i: (bi, r, i, 0)),
                   pl.BlockSpec((1, 1, rows, LANES), lambda bi, r, i: (bi, r, i, 0))),
        compiler_params=_params(("parallel", "parallel", "arbitrary"), 40),
        name="attn_prompt_d%d" % dil,
    )(qkv_g, qkv_g, qkv_g, qkv_g, qkv_g, bias)


def _merge_out_kernel(tm, o0_ref, o1_ref, o2_ref, m0_ref, m1_ref, m2_ref, l0_ref, l1_ref, l2_ref,
                      gexp_ref, x_ref, w_ref, y_ref, os_ref, ms_ref, ls_ref):
    d = x_ref.shape[-1]
    nslab = d // LANES

    def natural(o_ref, m_ref, l_ref, dil):
        if dil == 1:
            return o_ref[0, 0].astype(F32), m_ref[0, 0], l_ref[0, 0]
        n = tm // dil
        for r in range(dil):
            ms_ref[pl.ds(r, n, stride=dil), :] = m_ref[0, r]
            ls_ref[pl.ds(r, n, stride=dil), :] = l_ref[0, r]
            for c in range(nslab):
                os_ref.at[c][pl.ds(r, n, stride=dil), :] = o_ref[0, r, :, c * LANES:(c + 1) * LANES].astype(F32)
        return jnp.concatenate([os_ref[c] for c in range(nslab)], axis=1), ms_ref[...], ls_ref[...]

    trip = [natural(o_ref, m_ref, l_ref, GROUPS[g][1]) for g, (o_ref, m_ref, l_ref) in enumerate(
        ((o0_ref, m0_ref, l0_ref), (o1_ref, m1_ref, l1_ref), (o2_ref, m2_ref, l2_ref)))]
    mx = jnp.maximum(jnp.maximum(trip[0][1], trip[1][1]), trip[2][1])
    es = [jnp.exp2(t[1] - mx) for t in trip]
    inv = 1.0 / (es[0] * trip[0][2] + es[1] * trip[1][2] + es[2] * trip[2][2])
    acc = None
    for e, (o, _, _) in zip(es, trip):
        wt = e * inv
        hi = wt.astype(BF16)
        lo = (wt - hi.astype(F32)).astype(BF16)
        wx = _dot(jnp.concatenate([hi, lo], axis=1), gexp_ref[...])
        acc = wx * o if acc is None else acc + wx * o
    y_ref[...] = x_ref[...] + _dot(acc.astype(BF16), w_ref[...])


def _merge_out(outs, ms, ls, gexp, x, w, t_len, tm):
    m, d = x.shape
    nt = t_len // tm
    row = pl.BlockSpec((tm, d), lambda i: (i, 0))
    o_specs, s_specs = [], []
    for _, dil in GROUPS:
        o_specs.append(pl.BlockSpec((1, dil, tm // dil, d), lambda i: (i // nt, 0, i % nt, 0)))
        s_specs.append(pl.BlockSpec((1, dil, tm // dil, LANES), lambda i: (i // nt, 0, i % nt, 0)))
    return pl.pallas_call(
        functools.partial(_merge_out_kernel, tm),
        out_shape=jax.ShapeDtypeStruct((m, d), F32),
        grid=(m // tm,),
        in_specs=o_specs + s_specs + s_specs + [pl.BlockSpec(gexp.shape, lambda i: (0, 0)), row,
                                                pl.BlockSpec((d, d), lambda i: (0, 0))],
        out_specs=row,
        scratch_shapes=[pltpu.VMEM((d // LANES, tm, LANES), F32), pltpu.VMEM((tm, LANES), F32),
                        pltpu.VMEM((tm, LANES), F32)],
        compiler_params=_params(("parallel",), 40),
        name="merge_out",
    )(*outs, *ms, *ls, gexp, x, w)


SAMPLE_PAIRS = 4


def _attn_sample_kernel(ts, *refs):
    ng = N_GROUPS
    q_refs, k_refs, v_refs = refs[0:ng], refs[ng:2 * ng], refs[2 * ng:3 * ng]
    c_refs = refs[3 * ng:4 * ng]
    bias_ref, o_ref = refs[4 * ng], refs[4 * ng + 1]
    npair = SAMPLE_PAIRS
    lane = lax.broadcasted_iota(jnp.int32, (ts, LANES), 1)
    lo_half = lane < ATT_HEAD_DIM
    zpad = jnp.zeros((LANES - npair * ts, LANES), F32)

    tk, vs = [], []
    for g in range(ng):
        kst = jnp.concatenate([k_refs[g][0, :, p * LANES:(p + 1) * LANES] for p in range(npair)] + [zpad], axis=0)
        vst = jnp.concatenate([v_refs[g][0, :, p * LANES:(p + 1) * LANES] for p in range(npair)] + [zpad], axis=0)
        tk.append(kst.T.astype(BF16))
        vs.append(vst.astype(BF16))

    rows = []
    for p in range(npair):
        sl = slice(p * LANES, (p + 1) * LANES)
        old, new = [], []
        for g in range(ng):
            qp = q_refs[g][0, :, sl]
            q2 = jnp.concatenate([jnp.where(lo_half, qp, 0.0), jnp.where(lo_half, 0.0, qp)], axis=0).astype(BF16)
            kt = c_refs[g][0, 0, 2 * p:2 * p + 2].reshape(LANES, c_refs[g].shape[-1]).astype(BF16)
            old.append(_dot(q2, kt))
            new.append(_dot(q2, tk[g]))
        rows.append(jnp.concatenate(old + new, axis=1))
    s = jnp.concatenate(rows, axis=0) + bias_ref[...].reshape(npair * 2 * ts, -1)
    mx = jnp.max(s, axis=-1, keepdims=True)
    pr = jnp.exp2(s - mx)
    inv = 1.0 / jnp.sum(pr, axis=-1, keepdims=True)
    prb = pr.astype(BF16)

    for p in range(npair):
        sl = slice(p * LANES, (p + 1) * LANES)
        r0 = p * 2 * ts
        acc = jnp.zeros((2 * ts, LANES), F32)
        off = 0
        for g in range(ng):
            window = c_refs[g].shape[-1]
            vt = c_refs[g][0, 1, 2 * p:2 * p + 2].reshape(LANES, window).astype(BF16)
            acc = acc + _dot_nt(prb[r0:r0 + 2 * ts, off:off + window], vt)
            off += window
        for g in range(ng):
            acc = acc + _dot(prb[r0:r0 + 2 * ts, off:off + LANES], vs[g])
            off += LANES
        om = acc * inv[r0:r0 + 2 * ts]
        o_ref[0, :, sl] = jnp.where(lo_half, om[:ts], om[ts:])


def _attn_sample(qkv32, caches_t, bias):
    b, ts, _ = qkv32.shape
    d = ATT_HEADS * ATT_HEAD_DIM
    width = SAMPLE_PAIRS * LANES
    per_blk = d // width
    hb = 2 * SAMPLE_PAIRS

    def new_rows(which, g):
        return pl.BlockSpec((1, ts, width), lambda bi, hq: (bi, 0, (which * N_GROUPS + g) * per_blk + hq))

    in_specs = [new_rows(w, g) for w in range(3) for g in range(N_GROUPS)]
    for c in caches_t:
        in_specs.append(pl.BlockSpec((1, 2, hb, ATT_HEAD_DIM, c.shape[-1]), lambda bi, hq: (bi, 0, hq, 0, 0)))
    in_specs.append(pl.BlockSpec((SAMPLE_PAIRS,) + bias.shape[1:], lambda bi, hq: (hq, 0, 0)))
    return pl.pallas_call(
        functools.partial(_attn_sample_kernel, ts),
        out_shape=jax.ShapeDtypeStruct((b, ts, d), F32),
        grid=(b, per_blk),
        in_specs=in_specs,
        out_specs=pl.BlockSpec((1, ts, width), lambda bi, hq: (bi, 0, hq)),
        compiler_params=_params(("parallel", "arbitrary"), 56),
        name="attn_sample",
    )(*([qkv32] * (3 * N_GROUPS)), *caches_t, bias)


def _matmul_res_kernel(a_ref, x_ref, w_ref, y_ref):
    y_ref[...] = x_ref[...] + _dot(a_ref[...].astype(BF16), w_ref[...])


def _matmul_res(a, x, w, tm):
    m, d = x.shape
    row = pl.BlockSpec((tm, d), lambda i: (i, 0))
    return pl.pallas_call(
        _matmul_res_kernel,
        out_shape=jax.ShapeDtypeStruct((m, d), F32),
        grid=(m // tm,),
        in_specs=[row, row, pl.BlockSpec((d, d), lambda i: (0, 0))],
        out_specs=row,
        compiler_params=_params(("parallel",), 40),
        name="matmul_res",
    )(a, x, w)


def _alibi_slopes():
    n = N_GROUPS * ATT_HEADS
    return LOG2E * (2.0 ** (-8.0 * np.arange(1, n + 1) / n)).reshape(N_GROUPS, ATT_HEADS)


def _prompt_bias(g):
    slopes = _alibi_slopes()[g] * GROUPS[g][1]
    dist = np.arange(N_BACK)[:, None] - np.arange(2 * N_BACK)[None, :] + N_BACK
    band = (dist >= 0) & (dist <= N_BACK)
    has_prev = np.arange(2 * N_BACK)[None, :] >= N_BACK
    out = np.empty((2, ATT_HEADS, N_BACK, 2 * N_BACK), np.float32)
    for k, valid in enumerate((band & has_prev, band)):
        out[k] = np.where(valid[None], -slopes[:, None, None] * dist[None], NEG)
    return jnp.asarray(out.reshape(2, ATT_HEADS // 2, 2 * N_BACK, 2 * N_BACK))


def _sample_bias(ts):
    slopes = _alibi_slopes()
    npairs = ATT_HEADS // 2
    old, new = [], []
    for g, (window, dil) in enumerate(GROUPS):
        pos = np.arange(window)
        bc = np.full((npairs, 2 * ts, window), NEG, np.float32)
        bn = np.full((npairs, 2 * ts, LANES), NEG, np.float32)
        for h in range(ATT_HEADS):
            for s in range(ts):
                row = (h % 2) * ts + s
                back = window + s - pos
                valid = (back % dil == 0) & (back <= window)
                bc[h // 2, row] = np.where(valid, -slopes[g, h] * back, NEG)
                lane0 = ((h // 2) % SAMPLE_PAIRS) * ts
                for s2 in range(s + 1):
                    if (s - s2) % dil == 0:
                        bn[h // 2, row, lane0 + s2] = -slopes[g, h] * (s - s2)
        old.append(bc)
        new.append(bn)
    return jnp.asarray(np.concatenate(old + new, axis=-1))


def _head_group_matrices():
    d = ATT_HEADS * ATT_HEAD_DIM
    i = np.arange(2 * LANES)
    gblk = (i[:, None] // ATT_HEAD_DIM == i[None, :] // ATT_HEAD_DIM).astype(np.float32)
    gexp = (np.arange(LANES)[:, None] == np.arange(d)[None, :] // ATT_HEAD_DIM).astype(np.float32)
    return jnp.asarray(gblk, BF16), jnp.asarray(np.concatenate([gexp, gexp], axis=0), BF16)


def _tile(m, want):
    return want if m % want == 0 else m


def kernel(x_prompt, x_sample, state_hgrn, cache_kv_w128, cache_kv_w512, cache_kv_w2048, hg_lb_logits, hg_w_q, hg_w_f, hg_w_i, hg_w_g, hg_w_o, hg_norm_o, att_w_qkv, att_w_o, att_q_norm, att_k_norm, norm_mix, norm_ffn, ffn_w_up, ffn_w_down):
    b, t_len, d = x_prompt.shape
    bs, ts, _ = x_sample.shape
    depth = norm_mix.shape[0]
    caches_all = (cache_kv_w128, cache_kv_w512, cache_kv_w2048)
    gblk, gexp = _head_group_matrices()
    sample_bias = _sample_bias(ts)

    yp = x_prompt.reshape(b * t_len, d)
    ys = x_sample.reshape(bs * ts, d)
    tm_p = _tile(b * t_len, 512)
    tm_big = _tile(t_len, 1024)
    tm_s = _tile(bs * ts, 256)
    hg_p, hg_s = [], []
    kv_p = [[] for _ in GROUPS]
    kv_s = [[] for _ in GROUPS]
    for layer in range(depth):
        a = layer // 2
        nw = norm_mix[layer][None]
        if layer % 2 == 0:
            w4 = jnp.stack([hg_w_q[a], hg_w_i[a], hg_w_g[a], hg_w_f[a]]).astype(BF16)
            wo = hg_w_o[a].astype(BF16)
            gn = hg_norm_o[a][None]
            zeros = jnp.zeros((b,) + state_hgrn.shape[2:], F32)
            new = []
            for y, s0, nb_, tl, tm, tc in ((yp, zeros, b, t_len, tm_p, _tile(t_len, 512)),
                                            (ys, state_hgrn[a], bs, ts, tm_s, ts)):
                f, qig = _hgrn_proj(y, nw, hg_lb_logits, w4, layer, tm, BF16 if tc % 16 == 0 else F32)
                o, st = _hgrn_rec(f, qig, s0, nb_, tl, tc, 128 if tc >= 128 else 16)
                new.append((_hgrn_out(o, qig, gn, y, wo, tm), st))
            (yp, sp), (ys, ss) = new
            hg_p.append(sp)
            hg_s.append(ss)
        else:
            wqkv = att_w_qkv[a].astype(BF16)
            wo = att_w_o[a].astype(BF16)
            qn_row = jnp.tile(att_q_norm[a], ATT_HEADS)
            kn_row = jnp.tile(att_k_norm[a], ATT_HEADS)
            qkn = jnp.stack([qn_row * (ATT_SCALE * LOG2E), kn_row])
            qkv_groups = _qkv_perm(yp, nw, wqkv, qkn, gblk, b, t_len, tm_p)
            outs, ms, ls = [], [], []
            for g, (window, dil) in enumerate(GROUPS):
                qb = max(q for q in (1, 2, 4) if (t_len // dil) % (q * N_BACK) == 0)
                o, mrow, lrow = _attn_prompt(qkv_groups[g], _prompt_bias(g), qb)
                outs.append(o)
                ms.append(mrow)
                ls.append(lrow)
                keep = min(window, t_len)
                wt = jnp.stack([wqkv[:, (N_GROUPS + g) * d:(N_GROUPS + g + 1) * d].T,
                                wqkv[:, (2 * N_GROUPS + g) * d:(2 * N_GROUPS + g + 1) * d].T])
                kvt = _kv_tail(yp, nw, wt, kn_row[:, None], b, t_len, keep, _tile(keep, 512))
                kvt = kvt.reshape(b, 2, ATT_HEADS, ATT_HEAD_DIM, keep)
                kv_p[g].append(jnp.transpose(kvt, (0, 4, 1, 2, 3)))
            yp = _merge_out(outs, ms, ls, gexp, yp, wo, t_len, tm_p)
            qkv32 = _qkv_plain(ys, nw, wqkv, qkn, gblk, tm_s).reshape(bs, ts, -1)
            caches_t = [jnp.transpose(c[a], (0, 2, 3, 4, 1)) for c in caches_all]
            om = _attn_sample(qkv32, caches_t, sample_bias)
            ys = _matmul_res(om.reshape(bs * ts, d), ys, wo, tm_s)
            kv = qkv32.reshape(bs, ts, 3, N_GROUPS, ATT_HEADS, ATT_HEAD_DIM)
            for g in range(N_GROUPS):
                kv_s[g].append(kv[:, :, 1:, g])
        nf = norm_ffn[layer][None]
        wu = ffn_w_up[layer].astype(BF16)
        wd = ffn_w_down[layer].astype(BF16)
        yp = _ffn(yp, nf, wu, wd, tm_big, 1024)
        ys = _ffn(ys, nf, wu, wd, tm_s, 1024)
    return (yp.reshape(b, t_len, d), ys.reshape(bs, ts, d), jnp.stack(hg_p), jnp.stack(hg_s),
            jnp.stack(kv_p[0]), jnp.stack(kv_s[0]),
            jnp.stack(kv_p[1]), jnp.stack(kv_s[1]),
            jnp.stack(kv_p[2]), jnp.stack(kv_s[2]))
```

```python
import functools

import numpy as np
import jax
import jax.numpy as jnp
from jax import lax
from jax.experimental import pallas as pl
from jax.experimental.pallas import tpu as pltpu

F32 = jnp.float32
BF16 = jnp.bfloat16

RMS_EPS = 1e-6
HG_HEADS = 8
HG_DK = 128
GROUPS = ((128, 1), (512, 4), (2048, 16))
N_GROUPS = len(GROUPS)
ATT_HEADS = 16
ATT_HEAD_DIM = 64
ATT_SCALE = ATT_HEAD_DIM ** -0.5
N_BACK = 128
NEG = -1e30
LOG2E = 1.4426950408889634

LANES = 128
MXU_WIDTH = 256
MIB = 1024 * 1024


def _dot(a, b):
    return jnp.dot(a, b, preferred_element_type=F32)


def _dot_nt(a, b):
    return lax.dot_general(a, b, (((1,), (1,)), ((), ())), preferred_element_type=F32)


def _dot_tn(a, b):
    return lax.dot_general(a, b, (((0,), (0,)), ((), ())), preferred_element_type=F32)


def _split3(x):
    hi = x.astype(BF16)
    r1 = x - hi.astype(F32)
    mid = r1.astype(BF16)
    lo = (r1 - mid.astype(F32)).astype(BF16)
    return hi, mid, lo


def _rms_rows(x, w):
    ms = jnp.mean(x * x, axis=-1, keepdims=True)
    return x * lax.rsqrt(ms + RMS_EPS) * w


def _params(sem, vmem_mib):
    return pltpu.CompilerParams(dimension_semantics=sem, vmem_limit_bytes=vmem_mib * MIB)


def _hgrn_proj_kernel(layer, x_ref, nw_ref, lbl_ref, w_ref, f_ref, o_ref):
    xn = _rms_rows(x_ref[...], nw_ref[...]).astype(BF16)
    lg = lbl_ref[...]
    e = jnp.exp(lg - jnp.max(lg, axis=0, keepdims=True))
    lb_row = jnp.sum(e[:layer + 1], axis=0, keepdims=True) / jnp.sum(e, axis=0, keepdims=True)
    for p in range(4):
        for c in range(0, x_ref.shape[-1], MXU_WIDTH):
            cols = slice(c, c + MXU_WIDTH)
            y = _dot(xn, w_ref[p, :, cols])
            if p == 1:
                o_ref[p, :, cols] = y.astype(o_ref.dtype)
            elif p == 3:
                lb = lb_row[:, cols]
                f_ref[:, cols] = lb + (1.0 - lb) * jax.nn.sigmoid(y)
            else:
                o_ref[p, :, cols] = (y * jax.nn.sigmoid(y)).astype(o_ref.dtype)


def _hgrn_proj(x, nw, lb_logits, w4, layer, tm, qig_dtype):
    m, d = x.shape
    return pl.pallas_call(
        functools.partial(_hgrn_proj_kernel, layer),
        out_shape=(jax.ShapeDtypeStruct((m, d), F32), jax.ShapeDtypeStruct((3, m, d), qig_dtype)),
        grid=(m // tm,),
        in_specs=[
            pl.BlockSpec((tm, d), lambda i: (i, 0)),
            pl.BlockSpec((1, d), lambda i: (0, 0)),
            pl.BlockSpec(lb_logits.shape, lambda i: (0, 0)),
            pl.BlockSpec((4, d, d), lambda i: (0, 0, 0)),
        ],
        out_specs=(pl.BlockSpec((tm, d), lambda i: (i, 0)),
                   pl.BlockSpec((3, tm, d), lambda i: (0, i, 0))),
        compiler_params=_params(("parallel",), 48),
        name="hgrn_proj",
    )(x, nw, lb_logits, w4)


def _hgrn_rec_kernel(tc, c, q_ref, f_ref, v_ref, s0_ref, o_ref, so_ref, st_ref):
    t = pl.program_id(1)
    dk = HG_DK

    @pl.when(t == 0)
    def _():
        for h in range(HG_HEADS):
            st_ref[h] = s0_ref[0, h].T

    row = lax.broadcasted_iota(jnp.int32, (c, c), 0)
    col = lax.broadcasted_iota(jnp.int32, (c, c), 1)
    tril = jnp.where(row >= col, 1.0, 0.0).astype(BF16)
    rowl = lax.broadcasted_iota(jnp.int32, (c, dk), 0)
    n0 = min(c, 16)
    same_block = {}
    n = c // 2
    while n >= n0:
        shift = n.bit_length() - 1
        same_block[n] = (row >> shift) == (col >> shift)
        n //= 2

    def block_ref_rows(ch, n, pick):
        parts = [jnp.broadcast_to(ch[j * n + pick:j * n + pick + 1, :], (n, dk)) for j in range(c // n)]
        return parts[0] if len(parts) == 1 else jnp.concatenate(parts, axis=0)

    def do_chunk(qc, fc, vc, store):
        hi, mid, lo = _split3(jnp.log(fc))
        cum = _dot(tril, hi) + _dot(tril, mid) + _dot(tril, lo)
        def first_stage(h):
            sl = slice(h * dk, (h + 1) * dk)
            qh, kh, ch = qc[:, sl], 1.0 - fc[:, sl], cum[:, sl]
            s_t = st_ref[h]
            o = _dot_nt((qh * jnp.exp(ch)).astype(BF16), s_t.astype(BF16))
            pieces = []
            n = c
            while n > n0:
                half = n // 2
                b = block_ref_rows(ch, n, half - 1)
                upper = (rowl & (n - 1)) >= half
                e = jnp.exp(jnp.where(upper, ch - b, b - ch))
                qt = jnp.where(upper, qh * e, 0.0).astype(BF16)
                kt = jnp.where(upper, 0.0, kh * e).astype(BF16)
                pieces.append((n, _dot_nt(qt, kt)))
                n = half
            dq = ch - block_ref_rows(ch, n0, n0 // 2 - 1)
            pieces.append((n0, _dot_nt((qh * jnp.exp(dq)).astype(BF16), (kh * jnp.exp(-dq)).astype(BF16))))
            return h, s_t, o, pieces

        def second_stage(h, s_t, o, pieces):
            sl = slice(h * dk, (h + 1) * dk)
            kh, vh, ch = 1.0 - fc[:, sl], vc[:, sl], cum[:, sl]
            attn = None
            for n, a in pieces:
                if n == n0:
                    keep = col <= row
                    if n0 < c:
                        keep = keep & same_block[n0]
                    a = jnp.where(keep, a, 0.0)
                elif n < c:
                    a = jnp.where(same_block[n], a, 0.0)
                attn = a if attn is None else attn + a
            store(sl, o + _dot(attn.astype(BF16), vh.astype(BF16)))
            last = ch[c - 1:c, :]
            kd = kh * jnp.exp(last - ch)
            st_ref[h] = s_t * jnp.exp(last) + _dot_tn(vh.astype(BF16), kd.astype(BF16))

        pending = None
        for h in range(HG_HEADS):
            nxt = first_stage(h)
            if pending is not None:
                second_stage(*pending)
            pending = nxt
        second_stage(*pending)

    if tc >= c:
        def body(ci, carry):
            rows = pl.ds(pl.multiple_of(ci * c, c), c)

            def store(sl, o):
                o_ref[rows, sl] = o

            do_chunk(q_ref[0, rows, :].astype(F32), f_ref[rows, :], v_ref[0, rows, :].astype(F32), store)
            return carry

        lax.fori_loop(0, tc // c, body, 0)
    else:
        pad = c - tc
        z = jnp.zeros((pad, q_ref.shape[-1]), F32)

        def store(sl, o):
            o_ref[:, sl] = o[:tc]

        do_chunk(jnp.concatenate([q_ref[0].astype(F32), z], axis=0),
                 jnp.concatenate([f_ref[...], z + 1.0], axis=0),
                 jnp.concatenate([v_ref[0].astype(F32), z], axis=0), store)

    @pl.when(t == pl.num_programs(1) - 1)
    def _():
        for h in range(HG_HEADS):
            so_ref[0, h] = st_ref[h].T


def _hgrn_rec(f, qig, s0, b, t_len, tc, c):
    m, d = f.shape
    nt = t_len // tc
    h, dk, dv = s0.shape[1:]

    def row_spec(which):
        return pl.BlockSpec((1, tc, d), lambda bi, ti: (which, bi * nt + ti, 0))

    return pl.pallas_call(
        functools.partial(_hgrn_rec_kernel, tc, c),
        out_shape=(jax.ShapeDtypeStruct((m, d), F32), jax.ShapeDtypeStruct(s0.shape, F32)),
        grid=(b, nt),
        in_specs=[row_spec(0), pl.BlockSpec((tc, d), lambda bi, ti: (bi * nt + ti, 0)), row_spec(1),
                  pl.BlockSpec((1, h, dk, dv), lambda bi, ti: (bi, 0, 0, 0))],
        out_specs=(pl.BlockSpec((tc, d), lambda bi, ti: (bi * nt + ti, 0)),
                   pl.BlockSpec((1, h, dk, dv), lambda bi, ti: (bi, 0, 0, 0))),
        scratch_shapes=[pltpu.VMEM((h, dv, dk), F32)],
        compiler_params=_params(("parallel", "arbitrary"), 48),
        name="hgrn_rec",
    )(qig, f, qig, s0)


def _hgrn_out_kernel(o_ref, g_ref, gn_ref, x_ref, w_ref, y_ref):
    a = _rms_rows(o_ref[...], gn_ref[...]) * g_ref[0].astype(F32)
    y_ref[...] = x_ref[...] + _dot(a.astype(BF16), w_ref[...])


def _hgrn_out(o, qig, gn, x, w, tm):
    m, d = x.shape
    return pl.pallas_call(
        _hgrn_out_kernel,
        out_shape=jax.ShapeDtypeStruct((m, d), F32),
        grid=(m // tm,),
        in_specs=[
            pl.BlockSpec((tm, d), lambda i: (i, 0)),
            pl.BlockSpec((1, tm, d), lambda i: (2, i, 0)),
            pl.BlockSpec((1, d), lambda i: (0, 0)),
            pl.BlockSpec((tm, d), lambda i: (i, 0)),
            pl.BlockSpec((d, d), lambda i: (0, 0)),
        ],
        out_specs=pl.BlockSpec((tm, d), lambda i: (i, 0)),
        compiler_params=_params(("parallel",), 40),
        name="hgrn_out",
    )(o, qig, gn, x, w)


def _ffn_kernel(x_ref, nw_ref, wu_ref, wd_ref, y_ref, xn_ref):
    j = pl.program_id(1)

    @pl.when(j == 0)
    def _():
        x = x_ref[...]
        xn_ref[...] = _rms_rows(x, nw_ref[...]).astype(BF16)
        y_ref[...] = x

    h = jnp.maximum(_dot(xn_ref[...], wu_ref[...]), 0.0)
    y_ref[...] += _dot((h * h).astype(BF16), wd_ref[...])


def _ffn(x, nw, wu, wd, tm, tf):
    m, d = x.shape
    ff = wu.shape[1]
    return pl.pallas_call(
        _ffn_kernel,
        out_shape=jax.ShapeDtypeStruct((m, d), F32),
        grid=(m // tm, ff // tf),
        in_specs=[
            pl.BlockSpec((tm, d), lambda i, j: (i, 0)),
            pl.BlockSpec((1, d), lambda i, j: (0, 0)),
            pl.BlockSpec((d, tf), lambda i, j: (0, j)),
            pl.BlockSpec((tf, d), lambda i, j: (j, 0)),
        ],
        out_specs=pl.BlockSpec((tm, d), lambda i, j: (i, 0)),
        scratch_shapes=[pltpu.VMEM((tm, d), BF16)],
        compiler_params=_params(("parallel", "arbitrary"), 48),
        name="ffn",
    )(x, nw, wu, wd)


def _ffn_rows(x, nw_ref, wu_ref, wd_ref, y_ref, h_ref):
    xn = _rms_rows(x, nw_ref[...]).astype(BF16)
    for c in range(0, wu_ref.shape[-1], MXU_WIDTH):
        h = jnp.maximum(_dot(xn, wu_ref[:, c:c + MXU_WIDTH]), 0.0)
        h_ref[:, c:c + MXU_WIDTH] = (h * h).astype(BF16)
    hb = h_ref[...]
    for c in range(0, wd_ref.shape[-1], MXU_WIDTH):
        y_ref[:, c:c + MXU_WIDTH] = x[:, c:c + MXU_WIDTH] + _dot(hb, wd_ref[:, c:c + MXU_WIDTH])


def _hgrn_out_ffn_kernel(o_ref, g_ref, gn_ref, x_ref, wo_ref, nw_ref, wu_ref, wd_ref, y_ref, h_ref):
    a = _rms_rows(o_ref[...], gn_ref[...]) * g_ref[0].astype(F32)
    x = x_ref[...] + _dot(a.astype(BF16), wo_ref[...])
    _ffn_rows(x, nw_ref, wu_ref, wd_ref, y_ref, h_ref)


def _resident(shape):
    return pl.BlockSpec(shape, lambda i: (0,) * len(shape), pipeline_mode=pl.Buffered(1))


def _hgrn_out_ffn(o, qig, gn, x, wo, nw, wu, wd, tm):
    m, d = x.shape
    row = pl.BlockSpec((tm, d), lambda i: (i, 0))
    vec = pl.BlockSpec((1, d), lambda i: (0, 0))
    return pl.pallas_call(
        _hgrn_out_ffn_kernel,
        out_shape=jax.ShapeDtypeStruct((m, d), F32),
        grid=(m // tm,),
        in_specs=[row, pl.BlockSpec((1, tm, d), lambda i: (2, i, 0)), vec, row, _resident(wo.shape),
                  vec, _resident(wu.shape), _resident(wd.shape)],
        out_specs=row,
        scratch_shapes=[pltpu.VMEM((tm, wu.shape[-1]), BF16)],
        compiler_params=_params(("parallel",), 56),
        name="hgrn_out_ffn",
    )(o, qig, gn, x, wo, nw, wu, wd)


def _qk_norm_rows(y, qkn, gblk):
    y2 = (y * y).astype(BF16)
    w = gblk.shape[0]
    ss = jnp.concatenate([_dot(y2[:, c * w:(c + 1) * w], gblk) for c in range(y.shape[-1] // w)], axis=1)
    return y * lax.rsqrt(ss * (1.0 / ATT_HEAD_DIM) + RMS_EPS) * qkn


def _qkv_perm_kernel(tm, x_ref, nw_ref, w_ref, qkn_ref, gblk_ref, o0_ref, o1_ref, o2_ref, xs_ref, xp_ref):
    outs = (o0_ref, o1_ref, o2_ref)
    d = x_ref.shape[-1]
    nslab = d // LANES
    xn = _rms_rows(x_ref[...], nw_ref[...])
    xp_ref[0] = xn.astype(BF16)
    for c in range(nslab):
        xs_ref[c] = xn[:, c * LANES:(c + 1) * LANES]
    for g in range(1, N_GROUPS):
        dil = GROUPS[g][1]
        n = tm // dil
        for c in range(nslab):
            for r in range(dil):
                xp_ref[g, r * n:(r + 1) * n, c * LANES:(c + 1) * LANES] = (
                    xs_ref.at[c][pl.ds(r, n, stride=dil), :].astype(BF16))
    gblk = gblk_ref[...]

    def finish(g, which, c, y):
        dil = GROUPS[g][1]
        if which < 2:
            ss = _dot((y * y).astype(BF16), gblk)
            y = y * lax.rsqrt(ss * (1.0 / ATT_HEAD_DIM) + RMS_EPS) * qkn_ref[which:which + 1, c:c + MXU_WIDTH]
        outs[g][which, 0, :, :, c:c + MXU_WIDTH] = y.astype(BF16).reshape(dil, tm // dil, MXU_WIDTH)

    pending = None
    for g in range(N_GROUPS):
        xg = xp_ref[g]
        for which in range(3):
            base = (which * N_GROUPS + g) * d
            for c in range(0, d, MXU_WIDTH):
                y = _dot(xg, w_ref[:, base + c:base + c + MXU_WIDTH])
                if pending is not None:
                    finish(*pending)
                pending = (g, which, c, y)
    finish(*pending)


def _qkv_perm(x, nw, w, qkn, gblk, b, t_len, tm):
    m, d = x.shape
    nt = t_len // tm
    out_shapes, out_specs = [], []
    for _, dil in GROUPS:
        out_shapes.append(jax.ShapeDtypeStruct((3, b, dil, t_len // dil, d), BF16))
        out_specs.append(pl.BlockSpec((3, 1, dil, tm // dil, d), lambda i: (0, i // nt, 0, i % nt, 0)))
    return pl.pallas_call(
        functools.partial(_qkv_perm_kernel, tm),
        out_shape=tuple(out_shapes),
        grid=(m // tm,),
        in_specs=[
            pl.BlockSpec((tm, d), lambda i: (i, 0)),
            pl.BlockSpec((1, d), lambda i: (0, 0)),
            pl.BlockSpec(w.shape, lambda i: (0, 0), pipeline_mode=pl.Buffered(1)),
            pl.BlockSpec(qkn.shape, lambda i: (0, 0)),
            pl.BlockSpec(gblk.shape, lambda i: (0, 0)),
        ],
        out_specs=tuple(out_specs),
        scratch_shapes=[pltpu.VMEM((d // LANES, tm, LANES), F32),
                        pltpu.VMEM((N_GROUPS, tm, d), BF16)],
        compiler_params=_params(("parallel",), 56),
        name="qkv_perm",
    )(x, nw, w, qkn, gblk)


def _qkv_plain_kernel(x_ref, nw_ref, w_ref, qkn_ref, gblk_ref, o_ref, xn_ref):
    j = pl.program_id(1)

    @pl.when(j == 0)
    def _():
        xn_ref[...] = _rms_rows(x_ref[...], nw_ref[...]).astype(BF16)

    y = _dot(xn_ref[...], w_ref[...])

    @pl.when(j < 2 * N_GROUPS)
    def _():
        gain = jnp.where(j < N_GROUPS, qkn_ref[0:1, :], qkn_ref[1:2, :])
        o_ref[...] = _qk_norm_rows(y, gain, gblk_ref[...])

    @pl.when(j >= 2 * N_GROUPS)
    def _():
        o_ref[...] = y


def _qkv_plain(x, nw, w, qkn, gblk, tm):
    m, d = x.shape
    n = w.shape[1]
    return pl.pallas_call(
        _qkv_plain_kernel,
        out_shape=jax.ShapeDtypeStruct((m, n), F32),
        grid=(m // tm, n // d),
        in_specs=[
            pl.BlockSpec((tm, d), lambda i, j: (i, 0)),
            pl.BlockSpec((1, d), lambda i, j: (0, 0)),
            pl.BlockSpec((d, d), lambda i, j: (0, j)),
            pl.BlockSpec(qkn.shape, lambda i, j: (0, 0)),
            pl.BlockSpec(gblk.shape, lambda i, j: (0, 0)),
        ],
        out_specs=pl.BlockSpec((tm, d), lambda i, j: (i, j)),
        scratch_shapes=[pltpu.VMEM((tm, d), BF16)],
        compiler_params=_params(("parallel", "arbitrary"), 40),
        name="qkv_plain",
    )(x, nw, w, qkn, gblk)


def _kv_tail_kernel(x_ref, nw_ref, wt_ref, kn_ref, o_ref, xn_ref):
    kv = pl.program_id(2)

    @pl.when(kv == 0)
    def _():
        xn_ref[...] = _rms_rows(x_ref[...], nw_ref[...]).astype(BF16)

    yt = _dot_nt(wt_ref[0], xn_ref[...])

    @pl.when(kv == 0)
    def _():
        y3 = yt.reshape(ATT_HEADS, ATT_HEAD_DIM, yt.shape[-1])
        ms = jnp.mean(y3 * y3, axis=1, keepdims=True)
        o_ref[0, 0] = (y3 * lax.rsqrt(ms + RMS_EPS)).reshape(yt.shape) * kn_ref[...]

    @pl.when(kv == 1)
    def _():
        o_ref[0, 0] = yt


def _kv_tail(x, nw, wt, kn_col, b, t_len, keep, tt):
    m, d = x.shape
    first = (t_len - keep) // tt
    per_seq = t_len // tt
    return pl.pallas_call(
        _kv_tail_kernel,
        out_shape=jax.ShapeDtypeStruct((b, 2, d, keep), F32),
        grid=(b, keep // tt, 2),
        in_specs=[
            pl.BlockSpec((tt, d), lambda bi, ti, kv: (bi * per_seq + first + ti, 0)),
            pl.BlockSpec((1, d), lambda bi, ti, kv: (0, 0)),
            pl.BlockSpec((1, d, d), lambda bi, ti, kv: (kv, 0, 0)),
            pl.BlockSpec((d, 1), lambda bi, ti, kv: (0, 0)),
        ],
        out_specs=pl.BlockSpec((1, 1, d, tt), lambda bi, ti, kv: (bi, kv, 0, ti)),
        scratch_shapes=[pltpu.VMEM((tt, d), BF16)],
        compiler_params=_params(("parallel", "parallel", "arbitrary"), 40),
        name="kv_tail",
    )(x, nw, wt, kn_col)


def _attn_prompt_kernel(qb, q_ref, kc_ref, kp_ref, vc_ref, vp_ref, bias_ref, o_ref, m_ref, l_ref):
    i = pl.program_id(2)
    first = jnp.where(i == 0, 0, 1)
    lane = lax.broadcasted_iota(jnp.int32, (N_BACK, LANES), 1)
    lo_half = lane < ATT_HEAD_DIM
    npair = ATT_HEADS // 2
    stats = {}

    def scores(u, pr):
        rows = slice(u * N_BACK, (u + 1) * N_BACK)
        sl = slice(pr * LANES, (pr + 1) * LANES)
        qp = q_ref[0, 0, 0, rows, sl]
        kprev = kp_ref[0, 0, 0, :, sl] if u == 0 else kc_ref[0, 0, 0, (u - 1) * N_BACK:u * N_BACK, sl]
        kk = jnp.concatenate([kprev, kc_ref[0, 0, 0, rows, sl]], axis=0)
        zq = jnp.zeros_like(qp)
        q2 = jnp.concatenate([jnp.where(lo_half, qp, zq), jnp.where(lo_half, zq, qp)], axis=0)
        return _dot_nt(q2, kk)

    def finish(u, pr, s):
        rows = slice(u * N_BACK, (u + 1) * N_BACK)
        sl = slice(pr * LANES, (pr + 1) * LANES)
        vprev = vp_ref[0, 0, 0, :, sl] if u == 0 else vc_ref[0, 0, 0, (u - 1) * N_BACK:u * N_BACK, sl]
        vv = jnp.concatenate([vprev, vc_ref[0, 0, 0, rows, sl]], axis=0)
        s = s + (bias_ref[first, pr] if u == 0 else bias_ref[1, pr])
        mx = jnp.max(s, axis=-1, keepdims=True)
        p = jnp.exp2(s - mx)
        l = jnp.sum(p, axis=-1, keepdims=True)
        o2 = _dot(p.astype(BF16), vv)
        o_ref[0, 0, rows, sl] = jnp.where(lo_half, o2[:N_BACK], o2[N_BACK:]).astype(BF16)
        m_acc, l_acc = stats.get(u, (jnp.zeros((N_BACK, LANES), F32), jnp.ones((N_BACK, LANES), F32)))
        m_acc = jnp.where(lane == 2 * pr, mx[:N_BACK], jnp.where(lane == 2 * pr + 1, mx[N_BACK:], m_acc))
        l_acc = jnp.where(lane == 2 * pr, l[:N_BACK], jnp.where(lane == 2 * pr + 1, l[N_BACK:], l_acc))
        stats[u] = (m_acc, l_acc)
        if pr == npair - 1:
            m_ref[0, 0, rows, :] = m_acc
            l_ref[0, 0, rows, :] = l_acc

    pending = None
    for u in range(qb):
        for pr in range(npair):
            s = scores(u, pr)
            if pending is not None:
                finish(*pending)
            pending = (u, pr, s)
    finish(*pending)


def _attn_prompt(qkv_g, bias, qb):
    _, b, dil, l, d = qkv_g.shape
    rows = qb * N_BACK

    def cur(which):
        return pl.BlockSpec((1, 1, 1, rows, d), lambda bi, r, i: (which, bi, r, i, 0))

    def prev(which):
        return pl.BlockSpec((1, 1, 1, N_BACK, d),
                            lambda bi, r, i: (which, bi, r, jnp.maximum(i * qb - 1, 0), 0))

    return pl.pallas_call(
        functools.partial(_attn_prompt_kernel, qb),
        out_shape=(jax.ShapeDtypeStruct((b, dil, l, d), BF16),
                   jax.ShapeDtypeStruct((b, dil, l, LANES), F32),
                   jax.ShapeDtypeStruct((b, dil, l, LANES), F32)),
        grid=(b, dil, l // rows),
        in_specs=[cur(0), cur(1), prev(1), cur(2), prev(2),
                  pl.BlockSpec(bias.shape, lambda bi, r, i: (0, 0, 0, 0))],
        out_specs=(pl.BlockSpec((1, 1, rows, d), lambda bi, r, i: (bi, r, i, 0)),
                   pl.BlockSpec((1, 1, rows, LANES), lambda bi, r, i: (bi, r, i, 0)),
                   pl.BlockSpec((1, 1, rows, LANES), lambda bi, r, i: (bi, r, i, 0))),
        compiler_params=_params(("parallel", "parallel", "arbitrary"), 40),
        name="attn_prompt_d%d" % dil,
    )(qkv_g, qkv_g, qkv_g, qkv_g, qkv_g, bias)


def _merge_out_ffn_kernel(tm, o0_ref, o1_ref, o2_ref, m0_ref, m1_ref, m2_ref, l0_ref, l1_ref, l2_ref,
                          gexp_ref, x_ref, w_ref, nw_ref, wu_ref, wd_ref, y_ref,
                          os_ref, ms_ref, ls_ref, h_ref):
    d = x_ref.shape[-1]
    nslab = d // LANES

    def natural(o_ref, m_ref, l_ref, dil):
        if dil == 1:
            return o_ref[0, 0].astype(F32), m_ref[0, 0], l_ref[0, 0]
        n = tm // dil
        for r in range(dil):
            ms_ref[pl.ds(r, n, stride=dil), :] = m_ref[0, r]
            ls_ref[pl.ds(r, n, stride=dil), :] = l_ref[0, r]
            for c in range(nslab):
                os_ref.at[c][pl.ds(r, n, stride=dil), :] = o_ref[0, r, :, c * LANES:(c + 1) * LANES].astype(F32)
        return jnp.concatenate([os_ref[c] for c in range(nslab)], axis=1), ms_ref[...], ls_ref[...]

    trip = [natural(o_ref, m_ref, l_ref, GROUPS[g][1]) for g, (o_ref, m_ref, l_ref) in enumerate(
        ((o0_ref, m0_ref, l0_ref), (o1_ref, m1_ref, l1_ref), (o2_ref, m2_ref, l2_ref)))]
    mx = jnp.maximum(jnp.maximum(trip[0][1], trip[1][1]), trip[2][1])
    es = [jnp.exp2(t[1] - mx) for t in trip]
    inv = 1.0 / (es[0] * trip[0][2] + es[1] * trip[1][2] + es[2] * trip[2][2])
    acc = None
    for e, (o, _, _) in zip(es, trip):
        wt = e * inv
        hi = wt.astype(BF16)
        lo = (wt - hi.astype(F32)).astype(BF16)
        wx = _dot(jnp.concatenate([hi, lo], axis=1), gexp_ref[...])
        acc = wx * o if acc is None else acc + wx * o
    x = x_ref[...] + _dot(acc.astype(BF16), w_ref[...])
    _ffn_rows(x, nw_ref, wu_ref, wd_ref, y_ref, h_ref)


def _merge_out_ffn(outs, ms, ls, gexp, x, w, nw, wu, wd, t_len, tm):
    m, d = x.shape
    nt = t_len // tm
    row = pl.BlockSpec((tm, d), lambda i: (i, 0))
    o_specs, s_specs = [], []
    for _, dil in GROUPS:
        o_specs.append(pl.BlockSpec((1, dil, tm // dil, d), lambda i: (i // nt, 0, i % nt, 0)))
        s_specs.append(pl.BlockSpec((1, dil, tm // dil, LANES), lambda i: (i // nt, 0, i % nt, 0)))
    return pl.pallas_call(
        functools.partial(_merge_out_ffn_kernel, tm),
        out_shape=jax.ShapeDtypeStruct((m, d), F32),
        grid=(m // tm,),
        in_specs=o_specs + s_specs + s_specs + [_resident(gexp.shape), row, _resident(w.shape),
                                                pl.BlockSpec((1, d), lambda i: (0, 0)),
                                                _resident(wu.shape), _resident(wd.shape)],
        out_specs=row,
        scratch_shapes=[pltpu.VMEM((d // LANES, tm, LANES), F32), pltpu.VMEM((tm, LANES), F32),
                        pltpu.VMEM((tm, LANES), F32), pltpu.VMEM((tm, wu.shape[-1]), BF16)],
        compiler_params=_params(("parallel",), 56),
        name="merge_out_ffn",
    )(*outs, *ms, *ls, gexp, x, w, nw, wu, wd)


SAMPLE_PAIRS = 4


def _attn_sample_kernel(ts, *refs):
    ng = N_GROUPS
    q_refs, k_refs, v_refs = refs[0:ng], refs[ng:2 * ng], refs[2 * ng:3 * ng]
    c_refs = refs[3 * ng:4 * ng]
    bias_ref, o_ref = refs[4 * ng], refs[4 * ng + 1]
    npair = SAMPLE_PAIRS
    lane = lax.broadcasted_iota(jnp.int32, (ts, LANES), 1)
    lo_half = lane < ATT_HEAD_DIM
    zpad = jnp.zeros((LANES - npair * ts, LANES), F32)

    tk, vs = [], []
    for g in range(ng):
        kst = jnp.concatenate([k_refs[g][0, :, p * LANES:(p + 1) * LANES] for p in range(npair)] + [zpad], axis=0)
        vst = jnp.concatenate([v_refs[g][0, :, p * LANES:(p + 1) * LANES] for p in range(npair)] + [zpad], axis=0)
        tk.append(kst.T.astype(BF16))
        vs.append(vst.astype(BF16))

    rows = []
    for p in range(npair):
        sl = slice(p * LANES, (p + 1) * LANES)
        old, new = [], []
        for g in range(ng):
            qp = q_refs[g][0, :, sl]
            q2 = jnp.concatenate([jnp.where(lo_half, qp, 0.0), jnp.where(lo_half, 0.0, qp)], axis=0).astype(BF16)
            kt = c_refs[g][0, 0, 2 * p:2 * p + 2].reshape(LANES, c_refs[g].shape[-1]).astype(BF16)
            old.append(_dot(q2, kt))
            new.append(_dot(q2, tk[g]))
        rows.append(jnp.concatenate(old + new, axis=1))
    s = jnp.concatenate(rows, axis=0) + bias_ref[...].reshape(npair * 2 * ts, -1)
    mx = jnp.max(s, axis=-1, keepdims=True)
    pr = jnp.exp2(s - mx)
    inv = 1.0 / jnp.sum(pr, axis=-1, keepdims=True)
    prb = pr.astype(BF16)

    for p in range(npair):
        sl = slice(p * LANES, (p + 1) * LANES)
        r0 = p * 2 * ts
        acc = jnp.zeros((2 * ts, LANES), F32)
        off = 0
        for g in range(ng):
            window = c_refs[g].shape[-1]
            vt = c_refs[g][0, 1, 2 * p:2 * p + 2].reshape(LANES, window).astype(BF16)
            acc = acc + _dot_nt(prb[r0:r0 + 2 * ts, off:off + window], vt)
            off += window
        for g in range(ng):
            acc = acc + _dot(prb[r0:r0 + 2 * ts, off:off + LANES], vs[g])
            off += LANES
        om = acc * inv[r0:r0 + 2 * ts]
        o_ref[0, :, sl] = jnp.where(lo_half, om[:ts], om[ts:])


def _attn_sample(qkv32, caches_t, bias):
    b, ts, _ = qkv32.shape
    d = ATT_HEADS * ATT_HEAD_DIM
    width = SAMPLE_PAIRS * LANES
    per_blk = d // width
    hb = 2 * SAMPLE_PAIRS

    def new_rows(which, g):
        return pl.BlockSpec((1, ts, width), lambda bi, hq: (bi, 0, (which * N_GROUPS + g) * per_blk + hq))

    in_specs = [new_rows(w, g) for w in range(3) for g in range(N_GROUPS)]
    for c in caches_t:
        in_specs.append(pl.BlockSpec((1, 2, hb, ATT_HEAD_DIM, c.shape[-1]), lambda bi, hq: (bi, 0, hq, 0, 0)))
    in_specs.append(pl.BlockSpec((SAMPLE_PAIRS,) + bias.shape[1:], lambda bi, hq: (hq, 0, 0)))
    return pl.pallas_call(
        functools.partial(_attn_sample_kernel, ts),
        out_shape=jax.ShapeDtypeStruct((b, ts, d), F32),
        grid=(b, per_blk),
        in_specs=in_specs,
        out_specs=pl.BlockSpec((1, ts, width), lambda bi, hq: (bi, 0, hq)),
        compiler_params=_params(("parallel", "arbitrary"), 56),
        name="attn_sample",
    )(*([qkv32] * (3 * N_GROUPS)), *caches_t, bias)


def _matmul_res_kernel(a_ref, x_ref, w_ref, y_ref):
    y_ref[...] = x_ref[...] + _dot(a_ref[...].astype(BF16), w_ref[...])


def _matmul_res(a, x, w, tm):
    m, d = x.shape
    row = pl.BlockSpec((tm, d), lambda i: (i, 0))
    return pl.pallas_call(
        _matmul_res_kernel,
        out_shape=jax.ShapeDtypeStruct((m, d), F32),
        grid=(m // tm,),
        in_specs=[row, row, pl.BlockSpec((d, d), lambda i: (0, 0))],
        out_specs=row,
        compiler_params=_params(("parallel",), 40),
        name="matmul_res",
    )(a, x, w)


def _alibi_slopes():
    n = N_GROUPS * ATT_HEADS
    return LOG2E * (2.0 ** (-8.0 * np.arange(1, n + 1) / n)).reshape(N_GROUPS, ATT_HEADS)


def _prompt_bias(g):
    slopes = _alibi_slopes()[g] * GROUPS[g][1]
    dist = np.arange(N_BACK)[:, None] - np.arange(2 * N_BACK)[None, :] + N_BACK
    band = (dist >= 0) & (dist <= N_BACK)
    has_prev = np.arange(2 * N_BACK)[None, :] >= N_BACK
    out = np.empty((2, ATT_HEADS, N_BACK, 2 * N_BACK), np.float32)
    for k, valid in enumerate((band & has_prev, band)):
        out[k] = np.where(valid[None], -slopes[:, None, None] * dist[None], NEG)
    return jnp.asarray(out.reshape(2, ATT_HEADS // 2, 2 * N_BACK, 2 * N_BACK))


def _sample_bias(ts):
    slopes = _alibi_slopes()
    npairs = ATT_HEADS // 2
    old, new = [], []
    for g, (window, dil) in enumerate(GROUPS):
        pos = np.arange(window)
        bc = np.full((npairs, 2 * ts, window), NEG, np.float32)
        bn = np.full((npairs, 2 * ts, LANES), NEG, np.float32)
        for h in range(ATT_HEADS):
            for s in range(ts):
                row = (h % 2) * ts + s
                back = window + s - pos
                valid = (back % dil == 0) & (back <= window)
                bc[h // 2, row] = np.where(valid, -slopes[g, h] * back, NEG)
                lane0 = ((h // 2) % SAMPLE_PAIRS) * ts
                for s2 in range(s + 1):
                    if (s - s2) % dil == 0:
                        bn[h // 2, row, lane0 + s2] = -slopes[g, h] * (s - s2)
        old.append(bc)
        new.append(bn)
    return jnp.asarray(np.concatenate(old + new, axis=-1))


def _head_group_matrices():
    d = ATT_HEADS * ATT_HEAD_DIM
    i = np.arange(2 * LANES)
    gblk = (i[:, None] // ATT_HEAD_DIM == i[None, :] // ATT_HEAD_DIM).astype(np.float32)
    gexp = (np.arange(LANES)[:, None] == np.arange(d)[None, :] // ATT_HEAD_DIM).astype(np.float32)
    return jnp.asarray(gblk, BF16), jnp.asarray(np.concatenate([gexp, gexp], axis=0), BF16)


def _tile(m, want):
    return want if m % want == 0 else m


def kernel(x_prompt, x_sample, state_hgrn, cache_kv_w128, cache_kv_w512, cache_kv_w2048, hg_lb_logits, hg_w_q, hg_w_f, hg_w_i, hg_w_g, hg_w_o, hg_norm_o, att_w_qkv, att_w_o, att_q_norm, att_k_norm, norm_mix, norm_ffn, ffn_w_up, ffn_w_down):
    b, t_len, d = x_prompt.shape
    bs, ts, _ = x_sample.shape
    depth = norm_mix.shape[0]
    caches_all = (cache_kv_w128, cache_kv_w512, cache_kv_w2048)
    gblk, gexp = _head_group_matrices()
    sample_bias = _sample_bias(ts)

    yp = x_prompt.reshape(b * t_len, d)
    ys = x_sample.reshape(bs * ts, d)
    tm_p = _tile(b * t_len, 512)
    tm_s = _tile(bs * ts, 256)
    hg_p, hg_s = [], []
    kv_p = [[] for _ in GROUPS]
    kv_s = [[] for _ in GROUPS]
    for layer in range(depth):
        a = layer // 2
        nw = norm_mix[layer][None]
        nf = norm_ffn[layer][None]
        wu = ffn_w_up[layer].astype(BF16)
        wd = ffn_w_down[layer].astype(BF16)
        if layer % 2 == 0:
            w4 = jnp.stack([hg_w_q[a], hg_w_i[a], hg_w_g[a], hg_w_f[a]]).astype(BF16)
            wo = hg_w_o[a].astype(BF16)
            gn = hg_norm_o[a][None]
            zeros = jnp.zeros((b,) + state_hgrn.shape[2:], F32)
            tc = _tile(t_len, 512)
            f, qig = _hgrn_proj(yp, nw, hg_lb_logits, w4, layer, tm_p, BF16)
            o, sp = _hgrn_rec(f, qig, zeros, b, t_len, tc, 128)
            yp = _hgrn_out_ffn(o, qig, gn, yp, wo, nf, wu, wd, tm_p)
            f, qig = _hgrn_proj(ys, nw, hg_lb_logits, w4, layer, tm_s, F32)
            o, ss = _hgrn_rec(f, qig, state_hgrn[a], bs, ts, ts, 16)
            ys = _hgrn_out(o, qig, gn, ys, wo, tm_s)
            hg_p.append(sp)
            hg_s.append(ss)
        else:
            wqkv = att_w_qkv[a].astype(BF16)
            wo = att_w_o[a].astype(BF16)
            qn_row = jnp.tile(att_q_norm[a], ATT_HEADS)
            kn_row = jnp.tile(att_k_norm[a], ATT_HEADS)
            qkn = jnp.stack([qn_row * (ATT_SCALE * LOG2E), kn_row])
            qkv_groups = _qkv_perm(yp, nw, wqkv, qkn, gblk, b, t_len, tm_p)
            outs, ms, ls = [], [], []
            for g, (window, dil) in enumerate(GROUPS):
                qb = max(q for q in (1, 2, 4) if (t_len // dil) % (q * N_BACK) == 0)
                o, mrow, lrow = _attn_prompt(qkv_groups[g], _prompt_bias(g), qb)
                outs.append(o)
                ms.append(mrow)
                ls.append(lrow)
                keep = min(window, t_len)
                wt = jnp.stack([wqkv[:, (N_GROUPS + g) * d:(N_GROUPS + g + 1) * d].T,
                                wqkv[:, (2 * N_GROUPS + g) * d:(2 * N_GROUPS + g + 1) * d].T])
                kvt = _kv_tail(yp, nw, wt, kn_row[:, None], b, t_len, keep, _tile(keep, 512))
                kvt = kvt.reshape(b, 2, ATT_HEADS, ATT_HEAD_DIM, keep)
                kv_p[g].append(jnp.transpose(kvt, (0, 4, 1, 2, 3)))
            yp = _merge_out_ffn(outs, ms, ls, gexp, yp, wo, nf, wu, wd, t_len, tm_p)
            qkv32 = _qkv_plain(ys, nw, wqkv, qkn, gblk, tm_s).reshape(bs, ts, -1)
            caches_t = [jnp.transpose(c[a], (0, 2, 3, 4, 1)) for c in caches_all]
            om = _attn_sample(qkv32, caches_t, sample_bias)
            ys = _matmul_res(om.reshape(bs * ts, d), ys, wo, tm_s)
            kv = qkv32.reshape(bs, ts, 3, N_GROUPS, ATT_HEADS, ATT_HEAD_DIM)
            for g in range(N_GROUPS):
                kv_s[g].append(kv[:, :, 1:, g])
        ys = _ffn(ys, nf, wu, wd, tm_s, 1024)
    return (yp.reshape(b, t_len, d), ys.reshape(bs, ts, d), jnp.stack(hg_p), jnp.stack(hg_s),
            jnp.stack(kv_p[0]), jnp.stack(kv_s[0]),
            jnp.stack(kv_p[1]), jnp.stack(kv_s[1]),
            jnp.stack(kv_p[2]), jnp.stack(kv_s[2]))
```

```python
import functools

import numpy as np
import jax
import jax.numpy as jnp
from jax import lax
from jax.experimental import pallas as pl
from jax.experimental.pallas import tpu as pltpu

F32 = jnp.float32
BF16 = jnp.bfloat16

RMS_EPS = 1e-6
HG_HEADS = 8
HG_DK = 128
GROUPS = ((128, 1), (512, 4), (2048, 16))
N_GROUPS = len(GROUPS)
ATT_HEADS = 16
ATT_HEAD_DIM = 64
ATT_SCALE = ATT_HEAD_DIM ** -0.5
N_BACK = 128
NEG = -1e30
LOG2E = 1.4426950408889634

LANES = 128
MXU_WIDTH = 256
MIB = 1024 * 1024


def _dot(a, b):
    return jnp.dot(a, b, preferred_element_type=F32)


def _dot_nt(a, b):
    return lax.dot_general(a, b, (((1,), (1,)), ((), ())), preferred_element_type=F32)


def _dot_tn(a, b):
    return lax.dot_general(a, b, (((0,), (0,)), ((), ())), preferred_element_type=F32)


def _split3(x):
    hi = x.astype(BF16)
    r1 = x - hi.astype(F32)
    mid = r1.astype(BF16)
    lo = (r1 - mid.astype(F32)).astype(BF16)
    return hi, mid, lo


def _rms_rows(x, w):
    ms = jnp.mean(x * x, axis=-1, keepdims=True)
    return x * lax.rsqrt(ms + RMS_EPS) * w


def _params(sem, vmem_mib):
    return pltpu.CompilerParams(dimension_semantics=sem, vmem_limit_bytes=vmem_mib * MIB)


def _hgrn_proj_kernel(layer, x_ref, nw_ref, lbl_ref, w_ref, f_ref, o_ref):
    xn = _rms_rows(x_ref[...], nw_ref[...]).astype(BF16)
    lg = lbl_ref[...]
    e = jnp.exp(lg - jnp.max(lg, axis=0, keepdims=True))
    lb_row = jnp.sum(e[:layer + 1], axis=0, keepdims=True) / jnp.sum(e, axis=0, keepdims=True)
    for p in range(4):
        for c in range(0, x_ref.shape[-1], MXU_WIDTH):
            cols = slice(c, c + MXU_WIDTH)
            y = _dot(xn, w_ref[p, :, cols])
            if p == 1:
                o_ref[p, :, cols] = y.astype(o_ref.dtype)
            elif p == 3:
                lb = lb_row[:, cols]
                f_ref[:, cols] = lb + (1.0 - lb) * jax.nn.sigmoid(y)
            else:
                o_ref[p, :, cols] = (y * jax.nn.sigmoid(y)).astype(o_ref.dtype)


def _hgrn_proj(x, nw, lb_logits, w4, layer, tm, qig_dtype):
    m, d = x.shape
    return pl.pallas_call(
        functools.partial(_hgrn_proj_kernel, layer),
        out_shape=(jax.ShapeDtypeStruct((m, d), F32), jax.ShapeDtypeStruct((3, m, d), qig_dtype)),
        grid=(m // tm,),
        in_specs=[
            pl.BlockSpec((tm, d), lambda i: (i, 0)),
            pl.BlockSpec((1, d), lambda i: (0, 0)),
            pl.BlockSpec(lb_logits.shape, lambda i: (0, 0)),
            pl.BlockSpec((4, d, d), lambda i: (0, 0, 0)),
        ],
        out_specs=(pl.BlockSpec((tm, d), lambda i: (i, 0)),
                   pl.BlockSpec((3, tm, d), lambda i: (0, i, 0))),
        compiler_params=_params(("parallel",), 48),
        name="hgrn_proj",
    )(x, nw, lb_logits, w4)


def _hgrn_rec_kernel(tc, c, q_ref, f_ref, v_ref, s0_ref, o_ref, so_ref, st_ref):
    t = pl.program_id(1)
    dk = HG_DK

    @pl.when(t == 0)
    def _():
        for h in range(HG_HEADS):
            st_ref[h] = s0_ref[0, h].T

    row = lax.broadcasted_iota(jnp.int32, (c, c), 0)
    col = lax.broadcasted_iota(jnp.int32, (c, c), 1)
    tril = jnp.where(row >= col, 1.0, 0.0).astype(BF16)
    rowl = lax.broadcasted_iota(jnp.int32, (c, dk), 0)
    n0 = min(c, 16)
    same_block = {}
    n = c // 2
    while n >= n0:
        shift = n.bit_length() - 1
        same_block[n] = (row >> shift) == (col >> shift)
        n //= 2

    def block_ref_rows(ch, n, pick):
        parts = [jnp.broadcast_to(ch[j * n + pick:j * n + pick + 1, :], (n, dk)) for j in range(c // n)]
        return parts[0] if len(parts) == 1 else jnp.concatenate(parts, axis=0)

    def do_chunks(nchunk, get_q, get_f, get_v, store):
        cums = []
        for ci in range(nchunk):
            hi, mid, lo = _split3(jnp.log(get_f(ci, slice(None))))
            cums.append(_dot(tril, hi) + _dot(tril, mid) + _dot(tril, lo))

        def first_stage(ci, h):
            sl = slice(h * dk, (h + 1) * dk)
            qh, kh, vh, ch = get_q(ci, sl), 1.0 - get_f(ci, sl), get_v(ci, sl), cums[ci][:, sl]
            s_t = st_ref[h]
            o = _dot_nt((qh * jnp.exp(ch)).astype(BF16), s_t.astype(BF16))
            last = ch[c - 1:c, :]
            kd = kh * jnp.exp(last - ch)
            s_new = s_t * jnp.exp(last) + _dot_tn(vh.astype(BF16), kd.astype(BF16))
            pieces = []
            n = c
            while n > n0:
                half = n // 2
                b = block_ref_rows(ch, n, half - 1)
                upper = (rowl & (n - 1)) >= half
                e = jnp.exp(jnp.where(upper, ch - b, b - ch))
                qt = jnp.where(upper, qh * e, 0.0).astype(BF16)
                kt = jnp.where(upper, 0.0, kh * e).astype(BF16)
                pieces.append((n, _dot_nt(qt, kt)))
                n = half
            dq = ch - block_ref_rows(ch, n0, n0 // 2 - 1)
            pieces.append((n0, _dot_nt((qh * jnp.exp(dq)).astype(BF16), (kh * jnp.exp(-dq)).astype(BF16))))
            return ci, h, s_new, o, pieces

        def second_stage(ci, h, s_new, o, pieces):
            sl = slice(h * dk, (h + 1) * dk)
            vh = get_v(ci, sl)
            attn = None
            for n, a in pieces:
                if n == n0:
                    keep = col <= row
                    if n0 < c:
                        keep = keep & same_block[n0]
                    a = jnp.where(keep, a, 0.0)
                elif n < c:
                    a = jnp.where(same_block[n], a, 0.0)
                attn = a if attn is None else attn + a
            store(ci, sl, o + _dot(attn.astype(BF16), vh.astype(BF16)))
            st_ref[h] = s_new

        pending = None
        for ci in range(nchunk):
            for h in range(HG_HEADS):
                nxt = first_stage(ci, h)
                if pending is not None:
                    second_stage(*pending)
                pending = nxt
        second_stage(*pending)

    if tc >= c:
        def rows(ci):
            return slice(ci * c, (ci + 1) * c)

        def store(ci, sl, o):
            o_ref[rows(ci), sl] = o

        do_chunks(tc // c,
                  lambda ci, sl: q_ref[0, rows(ci), sl].astype(F32),
                  lambda ci, sl: f_ref[rows(ci), sl],
                  lambda ci, sl: v_ref[0, rows(ci), sl].astype(F32), store)
    else:
        pad = c - tc
        z = jnp.zeros((pad, q_ref.shape[-1]), F32)
        qp = jnp.concatenate([q_ref[0].astype(F32), z], axis=0)
        fp = jnp.concatenate([f_ref[...], z + 1.0], axis=0)
        vp = jnp.concatenate([v_ref[0].astype(F32), z], axis=0)

        def store(ci, sl, o):
            o_ref[:, sl] = o[:tc]

        do_chunks(1, lambda ci, sl: qp[:, sl], lambda ci, sl: fp[:, sl], lambda ci, sl: vp[:, sl], store)

    @pl.when(t == pl.num_programs(1) - 1)
    def _():
        for h in range(HG_HEADS):
            so_ref[0, h] = st_ref[h].T


def _hgrn_rec(f, qig, s0, b, t_len, tc, c):
    m, d = f.shape
    nt = t_len // tc
    h, dk, dv = s0.shape[1:]

    def row_spec(which):
        return pl.BlockSpec((1, tc, d), lambda bi, ti: (which, bi * nt + ti, 0))

    return pl.pallas_call(
        functools.partial(_hgrn_rec_kernel, tc, c),
        out_shape=(jax.ShapeDtypeStruct((m, d), F32), jax.ShapeDtypeStruct(s0.shape, F32)),
        grid=(b, nt),
        in_specs=[row_spec(0), pl.BlockSpec((tc, d), lambda bi, ti: (bi * nt + ti, 0)), row_spec(1),
                  pl.BlockSpec((1, h, dk, dv), lambda bi, ti: (bi, 0, 0, 0))],
        out_specs=(pl.BlockSpec((tc, d), lambda bi, ti: (bi * nt + ti, 0)),
                   pl.BlockSpec((1, h, dk, dv), lambda bi, ti: (bi, 0, 0, 0))),
        scratch_shapes=[pltpu.VMEM((h, dv, dk), F32)],
        compiler_params=_params(("parallel", "arbitrary"), 48),
        name="hgrn_rec",
    )(qig, f, qig, s0)


def _hgrn_out_kernel(o_ref, g_ref, gn_ref, x_ref, w_ref, y_ref):
    a = _rms_rows(o_ref[...], gn_ref[...]) * g_ref[0].astype(F32)
    y_ref[...] = x_ref[...] + _dot(a.astype(BF16), w_ref[...])


def _hgrn_out(o, qig, gn, x, w, tm):
    m, d = x.shape
    return pl.pallas_call(
        _hgrn_out_kernel,
        out_shape=jax.ShapeDtypeStruct((m, d), F32),
        grid=(m // tm,),
        in_specs=[
            pl.BlockSpec((tm, d), lambda i: (i, 0)),
            pl.BlockSpec((1, tm, d), lambda i: (2, i, 0)),
            pl.BlockSpec((1, d), lambda i: (0, 0)),
            pl.BlockSpec((tm, d), lambda i: (i, 0)),
            pl.BlockSpec((d, d), lambda i: (0, 0)),
        ],
        out_specs=pl.BlockSpec((tm, d), lambda i: (i, 0)),
        compiler_params=_params(("parallel",), 40),
        name="hgrn_out",
    )(o, qig, gn, x, w)


def _ffn_kernel(x_ref, nw_ref, wu_ref, wd_ref, y_ref, xn_ref):
    j = pl.program_id(1)

    @pl.when(j == 0)
    def _():
        x = x_ref[...]
        xn_ref[...] = _rms_rows(x, nw_ref[...]).astype(BF16)
        y_ref[...] = x

    h = jnp.maximum(_dot(xn_ref[...], wu_ref[...]), 0.0)
    y_ref[...] += _dot((h * h).astype(BF16), wd_ref[...])


def _ffn(x, nw, wu, wd, tm, tf):
    m, d = x.shape
    ff = wu.shape[1]
    return pl.pallas_call(
        _ffn_kernel,
        out_shape=jax.ShapeDtypeStruct((m, d), F32),
        grid=(m // tm, ff // tf),
        in_specs=[
            pl.BlockSpec((tm, d), lambda i, j: (i, 0)),
            pl.BlockSpec((1, d), lambda i, j: (0, 0)),
            pl.BlockSpec((d, tf), lambda i, j: (0, j)),
            pl.BlockSpec((tf, d), lambda i, j: (j, 0)),
        ],
        out_specs=pl.BlockSpec((tm, d), lambda i, j: (i, 0)),
        scratch_shapes=[pltpu.VMEM((tm, d), BF16)],
        compiler_params=_params(("parallel", "arbitrary"), 48),
        name="ffn",
    )(x, nw, wu, wd)


def _ffn_rows(x, nw_ref, wu_ref, wd_ref, y_ref, h_ref):
    xn = _rms_rows(x, nw_ref[...]).astype(BF16)
    for c in range(0, wu_ref.shape[-1], MXU_WIDTH):
        h = jnp.maximum(_dot(xn, wu_ref[:, c:c + MXU_WIDTH]), 0.0)
        h_ref[:, c:c + MXU_WIDTH] = (h * h).astype(BF16)
    hb = h_ref[...]
    for c in range(0, wd_ref.shape[-1], MXU_WIDTH):
        y_ref[:, c:c + MXU_WIDTH] = x[:, c:c + MXU_WIDTH] + _dot(hb, wd_ref[:, c:c + MXU_WIDTH])


def _hgrn_out_ffn_kernel(o_ref, g_ref, gn_ref, x_ref, wo_ref, nw_ref, wu_ref, wd_ref, y_ref, h_ref):
    a = _rms_rows(o_ref[...], gn_ref[...]) * g_ref[0].astype(F32)
    x = x_ref[...] + _dot(a.astype(BF16), wo_ref[...])
    _ffn_rows(x, nw_ref, wu_ref, wd_ref, y_ref, h_ref)


def _resident(shape):
    return pl.BlockSpec(shape, lambda i: (0,) * len(shape), pipeline_mode=pl.Buffered(1))


def _hgrn_out_ffn(o, qig, gn, x, wo, nw, wu, wd, tm):
    m, d = x.shape
    row = pl.BlockSpec((tm, d), lambda i: (i, 0))
    vec = pl.BlockSpec((1, d), lambda i: (0, 0))
    return pl.pallas_call(
        _hgrn_out_ffn_kernel,
        out_shape=jax.ShapeDtypeStruct((m, d), F32),
        grid=(m // tm,),
        in_specs=[row, pl.BlockSpec((1, tm, d), lambda i: (2, i, 0)), vec, row, _resident(wo.shape),
                  vec, _resident(wu.shape), _resident(wd.shape)],
        out_specs=row,
        scratch_shapes=[pltpu.VMEM((tm, wu.shape[-1]), BF16)],
        compiler_params=_params(("parallel",), 56),
        name="hgrn_out_ffn",
    )(o, qig, gn, x, wo, nw, wu, wd)


def _qk_norm_rows(y, qkn, gblk):
    y2 = (y * y).astype(BF16)
    w = gblk.shape[0]
    ss = jnp.concatenate([_dot(y2[:, c * w:(c + 1) * w], gblk) for c in range(y.shape[-1] // w)], axis=1)
    return y * lax.rsqrt(ss * (1.0 / ATT_HEAD_DIM) + RMS_EPS) * qkn


def _qkv_perm_kernel(tm, x_ref, nw_ref, w_ref, qkn_ref, gblk_ref, o0_ref, o1_ref, o2_ref, xs_ref, xp_ref):
    outs = (o0_ref, o1_ref, o2_ref)
    d = x_ref.shape[-1]
    nslab = d // LANES
    xn = _rms_rows(x_ref[...], nw_ref[...])
    xp_ref[0] = xn.astype(BF16)
    for c in range(nslab):
        xs_ref[c] = xn[:, c * LANES:(c + 1) * LANES]
    for g in range(1, N_GROUPS):
        dil = GROUPS[g][1]
        n = tm // dil
        for c in range(nslab):
            for r in range(dil):
                xp_ref[g, r * n:(r + 1) * n, c * LANES:(c + 1) * LANES] = (
                    xs_ref.at[c][pl.ds(r, n, stride=dil), :].astype(BF16))
    gblk = gblk_ref[...]

    def finish(g, which, c, y):
        dil = GROUPS[g][1]
        if which < 2:
            ss = _dot((y * y).astype(BF16), gblk)
            y = y * lax.rsqrt(ss * (1.0 / ATT_HEAD_DIM) + RMS_EPS) * qkn_ref[which:which + 1, c:c + MXU_WIDTH]
        outs[g][which, 0, :, :, c:c + MXU_WIDTH] = y.astype(BF16).reshape(dil, tm // dil, MXU_WIDTH)

    pending = None
    for g in range(N_GROUPS):
        xg = xp_ref[g]
        for which in range(3):
            base = (which * N_GROUPS + g) * d
            for c in range(0, d, MXU_WIDTH):
                y = _dot(xg, w_ref[:, base + c:base + c + MXU_WIDTH])
                if pending is not None:
                    finish(*pending)
                pending = (g, which, c, y)
    finish(*pending)


def _qkv_perm(x, nw, w, qkn, gblk, b, t_len, tm):
    m, d = x.shape
    nt = t_len // tm
    out_shapes, out_specs = [], []
    for _, dil in GROUPS:
        out_shapes.append(jax.ShapeDtypeStruct((3, b, dil, t_len // dil, d), BF16))
        out_specs.append(pl.BlockSpec((3, 1, dil, tm // dil, d), lambda i: (0, i // nt, 0, i % nt, 0)))
    return pl.pallas_call(
        functools.partial(_qkv_perm_kernel, tm),
        out_shape=tuple(out_shapes),
        grid=(m // tm,),
        in_specs=[
            pl.BlockSpec((tm, d), lambda i: (i, 0)),
            pl.BlockSpec((1, d), lambda i: (0, 0)),
            pl.BlockSpec(w.shape, lambda i: (0, 0), pipeline_mode=pl.Buffered(1)),
            pl.BlockSpec(qkn.shape, lambda i: (0, 0)),
            pl.BlockSpec(gblk.shape, lambda i: (0, 0)),
        ],
        out_specs=tuple(out_specs),
        scratch_shapes=[pltpu.VMEM((d // LANES, tm, LANES), F32),
                        pltpu.VMEM((N_GROUPS, tm, d), BF16)],
        compiler_params=_params(("parallel",), 56),
        name="qkv_perm",
    )(x, nw, w, qkn, gblk)


def _qkv_plain_kernel(x_ref, nw_ref, w_ref, qkn_ref, gblk_ref, o_ref, xn_ref):
    j = pl.program_id(1)

    @pl.when(j == 0)
    def _():
        xn_ref[...] = _rms_rows(x_ref[...], nw_ref[...]).astype(BF16)

    y = _dot(xn_ref[...], w_ref[...])

    @pl.when(j < 2 * N_GROUPS)
    def _():
        gain = jnp.where(j < N_GROUPS, qkn_ref[0:1, :], qkn_ref[1:2, :])
        o_ref[...] = _qk_norm_rows(y, gain, gblk_ref[...])

    @pl.when(j >= 2 * N_GROUPS)
    def _():
        o_ref[...] = y


def _qkv_plain(x, nw, w, qkn, gblk, tm):
    m, d = x.shape
    n = w.shape[1]
    return pl.pallas_call(
        _qkv_plain_kernel,
        out_shape=jax.ShapeDtypeStruct((m, n), F32),
        grid=(m // tm, n // d),
        in_specs=[
            pl.BlockSpec((tm, d), lambda i, j: (i, 0)),
            pl.BlockSpec((1, d), lambda i, j: (0, 0)),
            pl.BlockSpec((d, d), lambda i, j: (0, j)),
            pl.BlockSpec(qkn.shape, lambda i, j: (0, 0)),
            pl.BlockSpec(gblk.shape, lambda i, j: (0, 0)),
        ],
        out_specs=pl.BlockSpec((tm, d), lambda i, j: (i, j)),
        scratch_shapes=[pltpu.VMEM((tm, d), BF16)],
        compiler_params=_params(("parallel", "arbitrary"), 40),
        name="qkv_plain",
    )(x, nw, w, qkn, gblk)


def _kv_tail_kernel(x_ref, nw_ref, wt_ref, kn_ref, o_ref, xn_ref):
    xn_ref[...] = _rms_rows(x_ref[...], nw_ref[...]).astype(BF16)
    tt = x_ref.shape[0]
    slab = min(tt, MXU_WIDTH)
    for kv in range(2):
        for c in range(0, tt, slab):
            yt = _dot_nt(wt_ref[kv], xn_ref[c:c + slab, :])
            if kv == 0:
                y3 = yt.reshape(ATT_HEADS, ATT_HEAD_DIM, slab)
                ms = jnp.mean(y3 * y3, axis=1, keepdims=True)
                yt = (y3 * lax.rsqrt(ms + RMS_EPS)).reshape(yt.shape) * kn_ref[...]
            o_ref[0, kv, :, c:c + slab] = yt


def _kv_tail(x, nw, wt, kn_col, b, t_len, keep, tt):
    m, d = x.shape
    first = (t_len - keep) // tt
    per_seq = t_len // tt
    return pl.pallas_call(
        _kv_tail_kernel,
        out_shape=jax.ShapeDtypeStruct((b, 2, d, keep), F32),
        grid=(b, keep // tt),
        in_specs=[
            pl.BlockSpec((tt, d), lambda bi, ti: (bi * per_seq + first + ti, 0)),
            pl.BlockSpec((1, d), lambda bi, ti: (0, 0)),
            pl.BlockSpec((2, d, d), lambda bi, ti: (0, 0, 0)),
            pl.BlockSpec((d, 1), lambda bi, ti: (0, 0)),
        ],
        out_specs=pl.BlockSpec((1, 2, d, tt), lambda bi, ti: (bi, 0, 0, ti)),
        scratch_shapes=[pltpu.VMEM((tt, d), BF16)],
        compiler_params=_params(("parallel", "parallel"), 40),
        name="kv_tail",
    )(x, nw, wt, kn_col)


def _attn_prompt_kernel(rb, qb, q_ref, kc_ref, kp_ref, vc_ref, vp_ref, bias_ref, o_ref, m_ref, l_ref):
    i = pl.program_id(2)
    first = jnp.where(i == 0, 0, 1)
    lane = lax.broadcasted_iota(jnp.int32, (N_BACK, LANES), 1)
    lo_half = lane < ATT_HEAD_DIM
    npair = ATT_HEADS // 2
    stats = {}

    def scores(r, u, pr):
        rows = slice(u * N_BACK, (u + 1) * N_BACK)
        sl = slice(pr * LANES, (pr + 1) * LANES)
        qp = q_ref[0, 0, r, rows, sl]
        kprev = kp_ref[0, 0, r, :, sl] if u == 0 else kc_ref[0, 0, r, (u - 1) * N_BACK:u * N_BACK, sl]
        kk = jnp.concatenate([kprev, kc_ref[0, 0, r, rows, sl]], axis=0)
        zq = jnp.zeros_like(qp)
        q2 = jnp.concatenate([jnp.where(lo_half, qp, zq), jnp.where(lo_half, zq, qp)], axis=0)
        return _dot_nt(q2, kk)

    def finish(r, u, pr, s):
        rows = slice(u * N_BACK, (u + 1) * N_BACK)
        sl = slice(pr * LANES, (pr + 1) * LANES)
        vprev = vp_ref[0, 0, r, :, sl] if u == 0 else vc_ref[0, 0, r, (u - 1) * N_BACK:u * N_BACK, sl]
        vv = jnp.concatenate([vprev, vc_ref[0, 0, r, rows, sl]], axis=0)
        s = s + (bias_ref[first, pr] if u == 0 else bias_ref[1, pr])
        mx = jnp.max(s, axis=-1, keepdims=True)
        p = jnp.exp2(s - mx)
        l = jnp.sum(p, axis=-1, keepdims=True)
        o2 = _dot(p.astype(BF16), vv)
        o_ref[0, r, rows, sl] = jnp.where(lo_half, o2[:N_BACK], o2[N_BACK:]).astype(BF16)
        m_acc, l_acc = stats.get((r, u), (jnp.zeros((N_BACK, LANES), F32), jnp.ones((N_BACK, LANES), F32)))
        m_acc = jnp.where(lane == 2 * pr, mx[:N_BACK], jnp.where(lane == 2 * pr + 1, mx[N_BACK:], m_acc))
        l_acc = jnp.where(lane == 2 * pr, l[:N_BACK], jnp.where(lane == 2 * pr + 1, l[N_BACK:], l_acc))
        stats[(r, u)] = (m_acc, l_acc)
        if pr == npair - 1:
            m_ref[0, r, rows, :] = m_acc
            l_ref[0, r, rows, :] = l_acc

    pending = None
    for r in range(rb):
        for u in range(qb):
            for pr in range(npair):
                s = scores(r, u, pr)
                if pending is not None:
                    finish(*pending)
                pending = (r, u, pr, s)
    finish(*pending)


def _attn_prompt(qkv_g, bias, rb, qb):
    _, b, dil, l, d = qkv_g.shape
    rows = qb * N_BACK

    def cur(which):
        return pl.BlockSpec((1, 1, rb, rows, d), lambda bi, r, i: (which, bi, r, i, 0))

    def prev(which):
        return pl.BlockSpec((1, 1, rb, N_BACK, d),
                            lambda bi, r, i: (which, bi, r, jnp.maximum(i * qb - 1, 0), 0))

    return pl.pallas_call(
        functools.partial(_attn_prompt_kernel, rb, qb),
        out_shape=(jax.ShapeDtypeStruct((b, dil, l, d), BF16),
                   jax.ShapeDtypeStruct((b, dil, l, LANES), F32),
                   jax.ShapeDtypeStruct((b, dil, l, LANES), F32)),
        grid=(b, dil // rb, l // rows),
        in_specs=[cur(0), cur(1), prev(1), cur(2), prev(2),
                  pl.BlockSpec(bias.shape, lambda bi, r, i: (0, 0, 0, 0))],
        out_specs=(pl.BlockSpec((1, rb, rows, d), lambda bi, r, i: (bi, r, i, 0)),
                   pl.BlockSpec((1, rb, rows, LANES), lambda bi, r, i: (bi, r, i, 0)),
                   pl.BlockSpec((1, rb, rows, LANES), lambda bi, r, i: (bi, r, i, 0))),
        compiler_params=_params(("parallel", "parallel", "arbitrary"), 40),
        name="attn_prompt_d%d" % dil,
    )(qkv_g, qkv_g, qkv_g, qkv_g, qkv_g, bias)


def _merge_out_ffn_kernel(tm, o0_ref, o1_ref, o2_ref, m0_ref, m1_ref, m2_ref, l0_ref, l1_ref, l2_ref,
                          gexp_ref, x_ref, w_ref, nw_ref, wu_ref, wd_ref, y_ref,
                          os_ref, ms_ref, ls_ref, h_ref):
    d = x_ref.shape[-1]
    nslab = d // LANES

    def natural(o_ref, m_ref, l_ref, dil):
        if dil == 1:
            return o_ref[0, 0].astype(F32), m_ref[0, 0], l_ref[0, 0]
        n = tm // dil
        for r in range(dil):
            ms_ref[pl.ds(r, n, stride=dil), :] = m_ref[0, r]
            ls_ref[pl.ds(r, n, stride=dil), :] = l_ref[0, r]
            for c in range(nslab):
                os_ref.at[c][pl.ds(r, n, stride=dil), :] = o_ref[0, r, :, c * LANES:(c + 1) * LANES].astype(F32)
        return jnp.concatenate([os_ref[c] for c in range(nslab)], axis=1), ms_ref[...], ls_ref[...]

    trip = [natural(o_ref, m_ref, l_ref, GROUPS[g][1]) for g, (o_ref, m_ref, l_ref) in enumerate(
        ((o0_ref, m0_ref, l0_ref), (o1_ref, m1_ref, l1_ref), (o2_ref, m2_ref, l2_ref)))]
    mx = jnp.maximum(jnp.maximum(trip[0][1], trip[1][1]), trip[2][1])
    es = [jnp.exp2(t[1] - mx) for t in trip]
    inv = 1.0 / (es[0] * trip[0][2] + es[1] * trip[1][2] + es[2] * trip[2][2])
    acc = None
    for e, (o, _, _) in zip(es, trip):
        wt = e * inv
        hi = wt.astype(BF16)
        lo = (wt - hi.astype(F32)).astype(BF16)
        wx = _dot(jnp.concatenate([hi, lo], axis=1), gexp_ref[...])
        acc = wx * o if acc is None else acc + wx * o
    x = x_ref[...] + _dot(acc.astype(BF16), w_ref[...])
    _ffn_rows(x, nw_ref, wu_ref, wd_ref, y_ref, h_ref)


def _merge_out_ffn(outs, ms, ls, gexp, x, w, nw, wu, wd, t_len, tm):
    m, d = x.shape
    nt = t_len // tm
    row = pl.BlockSpec((tm, d), lambda i: (i, 0))
    o_specs, s_specs = [], []
    for _, dil in GROUPS:
        o_specs.append(pl.BlockSpec((1, dil, tm // dil, d), lambda i: (i // nt, 0, i % nt, 0)))
        s_specs.append(pl.BlockSpec((1, dil, tm // dil, LANES), lambda i: (i // nt, 0, i % nt, 0)))
    return pl.pallas_call(
        functools.partial(_merge_out_ffn_kernel, tm),
        out_shape=jax.ShapeDtypeStruct((m, d), F32),
        grid=(m // tm,),
        in_specs=o_specs + s_specs + s_specs + [_resident(gexp.shape), row, _resident(w.shape),
                                                pl.BlockSpec((1, d), lambda i: (0, 0)),
                                                _resident(wu.shape), _resident(wd.shape)],
        out_specs=row,
        scratch_shapes=[pltpu.VMEM((d // LANES, tm, LANES), F32), pltpu.VMEM((tm, LANES), F32),
                        pltpu.VMEM((tm, LANES), F32), pltpu.VMEM((tm, wu.shape[-1]), BF16)],
        compiler_params=_params(("parallel",), 56),
        name="merge_out_ffn",
    )(*outs, *ms, *ls, gexp, x, w, nw, wu, wd)


SAMPLE_PAIRS = 4


def _attn_sample_kernel(ts, *refs):
    ng = N_GROUPS
    q_refs, k_refs, v_refs = refs[0:ng], refs[ng:2 * ng], refs[2 * ng:3 * ng]
    c_refs = refs[3 * ng:4 * ng]
    bias_ref, o_ref = refs[4 * ng], refs[4 * ng + 1]
    npair = SAMPLE_PAIRS
    lane = lax.broadcasted_iota(jnp.int32, (ts, LANES), 1)
    lo_half = lane < ATT_HEAD_DIM
    zpad = jnp.zeros((LANES - npair * ts, LANES), F32)

    tk, vs = [], []
    for g in range(ng):
        kst = jnp.concatenate([k_refs[g][0, :, p * LANES:(p + 1) * LANES] for p in range(npair)] + [zpad], axis=0)
        vst = jnp.concatenate([v_refs[g][0, :, p * LANES:(p + 1) * LANES] for p in range(npair)] + [zpad], axis=0)
        tk.append(kst.T.astype(BF16))
        vs.append(vst.astype(BF16))

    rows = []
    for p in range(npair):
        sl = slice(p * LANES, (p + 1) * LANES)
        old, new = [], []
        for g in range(ng):
            qp = q_refs[g][0, :, sl]
            q2 = jnp.concatenate([jnp.where(lo_half, qp, 0.0), jnp.where(lo_half, 0.0, qp)], axis=0).astype(BF16)
            kt = c_refs[g][0, 0, 2 * p:2 * p + 2].reshape(LANES, c_refs[g].shape[-1]).astype(BF16)
            old.append(_dot(q2, kt))
            new.append(_dot(q2, tk[g]))
        rows.append(jnp.concatenate(old + new, axis=1))
    s = jnp.concatenate(rows, axis=0) + bias_ref[...].reshape(npair * 2 * ts, -1)
    mx = jnp.max(s, axis=-1, keepdims=True)
    pr = jnp.exp2(s - mx)
    inv = 1.0 / jnp.sum(pr, axis=-1, keepdims=True)
    prb = pr.astype(BF16)

    for p in range(npair):
        sl = slice(p * LANES, (p + 1) * LANES)
        r0 = p * 2 * ts
        acc = jnp.zeros((2 * ts, LANES), F32)
        off = 0
        for g in range(ng):
            window = c_refs[g].shape[-1]
            vt = c_refs[g][0, 1, 2 * p:2 * p + 2].reshape(LANES, window).astype(BF16)
            acc = acc + _dot_nt(prb[r0:r0 + 2 * ts, off:off + window], vt)
            off += window
        for g in range(ng):
            acc = acc + _dot(prb[r0:r0 + 2 * ts, off:off + LANES], vs[g])
            off += LANES
        om = acc * inv[r0:r0 + 2 * ts]
        o_ref[0, :, sl] = jnp.where(lo_half, om[:ts], om[ts:])


def _attn_sample(qkv32, caches_t, bias):
    b, ts, _ = qkv32.shape
    d = ATT_HEADS * ATT_HEAD_DIM
    width = SAMPLE_PAIRS * LANES
    per_blk = d // width
    hb = 2 * SAMPLE_PAIRS

    def new_rows(which, g):
        return pl.BlockSpec((1, ts, width), lambda bi, hq: (bi, 0, (which * N_GROUPS + g) * per_blk + hq))

    in_specs = [new_rows(w, g) for w in range(3) for g in range(N_GROUPS)]
    for c in caches_t:
        in_specs.append(pl.BlockSpec((1, 2, hb, ATT_HEAD_DIM, c.shape[-1]), lambda bi, hq: (bi, 0, hq, 0, 0)))
    in_specs.append(pl.BlockSpec((SAMPLE_PAIRS,) + bias.shape[1:], lambda bi, hq: (hq, 0, 0)))
    return pl.pallas_call(
        functools.partial(_attn_sample_kernel, ts),
        out_shape=jax.ShapeDtypeStruct((b, ts, d), F32),
        grid=(b, per_blk),
        in_specs=in_specs,
        out_specs=pl.BlockSpec((1, ts, width), lambda bi, hq: (bi, 0, hq)),
        compiler_params=_params(("parallel", "arbitrary"), 56),
        name="attn_sample",
    )(*([qkv32] * (3 * N_GROUPS)), *caches_t, bias)


def _matmul_res_kernel(a_ref, x_ref, w_ref, y_ref):
    y_ref[...] = x_ref[...] + _dot(a_ref[...].astype(BF16), w_ref[...])


def _matmul_res(a, x, w, tm):
    m, d = x.shape
    row = pl.BlockSpec((tm, d), lambda i: (i, 0))
    return pl.pallas_call(
        _matmul_res_kernel,
        out_shape=jax.ShapeDtypeStruct((m, d), F32),
        grid=(m // tm,),
        in_specs=[row, row, pl.BlockSpec((d, d), lambda i: (0, 0))],
        out_specs=row,
        compiler_params=_params(("parallel",), 40),
        name="matmul_res",
    )(a, x, w)


def _alibi_slopes():
    n = N_GROUPS * ATT_HEADS
    return LOG2E * (2.0 ** (-8.0 * np.arange(1, n + 1) / n)).reshape(N_GROUPS, ATT_HEADS)


def _prompt_bias(g):
    slopes = _alibi_slopes()[g] * GROUPS[g][1]
    dist = np.arange(N_BACK)[:, None] - np.arange(2 * N_BACK)[None, :] + N_BACK
    band = (dist >= 0) & (dist <= N_BACK)
    has_prev = np.arange(2 * N_BACK)[None, :] >= N_BACK
    out = np.empty((2, ATT_HEADS, N_BACK, 2 * N_BACK), np.float32)
    for k, valid in enumerate((band & has_prev, band)):
        out[k] = np.where(valid[None], -slopes[:, None, None] * dist[None], NEG)
    return jnp.asarray(out.reshape(2, ATT_HEADS // 2, 2 * N_BACK, 2 * N_BACK))


def _sample_bias(ts):
    slopes = _alibi_slopes()
    npairs = ATT_HEADS // 2
    old, new = [], []
    for g, (window, dil) in enumerate(GROUPS):
        pos = np.arange(window)
        bc = np.full((npairs, 2 * ts, window), NEG, np.float32)
        bn = np.full((npairs, 2 * ts, LANES), NEG, np.float32)
        for h in range(ATT_HEADS):
            for s in range(ts):
                row = (h % 2) * ts + s
                back = window + s - pos
                valid = (back % dil == 0) & (back <= window)
                bc[h // 2, row] = np.where(valid, -slopes[g, h] * back, NEG)
                lane0 = ((h // 2) % SAMPLE_PAIRS) * ts
                for s2 in range(s + 1):
                    if (s - s2) % dil == 0:
                        bn[h // 2, row, lane0 + s2] = -slopes[g, h] * (s - s2)
        old.append(bc)
        new.append(bn)
    return jnp.asarray(np.concatenate(old + new, axis=-1))


def _head_group_matrices():
    d = ATT_HEADS * ATT_HEAD_DIM
    i = np.arange(2 * LANES)
    gblk = (i[:, None] // ATT_HEAD_DIM == i[None, :] // ATT_HEAD_DIM).astype(np.float32)
    gexp = (np.arange(LANES)[:, None] == np.arange(d)[None, :] // ATT_HEAD_DIM).astype(np.float32)
    return jnp.asarray(gblk, BF16), jnp.asarray(np.concatenate([gexp, gexp], axis=0), BF16)


def _tile(m, want):
    return want if m % want == 0 else m


def kernel(x_prompt, x_sample, state_hgrn, cache_kv_w128, cache_kv_w512, cache_kv_w2048, hg_lb_logits, hg_w_q, hg_w_f, hg_w_i, hg_w_g, hg_w_o, hg_norm_o, att_w_qkv, att_w_o, att_q_norm, att_k_norm, norm_mix, norm_ffn, ffn_w_up, ffn_w_down):
    b, t_len, d = x_prompt.shape
    bs, ts, _ = x_sample.shape
    depth = norm_mix.shape[0]
    caches_all = (cache_kv_w128, cache_kv_w512, cache_kv_w2048)
    gblk, gexp = _head_group_matrices()
    sample_bias = _sample_bias(ts)

    yp = x_prompt.reshape(b * t_len, d)
    ys = x_sample.reshape(bs * ts, d)
    tm_p = _tile(b * t_len, 512)
    tm_s = _tile(bs * ts, 256)
    hg_p, hg_s = [], []
    kv_p = [[] for _ in GROUPS]
    kv_s = [[] for _ in GROUPS]
    for layer in range(depth):
        a = layer // 2
        nw = norm_mix[layer][None]
        nf = norm_ffn[layer][None]
        wu = ffn_w_up[layer].astype(BF16)
        wd = ffn_w_down[layer].astype(BF16)
        if layer % 2 == 0:
            w4 = jnp.stack([hg_w_q[a], hg_w_i[a], hg_w_g[a], hg_w_f[a]]).astype(BF16)
            wo = hg_w_o[a].astype(BF16)
            gn = hg_norm_o[a][None]
            zeros = jnp.zeros((b,) + state_hgrn.shape[2:], F32)
            tc = _tile(t_len, 512)
            f, qig = _hgrn_proj(yp, nw, hg_lb_logits, w4, layer, tm_p, BF16)
            o, sp = _hgrn_rec(f, qig, zeros, b, t_len, tc, 128)
            yp = _hgrn_out_ffn(o, qig, gn, yp, wo, nf, wu, wd, tm_p)
            f, qig = _hgrn_proj(ys, nw, hg_lb_logits, w4, layer, tm_s, F32)
            o, ss = _hgrn_rec(f, qig, state_hgrn[a], bs, ts, ts, 16)
            ys = _hgrn_out(o, qig, gn, ys, wo, tm_s)
            hg_p.append(sp)
            hg_s.append(ss)
        else:
            wqkv = att_w_qkv[a].astype(BF16)
            wo = att_w_o[a].astype(BF16)
            qn_row = jnp.tile(att_q_norm[a], ATT_HEADS)
            kn_row = jnp.tile(att_k_norm[a], ATT_HEADS)
            qkn = jnp.stack([qn_row * (ATT_SCALE * LOG2E), kn_row])
            qkv_groups = _qkv_perm(yp, nw, wqkv, qkn, gblk, b, t_len, tm_p)
            outs, ms, ls = [], [], []
            for g, (window, dil) in enumerate(GROUPS):
                qb = max(q for q in (1, 2, 4) if (t_len // dil) % (q * N_BACK) == 0)
                rb = max(r for r in (1, 2, 4) if r * qb <= 4 and dil % r == 0)
                o, mrow, lrow = _attn_prompt(qkv_groups[g], _prompt_bias(g), rb, qb)
                outs.append(o)
                ms.append(mrow)
                ls.append(lrow)
                keep = min(window, t_len)
                wt = jnp.stack([wqkv[:, (N_GROUPS + g) * d:(N_GROUPS + g + 1) * d].T,
                                wqkv[:, (2 * N_GROUPS + g) * d:(2 * N_GROUPS + g + 1) * d].T])
                kvt = _kv_tail(yp, nw, wt, kn_row[:, None], b, t_len, keep, _tile(keep, 512))
                kvt = kvt.reshape(b, 2, ATT_HEADS, ATT_HEAD_DIM, keep)
                kv_p[g].append(jnp.transpose(kvt, (0, 4, 1, 2, 3)))
            yp = _merge_out_ffn(outs, ms, ls, gexp, yp, wo, nf, wu, wd, t_len, tm_p)
            qkv32 = _qkv_plain(ys, nw, wqkv, qkn, gblk, tm_s).reshape(bs, ts, -1)
            caches_t = [jnp.transpose(c[a], (0, 2, 3, 4, 1)) for c in caches_all]
            om = _attn_sample(qkv32, caches_t, sample_bias)
            ys = _matmul_res(om.reshape(bs * ts, d), ys, wo, tm_s)
            kv = qkv32.reshape(bs, ts, 3, N_GROUPS, ATT_HEADS, ATT_HEAD_DIM)
            for g in range(N_GROUPS):
                kv_s[g].append(kv[:, :, 1:, g])
        ys = _ffn(ys, nf, wu, wd, tm_s, 1024)
    return (yp.reshape(b, t_len, d), ys.reshape(bs, ts, d), jnp.stack(hg_p), jnp.stack(hg_s),
            jnp.stack(kv_p[0]), jnp.stack(kv_s[0]),
            jnp.stack(kv_p[1]), jnp.stack(kv_s[1]),
            jnp.stack(kv_p[2]), jnp.stack(kv_s[2]))
```

```python
import functools

import numpy as np
import jax
import jax.numpy as jnp
from jax import lax
from jax.experimental import pallas as pl
from jax.experimental.pallas import tpu as pltpu

F32 = jnp.float32
BF16 = jnp.bfloat16

RMS_EPS = 1e-6
HG_HEADS = 8
HG_DK = 128
GROUPS = ((128, 1), (512, 4), (2048, 16))
N_GROUPS = len(GROUPS)
ATT_HEADS = 16
ATT_HEAD_DIM = 64
ATT_SCALE = ATT_HEAD_DIM ** -0.5
N_BACK = 128
NEG = -1e30
LOG2E = 1.4426950408889634

LANES = 128
MXU_WIDTH = 256
ATTN_STEP_BLOCKS = 8
MIB = 1024 * 1024


def _dot(a, b):
    return jnp.dot(a, b, preferred_element_type=F32)


def _dot_nt(a, b):
    return lax.dot_general(a, b, (((1,), (1,)), ((), ())), preferred_element_type=F32)


def _dot_tn(a, b):
    return lax.dot_general(a, b, (((0,), (0,)), ((), ())), preferred_element_type=F32)


def _split3(x):
    hi = x.astype(BF16)
    r1 = x - hi.astype(F32)
    mid = r1.astype(BF16)
    lo = (r1 - mid.astype(F32)).astype(BF16)
    return hi, mid, lo


def _rms_rows(x, w):
    ms = jnp.mean(x * x, axis=-1, keepdims=True)
    return x * lax.rsqrt(ms + RMS_EPS) * w


def _params(sem, vmem_mib):
    return pltpu.CompilerParams(dimension_semantics=sem, vmem_limit_bytes=vmem_mib * MIB)


def _hgrn_proj_kernel(layer, x_ref, nw_ref, lbl_ref, w_ref, f_ref, o_ref):
    xn = _rms_rows(x_ref[...], nw_ref[...]).astype(BF16)
    lg = lbl_ref[...]
    e = jnp.exp(lg - jnp.max(lg, axis=0, keepdims=True))
    lb_row = jnp.sum(e[:layer + 1], axis=0, keepdims=True) / jnp.sum(e, axis=0, keepdims=True)
    for p in range(4):
        for c in range(0, x_ref.shape[-1], MXU_WIDTH):
            cols = slice(c, c + MXU_WIDTH)
            y = _dot(xn, w_ref[p, :, cols])
            if p == 1:
                o_ref[p, :, cols] = y.astype(o_ref.dtype)
            elif p == 3:
                lb = lb_row[:, cols]
                f_ref[:, cols] = lb + (1.0 - lb) * jax.nn.sigmoid(y)
            else:
                o_ref[p, :, cols] = (y * jax.nn.sigmoid(y)).astype(o_ref.dtype)


def _hgrn_proj(x, nw, lb_logits, w4, layer, tm, qig_dtype):
    m, d = x.shape
    return pl.pallas_call(
        functools.partial(_hgrn_proj_kernel, layer),
        out_shape=(jax.ShapeDtypeStruct((m, d), F32), jax.ShapeDtypeStruct((3, m, d), qig_dtype)),
        grid=(m // tm,),
        in_specs=[
            pl.BlockSpec((tm, d), lambda i: (i, 0)),
            pl.BlockSpec((1, d), lambda i: (0, 0)),
            pl.BlockSpec(lb_logits.shape, lambda i: (0, 0)),
            pl.BlockSpec((4, d, d), lambda i: (0, 0, 0)),
        ],
        out_specs=(pl.BlockSpec((tm, d), lambda i: (i, 0)),
                   pl.BlockSpec((3, tm, d), lambda i: (0, i, 0))),
        compiler_params=_params(("parallel",), 48),
        name="hgrn_proj",
    )(x, nw, lb_logits, w4)


def _hgrn_rec_kernel(tc, c, q_ref, f_ref, v_ref, s0_ref, o_ref, so_ref, st_ref):
    t = pl.program_id(1)
    dk = HG_DK

    @pl.when(t == 0)
    def _():
        for h in range(HG_HEADS):
            st_ref[h] = s0_ref[0, h].T

    row = lax.broadcasted_iota(jnp.int32, (c, c), 0)
    col = lax.broadcasted_iota(jnp.int32, (c, c), 1)
    tril = jnp.where(row >= col, 1.0, 0.0).astype(BF16)
    rowl = lax.broadcasted_iota(jnp.int32, (c, dk), 0)
    n0 = min(c, 16)
    same_block = {}
    n = c // 2
    while n >= n0:
        shift = n.bit_length() - 1
        same_block[n] = (row >> shift) == (col >> shift)
        n //= 2

    def block_ref_rows(ch, n, pick):
        parts = [jnp.broadcast_to(ch[j * n + pick:j * n + pick + 1, :], (n, dk)) for j in range(c // n)]
        return parts[0] if len(parts) == 1 else jnp.concatenate(parts, axis=0)

    def do_chunks(nchunk, get_q, get_f, get_v, store):
        cums = []
        for ci in range(nchunk):
            hi, mid, lo = _split3(jnp.log(get_f(ci, slice(None))))
            cums.append(_dot(tril, hi) + _dot(tril, mid) + _dot(tril, lo))

        def first_stage(ci, h):
            sl = slice(h * dk, (h + 1) * dk)
            qh, kh, vh, ch = get_q(ci, sl), 1.0 - get_f(ci, sl), get_v(ci, sl), cums[ci][:, sl]
            s_t = st_ref[h]
            o = _dot_nt((qh * jnp.exp(ch)).astype(BF16), s_t.astype(BF16))
            last = ch[c - 1:c, :]
            kd = kh * jnp.exp(last - ch)
            s_new = s_t * jnp.exp(last) + _dot_tn(vh.astype(BF16), kd.astype(BF16))
            pieces = []
            n = c
            while n > n0:
                half = n // 2
                b = block_ref_rows(ch, n, half - 1)
                upper = (rowl & (n - 1)) >= half
                e = jnp.exp(jnp.where(upper, ch - b, b - ch))
                qt = jnp.where(upper, qh * e, 0.0).astype(BF16)
                kt = jnp.where(upper, 0.0, kh * e).astype(BF16)
                pieces.append((n, _dot_nt(qt, kt)))
                n = half
            dq = ch - block_ref_rows(ch, n0, n0 // 2 - 1)
            pieces.append((n0, _dot_nt((qh * jnp.exp(dq)).astype(BF16), (kh * jnp.exp(-dq)).astype(BF16))))
            return ci, h, s_new, o, pieces

        def second_stage(ci, h, s_new, o, pieces):
            sl = slice(h * dk, (h + 1) * dk)
            vh = get_v(ci, sl)
            attn = None
            for n, a in pieces:
                if n == n0:
                    keep = col <= row
                    if n0 < c:
                        keep = keep & same_block[n0]
                    a = jnp.where(keep, a, 0.0)
                elif n < c:
                    a = jnp.where(same_block[n], a, 0.0)
                attn = a if attn is None else attn + a
            store(ci, sl, o + _dot(attn.astype(BF16), vh.astype(BF16)))
            st_ref[h] = s_new

        pending = None
        for ci in range(nchunk):
            for h in range(HG_HEADS):
                nxt = first_stage(ci, h)
                if pending is not None:
                    second_stage(*pending)
                pending = nxt
        second_stage(*pending)

    if tc >= c:
        def rows(ci):
            return slice(ci * c, (ci + 1) * c)

        def store(ci, sl, o):
            o_ref[rows(ci), sl] = o

        do_chunks(tc // c,
                  lambda ci, sl: q_ref[0, rows(ci), sl].astype(F32),
                  lambda ci, sl: f_ref[rows(ci), sl],
                  lambda ci, sl: v_ref[0, rows(ci), sl].astype(F32), store)
    else:
        pad = c - tc
        z = jnp.zeros((pad, q_ref.shape[-1]), F32)
        qp = jnp.concatenate([q_ref[0].astype(F32), z], axis=0)
        fp = jnp.concatenate([f_ref[...], z + 1.0], axis=0)
        vp = jnp.concatenate([v_ref[0].astype(F32), z], axis=0)

        def store(ci, sl, o):
            o_ref[:, sl] = o[:tc]

        do_chunks(1, lambda ci, sl: qp[:, sl], lambda ci, sl: fp[:, sl], lambda ci, sl: vp[:, sl], store)

    @pl.when(t == pl.num_programs(1) - 1)
    def _():
        for h in range(HG_HEADS):
            so_ref[0, h] = st_ref[h].T


def _hgrn_rec(f, qig, s0, b, t_len, tc, c):
    m, d = f.shape
    nt = t_len // tc
    h, dk, dv = s0.shape[1:]

    def row_spec(which):
        return pl.BlockSpec((1, tc, d), lambda bi, ti: (which, bi * nt + ti, 0))

    return pl.pallas_call(
        functools.partial(_hgrn_rec_kernel, tc, c),
        out_shape=(jax.ShapeDtypeStruct((m, d), F32), jax.ShapeDtypeStruct(s0.shape, F32)),
        grid=(b, nt),
        in_specs=[row_spec(0), pl.BlockSpec((tc, d), lambda bi, ti: (bi * nt + ti, 0)), row_spec(1),
                  pl.BlockSpec((1, h, dk, dv), lambda bi, ti: (bi, 0, 0, 0))],
        out_specs=(pl.BlockSpec((tc, d), lambda bi, ti: (bi * nt + ti, 0)),
                   pl.BlockSpec((1, h, dk, dv), lambda bi, ti: (bi, 0, 0, 0))),
        scratch_shapes=[pltpu.VMEM((h, dv, dk), F32)],
        compiler_params=_params(("parallel", "arbitrary"), 48),
        name="hgrn_rec",
    )(qig, f, qig, s0)


def _hgrn_out_kernel(o_ref, g_ref, gn_ref, x_ref, w_ref, y_ref):
    a = _rms_rows(o_ref[...], gn_ref[...]) * g_ref[0].astype(F32)
    y_ref[...] = x_ref[...] + _dot(a.astype(BF16), w_ref[...])


def _hgrn_out(o, qig, gn, x, w, tm):
    m, d = x.shape
    return pl.pallas_call(
        _hgrn_out_kernel,
        out_shape=jax.ShapeDtypeStruct((m, d), F32),
        grid=(m // tm,),
        in_specs=[
            pl.BlockSpec((tm, d), lambda i: (i, 0)),
            pl.BlockSpec((1, tm, d), lambda i: (2, i, 0)),
            pl.BlockSpec((1, d), lambda i: (0, 0)),
            pl.BlockSpec((tm, d), lambda i: (i, 0)),
            pl.BlockSpec((d, d), lambda i: (0, 0)),
        ],
        out_specs=pl.BlockSpec((tm, d), lambda i: (i, 0)),
        compiler_params=_params(("parallel",), 40),
        name="hgrn_out",
    )(o, qig, gn, x, w)


def _ffn_kernel(x_ref, nw_ref, wu_ref, wd_ref, y_ref, xn_ref):
    j = pl.program_id(1)

    @pl.when(j == 0)
    def _():
        x = x_ref[...]
        xn_ref[...] = _rms_rows(x, nw_ref[...]).astype(BF16)
        y_ref[...] = x

    h = jnp.maximum(_dot(xn_ref[...], wu_ref[...]), 0.0)
    y_ref[...] += _dot((h * h).astype(BF16), wd_ref[...])


def _ffn(x, nw, wu, wd, tm, tf):
    m, d = x.shape
    ff = wu.shape[1]
    return pl.pallas_call(
        _ffn_kernel,
        out_shape=jax.ShapeDtypeStruct((m, d), F32),
        grid=(m // tm, ff // tf),
        in_specs=[
            pl.BlockSpec((tm, d), lambda i, j: (i, 0)),
            pl.BlockSpec((1, d), lambda i, j: (0, 0)),
            pl.BlockSpec((d, tf), lambda i, j: (0, j)),
            pl.BlockSpec((tf, d), lambda i, j: (j, 0)),
        ],
        out_specs=pl.BlockSpec((tm, d), lambda i, j: (i, 0)),
        scratch_shapes=[pltpu.VMEM((tm, d), BF16)],
        compiler_params=_params(("parallel", "arbitrary"), 48),
        name="ffn",
    )(x, nw, wu, wd)


def _ffn_rows(x, nw_ref, wu_ref, wd_ref, y_ref, h_ref):
    xn = _rms_rows(x, nw_ref[...]).astype(BF16)
    for c in range(0, wu_ref.shape[-1], MXU_WIDTH):
        h = jnp.maximum(_dot(xn, wu_ref[:, c:c + MXU_WIDTH]), 0.0)
        h_ref[:, c:c + MXU_WIDTH] = (h * h).astype(BF16)
    hb = h_ref[...]
    for c in range(0, wd_ref.shape[-1], MXU_WIDTH):
        y_ref[:, c:c + MXU_WIDTH] = x[:, c:c + MXU_WIDTH] + _dot(hb, wd_ref[:, c:c + MXU_WIDTH])


def _hgrn_out_ffn_kernel(o_ref, g_ref, gn_ref, x_ref, wo_ref, nw_ref, wu_ref, wd_ref, y_ref, h_ref):
    a = _rms_rows(o_ref[...], gn_ref[...]) * g_ref[0].astype(F32)
    x = x_ref[...] + _dot(a.astype(BF16), wo_ref[...])
    _ffn_rows(x, nw_ref, wu_ref, wd_ref, y_ref, h_ref)


def _resident(shape):
    return pl.BlockSpec(shape, lambda i: (0,) * len(shape), pipeline_mode=pl.Buffered(1))


def _hgrn_out_ffn(o, qig, gn, x, wo, nw, wu, wd, tm):
    m, d = x.shape
    row = pl.BlockSpec((tm, d), lambda i: (i, 0))
    vec = pl.BlockSpec((1, d), lambda i: (0, 0))
    return pl.pallas_call(
        _hgrn_out_ffn_kernel,
        out_shape=jax.ShapeDtypeStruct((m, d), F32),
        grid=(m // tm,),
        in_specs=[row, pl.BlockSpec((1, tm, d), lambda i: (2, i, 0)), vec, row, _resident(wo.shape),
                  vec, _resident(wu.shape), _resident(wd.shape)],
        out_specs=row,
        scratch_shapes=[pltpu.VMEM((tm, wu.shape[-1]), BF16)],
        compiler_params=_params(("parallel",), 56),
        name="hgrn_out_ffn",
    )(o, qig, gn, x, wo, nw, wu, wd)


def _qk_norm_rows(y, qkn, gblk):
    y2 = (y * y).astype(BF16)
    w = gblk.shape[0]
    ss = jnp.concatenate([_dot(y2[:, c * w:(c + 1) * w], gblk) for c in range(y.shape[-1] // w)], axis=1)
    return y * lax.rsqrt(ss * (1.0 / ATT_HEAD_DIM) + RMS_EPS) * qkn


def _qkv_perm_kernel(tm, x_ref, nw_ref, w_ref, qkn_ref, gblk_ref, o0_ref, o1_ref, o2_ref, xs_ref, xp_ref):
    outs = (o0_ref, o1_ref, o2_ref)
    d = x_ref.shape[-1]
    nslab = d // LANES
    xn = _rms_rows(x_ref[...], nw_ref[...])
    xp_ref[0] = xn.astype(BF16)
    for c in range(nslab):
        xs_ref[c] = xn[:, c * LANES:(c + 1) * LANES]
    for g in range(1, N_GROUPS):
        dil = GROUPS[g][1]
        n = tm // dil
        for c in range(nslab):
            for r in range(dil):
                xp_ref[g, r * n:(r + 1) * n, c * LANES:(c + 1) * LANES] = (
                    xs_ref.at[c][pl.ds(r, n, stride=dil), :].astype(BF16))
    gblk = gblk_ref[...]

    def finish(g, which, c, y):
        dil = GROUPS[g][1]
        if which < 2:
            ss = _dot((y * y).astype(BF16), gblk)
            y = y * lax.rsqrt(ss * (1.0 / ATT_HEAD_DIM) + RMS_EPS) * qkn_ref[which:which + 1, c:c + MXU_WIDTH]
        outs[g][which, 0, :, :, c:c + MXU_WIDTH] = y.astype(BF16).reshape(dil, tm // dil, MXU_WIDTH)

    pending = None
    for g in range(N_GROUPS):
        xg = xp_ref[g]
        for which in range(3):
            base = (which * N_GROUPS + g) * d
            for c in range(0, d, MXU_WIDTH):
                y = _dot(xg, w_ref[:, base + c:base + c + MXU_WIDTH])
                if pending is not None:
                    finish(*pending)
                pending = (g, which, c, y)
    finish(*pending)


def _qkv_perm(x, nw, w, qkn, gblk, b, t_len, tm):
    m, d = x.shape
    nt = t_len // tm
    out_shapes, out_specs = [], []
    for _, dil in GROUPS:
        out_shapes.append(jax.ShapeDtypeStruct((3, b, dil, t_len // dil, d), BF16))
        out_specs.append(pl.BlockSpec((3, 1, dil, tm // dil, d), lambda i: (0, i // nt, 0, i % nt, 0)))
    return pl.pallas_call(
        functools.partial(_qkv_perm_kernel, tm),
        out_shape=tuple(out_shapes),
        grid=(m // tm,),
        in_specs=[
            pl.BlockSpec((tm, d), lambda i: (i, 0)),
            pl.BlockSpec((1, d), lambda i: (0, 0)),
            pl.BlockSpec(w.shape, lambda i: (0, 0), pipeline_mode=pl.Buffered(1)),
            pl.BlockSpec(qkn.shape, lambda i: (0, 0)),
            pl.BlockSpec(gblk.shape, lambda i: (0, 0)),
        ],
        out_specs=tuple(out_specs),
        scratch_shapes=[pltpu.VMEM((d // LANES, tm, LANES), F32),
                        pltpu.VMEM((N_GROUPS, tm, d), BF16)],
        compiler_params=_params(("parallel",), 56),
        name="qkv_perm",
    )(x, nw, w, qkn, gblk)


def _qkv_plain_kernel(x_ref, nw_ref, w_ref, qkn_ref, gblk_ref, o_ref, xn_ref):
    j = pl.program_id(1)

    @pl.when(j == 0)
    def _():
        xn_ref[...] = _rms_rows(x_ref[...], nw_ref[...]).astype(BF16)

    y = _dot(xn_ref[...], w_ref[...])

    @pl.when(j < 2 * N_GROUPS)
    def _():
        gain = jnp.where(j < N_GROUPS, qkn_ref[0:1, :], qkn_ref[1:2, :])
        o_ref[...] = _qk_norm_rows(y, gain, gblk_ref[...])

    @pl.when(j >= 2 * N_GROUPS)
    def _():
        o_ref[...] = y


def _qkv_plain(x, nw, w, qkn, gblk, tm):
    m, d = x.shape
    n = w.shape[1]
    return pl.pallas_call(
        _qkv_plain_kernel,
        out_shape=jax.ShapeDtypeStruct((m, n), F32),
        grid=(m // tm, n // d),
        in_specs=[
            pl.BlockSpec((tm, d), lambda i, j: (i, 0)),
            pl.BlockSpec((1, d), lambda i, j: (0, 0)),
            pl.BlockSpec((d, d), lambda i, j: (0, j)),
            pl.BlockSpec(qkn.shape, lambda i, j: (0, 0)),
            pl.BlockSpec(gblk.shape, lambda i, j: (0, 0)),
        ],
        out_specs=pl.BlockSpec((tm, d), lambda i, j: (i, j)),
        scratch_shapes=[pltpu.VMEM((tm, d), BF16)],
        compiler_params=_params(("parallel", "arbitrary"), 40),
        name="qkv_plain",
    )(x, nw, w, qkn, gblk)


def _kv_tail_kernel(x_ref, nw_ref, wt_ref, kn_ref, o_ref, xn_ref):
    xn_ref[...] = _rms_rows(x_ref[...], nw_ref[...]).astype(BF16)
    tt = x_ref.shape[0]
    slab = min(tt, MXU_WIDTH)
    for kv in range(2):
        for c in range(0, tt, slab):
            yt = _dot_nt(wt_ref[kv], xn_ref[c:c + slab, :])
            if kv == 0:
                y3 = yt.reshape(ATT_HEADS, ATT_HEAD_DIM, slab)
                ms = jnp.mean(y3 * y3, axis=1, keepdims=True)
                yt = (y3 * lax.rsqrt(ms + RMS_EPS)).reshape(yt.shape) * kn_ref[...]
            o_ref[0, kv, :, c:c + slab] = yt


def _kv_tail(x, nw, wt, kn_col, b, t_len, keep, tt):
    m, d = x.shape
    first = (t_len - keep) // tt
    per_seq = t_len // tt
    return pl.pallas_call(
        _kv_tail_kernel,
        out_shape=jax.ShapeDtypeStruct((b, 2, d, keep), F32),
        grid=(b, keep // tt),
        in_specs=[
            pl.BlockSpec((tt, d), lambda bi, ti: (bi * per_seq + first + ti, 0)),
            pl.BlockSpec((1, d), lambda bi, ti: (0, 0)),
            pl.BlockSpec((2, d, d), lambda bi, ti: (0, 0, 0)),
            pl.BlockSpec((d, 1), lambda bi, ti: (0, 0)),
        ],
        out_specs=pl.BlockSpec((1, 2, d, tt), lambda bi, ti: (bi, 0, 0, ti)),
        scratch_shapes=[pltpu.VMEM((tt, d), BF16)],
        compiler_params=_params(("parallel", "parallel"), 40),
        name="kv_tail",
    )(x, nw, wt, kn_col)


def _attn_prompt_kernel(rb, qb, q_ref, kc_ref, kp_ref, vc_ref, vp_ref, bias_ref, o_ref, m_ref, l_ref):
    i = pl.program_id(2)
    first = jnp.where(i == 0, 0, 1)
    lane = lax.broadcasted_iota(jnp.int32, (N_BACK, LANES), 1)
    lo_half = lane < ATT_HEAD_DIM
    npair = ATT_HEADS // 2
    stats = {}

    def scores(r, u, pr):
        rows = slice(u * N_BACK, (u + 1) * N_BACK)
        sl = slice(pr * LANES, (pr + 1) * LANES)
        qp = q_ref[0, 0, r, rows, sl]
        kprev = kp_ref[0, 0, r, :, sl] if u == 0 else kc_ref[0, 0, r, (u - 1) * N_BACK:u * N_BACK, sl]
        kk = jnp.concatenate([kprev, kc_ref[0, 0, r, rows, sl]], axis=0)
        zq = jnp.zeros_like(qp)
        q2 = jnp.concatenate([jnp.where(lo_half, qp, zq), jnp.where(lo_half, zq, qp)], axis=0)
        return _dot_nt(q2, kk)

    def finish(r, u, pr, s):
        rows = slice(u * N_BACK, (u + 1) * N_BACK)
        sl = slice(pr * LANES, (pr + 1) * LANES)
        vprev = vp_ref[0, 0, r, :, sl] if u == 0 else vc_ref[0, 0, r, (u - 1) * N_BACK:u * N_BACK, sl]
        vv = jnp.concatenate([vprev, vc_ref[0, 0, r, rows, sl]], axis=0)
        s = s + (bias_ref[first, pr] if u == 0 else bias_ref[1, pr])
        mx = jnp.max(s, axis=-1, keepdims=True)
        p = jnp.exp2(s - mx)
        l = jnp.sum(p, axis=-1, keepdims=True)
        o2 = _dot(p.astype(BF16), vv)
        o_ref[0, r, rows, sl] = jnp.where(lo_half, o2[:N_BACK], o2[N_BACK:]).astype(BF16)
        m_acc, l_acc = stats.get((r, u), (jnp.zeros((N_BACK, LANES), F32), jnp.ones((N_BACK, LANES), F32)))
        m_acc = jnp.where(lane == 2 * pr, mx[:N_BACK], jnp.where(lane == 2 * pr + 1, mx[N_BACK:], m_acc))
        l_acc = jnp.where(lane == 2 * pr, l[:N_BACK], jnp.where(lane == 2 * pr + 1, l[N_BACK:], l_acc))
        stats[(r, u)] = (m_acc, l_acc)
        if pr == npair - 1:
            m_ref[0, r, rows, :] = m_acc
            l_ref[0, r, rows, :] = l_acc

    pending = None
    for r in range(rb):
        for u in range(qb):
            for pr in range(npair):
                s = scores(r, u, pr)
                if pending is not None:
                    finish(*pending)
                pending = (r, u, pr, s)
    finish(*pending)


def _attn_prompt(qkv_g, bias, rb, qb):
    _, b, dil, l, d = qkv_g.shape
    rows = qb * N_BACK

    def cur(which):
        return pl.BlockSpec((1, 1, rb, rows, d), lambda bi, r, i: (which, bi, r, i, 0))

    def prev(which):
        return pl.BlockSpec((1, 1, rb, N_BACK, d),
                            lambda bi, r, i: (which, bi, r, jnp.maximum(i * qb - 1, 0), 0))

    return pl.pallas_call(
        functools.partial(_attn_prompt_kernel, rb, qb),
        out_shape=(jax.ShapeDtypeStruct((b, dil, l, d), BF16),
                   jax.ShapeDtypeStruct((b, dil, l, LANES), F32),
                   jax.ShapeDtypeStruct((b, dil, l, LANES), F32)),
        grid=(b, dil // rb, l // rows),
        in_specs=[cur(0), cur(1), prev(1), cur(2), prev(2),
                  pl.BlockSpec(bias.shape, lambda bi, r, i: (0, 0, 0, 0))],
        out_specs=(pl.BlockSpec((1, rb, rows, d), lambda bi, r, i: (bi, r, i, 0)),
                   pl.BlockSpec((1, rb, rows, LANES), lambda bi, r, i: (bi, r, i, 0)),
                   pl.BlockSpec((1, rb, rows, LANES), lambda bi, r, i: (bi, r, i, 0))),
        compiler_params=_params(("parallel", "parallel", "arbitrary"), 40),
        name="attn_prompt_d%d" % dil,
    )(qkv_g, qkv_g, qkv_g, qkv_g, qkv_g, bias)


def _merge_out_ffn_kernel(tm, o0_ref, o1_ref, o2_ref, m0_ref, m1_ref, m2_ref, l0_ref, l1_ref, l2_ref,
                          gexp_ref, x_ref, w_ref, nw_ref, wu_ref, wd_ref, y_ref,
                          os_ref, ms_ref, ls_ref, h_ref):
    d = x_ref.shape[-1]
    nslab = d // LANES

    def natural(o_ref, m_ref, l_ref, dil):
        if dil == 1:
            return o_ref[0, 0].astype(F32), m_ref[0, 0], l_ref[0, 0]
        n = tm // dil
        for r in range(dil):
            ms_ref[pl.ds(r, n, stride=dil), :] = m_ref[0, r]
            ls_ref[pl.ds(r, n, stride=dil), :] = l_ref[0, r]
            for c in range(nslab):
                os_ref.at[c][pl.ds(r, n, stride=dil), :] = o_ref[0, r, :, c * LANES:(c + 1) * LANES].astype(F32)
        return jnp.concatenate([os_ref[c] for c in range(nslab)], axis=1), ms_ref[...], ls_ref[...]

    trip = [natural(o_ref, m_ref, l_ref, GROUPS[g][1]) for g, (o_ref, m_ref, l_ref) in enumerate(
        ((o0_ref, m0_ref, l0_ref), (o1_ref, m1_ref, l1_ref), (o2_ref, m2_ref, l2_ref)))]
    mx = jnp.maximum(jnp.maximum(trip[0][1], trip[1][1]), trip[2][1])
    es = [jnp.exp2(t[1] - mx) for t in trip]
    inv = 1.0 / (es[0] * trip[0][2] + es[1] * trip[1][2] + es[2] * trip[2][2])
    acc = None
    for e, (o, _, _) in zip(es, trip):
        wt = e * inv
        hi = wt.astype(BF16)
        lo = (wt - hi.astype(F32)).astype(BF16)
        wx = _dot(jnp.concatenate([hi, lo], axis=1), gexp_ref[...])
        acc = wx * o if acc is None else acc + wx * o
    x = x_ref[...] + _dot(acc.astype(BF16), w_ref[...])
    _ffn_rows(x, nw_ref, wu_ref, wd_ref, y_ref, h_ref)


def _merge_out_ffn(outs, ms, ls, gexp, x, w, nw, wu, wd, t_len, tm):
    m, d = x.shape
    nt = t_len // tm
    row = pl.BlockSpec((tm, d), lambda i: (i, 0))
    o_specs, s_specs = [], []
    for _, dil in GROUPS:
        o_specs.append(pl.BlockSpec((1, dil, tm // dil, d), lambda i: (i // nt, 0, i % nt, 0)))
        s_specs.append(pl.BlockSpec((1, dil, tm // dil, LANES), lambda i: (i // nt, 0, i % nt, 0)))
    return pl.pallas_call(
        functools.partial(_merge_out_ffn_kernel, tm),
        out_shape=jax.ShapeDtypeStruct((m, d), F32),
        grid=(m // tm,),
        in_specs=o_specs + s_specs + s_specs + [_resident(gexp.shape), row, _resident(w.shape),
                                                pl.BlockSpec((1, d), lambda i: (0, 0)),
                                                _resident(wu.shape), _resident(wd.shape)],
        out_specs=row,
        scratch_shapes=[pltpu.VMEM((d // LANES, tm, LANES), F32), pltpu.VMEM((tm, LANES), F32),
                        pltpu.VMEM((tm, LANES), F32), pltpu.VMEM((tm, wu.shape[-1]), BF16)],
        compiler_params=_params(("parallel",), 56),
        name="merge_out_ffn",
    )(*outs, *ms, *ls, gexp, x, w, nw, wu, wd)


SAMPLE_PAIRS = 4


def _attn_sample_kernel(ts, *refs):
    ng = N_GROUPS
    q_refs, k_refs, v_refs = refs[0:ng], refs[ng:2 * ng], refs[2 * ng:3 * ng]
    c_refs = refs[3 * ng:4 * ng]
    bias_ref, o_ref = refs[4 * ng], refs[4 * ng + 1]
    npair = SAMPLE_PAIRS
    lane = lax.broadcasted_iota(jnp.int32, (ts, LANES), 1)
    lo_half = lane < ATT_HEAD_DIM
    zpad = jnp.zeros((LANES - npair * ts, LANES), F32)

    tk, vs = [], []
    for g in range(ng):
        kst = jnp.concatenate([k_refs[g][0, :, p * LANES:(p + 1) * LANES] for p in range(npair)] + [zpad], axis=0)
        vst = jnp.concatenate([v_refs[g][0, :, p * LANES:(p + 1) * LANES] for p in range(npair)] + [zpad], axis=0)
        tk.append(kst.T.astype(BF16))
        vs.append(vst.astype(BF16))

    rows = []
    for p in range(npair):
        sl = slice(p * LANES, (p + 1) * LANES)
        old, new = [], []
        for g in range(ng):
            qp = q_refs[g][0, :, sl]
            q2 = jnp.concatenate([jnp.where(lo_half, qp, 0.0), jnp.where(lo_half, 0.0, qp)], axis=0).astype(BF16)
            kt = c_refs[g][0, 0, 2 * p:2 * p + 2].reshape(LANES, c_refs[g].shape[-1]).astype(BF16)
            old.append(_dot(q2, kt))
            new.append(_dot(q2, tk[g]))
        rows.append(jnp.concatenate(old + new, axis=1))
    s = jnp.concatenate(rows, axis=0) + bias_ref[...].reshape(npair * 2 * ts, -1)
    mx = jnp.max(s, axis=-1, keepdims=True)
    pr = jnp.exp2(s - mx)
    inv = 1.0 / jnp.sum(pr, axis=-1, keepdims=True)
    prb = pr.astype(BF16)

    for p in range(npair):
        sl = slice(p * LANES, (p + 1) * LANES)
        r0 = p * 2 * ts
        acc = jnp.zeros((2 * ts, LANES), F32)
        off = 0
        for g in range(ng):
            window = c_refs[g].shape[-1]
            vt = c_refs[g][0, 1, 2 * p:2 * p + 2].reshape(LANES, window).astype(BF16)
            acc = acc + _dot_nt(prb[r0:r0 + 2 * ts, off:off + window], vt)
            off += window
        for g in range(ng):
            acc = acc + _dot(prb[r0:r0 + 2 * ts, off:off + LANES], vs[g])
            off += LANES
        om = acc * inv[r0:r0 + 2 * ts]
        o_ref[0, :, sl] = jnp.where(lo_half, om[:ts], om[ts:])


def _attn_sample(qkv32, caches_t, bias):
    b, ts, _ = qkv32.shape
    d = ATT_HEADS * ATT_HEAD_DIM
    width = SAMPLE_PAIRS * LANES
    per_blk = d // width
    hb = 2 * SAMPLE_PAIRS

    def new_rows(which, g):
        return pl.BlockSpec((1, ts, width), lambda bi, hq: (bi, 0, (which * N_GROUPS + g) * per_blk + hq))

    in_specs = [new_rows(w, g) for w in range(3) for g in range(N_GROUPS)]
    for c in caches_t:
        in_specs.append(pl.BlockSpec((1, 2, hb, ATT_HEAD_DIM, c.shape[-1]), lambda bi, hq: (bi, 0, hq, 0, 0)))
    in_specs.append(pl.BlockSpec((SAMPLE_PAIRS,) + bias.shape[1:], lambda bi, hq: (hq, 0, 0)))
    return pl.pallas_call(
        functools.partial(_attn_sample_kernel, ts),
        out_shape=jax.ShapeDtypeStruct((b, ts, d), F32),
        grid=(b, per_blk),
        in_specs=in_specs,
        out_specs=pl.BlockSpec((1, ts, width), lambda bi, hq: (bi, 0, hq)),
        compiler_params=_params(("parallel", "arbitrary"), 56),
        name="attn_sample",
    )(*([qkv32] * (3 * N_GROUPS)), *caches_t, bias)


def _matmul_res_kernel(a_ref, x_ref, w_ref, y_ref):
    y_ref[...] = x_ref[...] + _dot(a_ref[...].astype(BF16), w_ref[...])


def _matmul_res(a, x, w, tm):
    m, d = x.shape
    row = pl.BlockSpec((tm, d), lambda i: (i, 0))
    return pl.pallas_call(
        _matmul_res_kernel,
        out_shape=jax.ShapeDtypeStruct((m, d), F32),
        grid=(m // tm,),
        in_specs=[row, row, pl.BlockSpec((d, d), lambda i: (0, 0))],
        out_specs=row,
        compiler_params=_params(("parallel",), 40),
        name="matmul_res",
    )(a, x, w)


def _alibi_slopes():
    n = N_GROUPS * ATT_HEADS
    return LOG2E * (2.0 ** (-8.0 * np.arange(1, n + 1) / n)).reshape(N_GROUPS, ATT_HEADS)


def _prompt_bias(g):
    slopes = _alibi_slopes()[g] * GROUPS[g][1]
    dist = np.arange(N_BACK)[:, None] - np.arange(2 * N_BACK)[None, :] + N_BACK
    band = (dist >= 0) & (dist <= N_BACK)
    has_prev = np.arange(2 * N_BACK)[None, :] >= N_BACK
    out = np.empty((2, ATT_HEADS, N_BACK, 2 * N_BACK), np.float32)
    for k, valid in enumerate((band & has_prev, band)):
        out[k] = np.where(valid[None], -slopes[:, None, None] * dist[None], NEG)
    return jnp.asarray(out.reshape(2, ATT_HEADS // 2, 2 * N_BACK, 2 * N_BACK))


def _sample_bias(ts):
    slopes = _alibi_slopes()
    npairs = ATT_HEADS // 2
    old, new = [], []
    for g, (window, dil) in enumerate(GROUPS):
        pos = np.arange(window)
        bc = np.full((npairs, 2 * ts, window), NEG, np.float32)
        bn = np.full((npairs, 2 * ts, LANES), NEG, np.float32)
        for h in range(ATT_HEADS):
            for s in range(ts):
                row = (h % 2) * ts + s
                back = window + s - pos
                valid = (back % dil == 0) & (back <= window)
                bc[h // 2, row] = np.where(valid, -slopes[g, h] * back, NEG)
                lane0 = ((h // 2) % SAMPLE_PAIRS) * ts
                for s2 in range(s + 1):
                    if (s - s2) % dil == 0:
                        bn[h // 2, row, lane0 + s2] = -slopes[g, h] * (s - s2)
        old.append(bc)
        new.append(bn)
    return jnp.asarray(np.concatenate(old + new, axis=-1))


def _head_group_matrices():
    d = ATT_HEADS * ATT_HEAD_DIM
    i = np.arange(2 * LANES)
    gblk = (i[:, None] // ATT_HEAD_DIM == i[None, :] // ATT_HEAD_DIM).astype(np.float32)
    gexp = (np.arange(LANES)[:, None] == np.arange(d)[None, :] // ATT_HEAD_DIM).astype(np.float32)
    return jnp.asarray(gblk, BF16), jnp.asarray(np.concatenate([gexp, gexp], axis=0), BF16)


def _tile(m, want):
    return want if m % want == 0 else m


def kernel(x_prompt, x_sample, state_hgrn, cache_kv_w128, cache_kv_w512, cache_kv_w2048, hg_lb_logits, hg_w_q, hg_w_f, hg_w_i, hg_w_g, hg_w_o, hg_norm_o, att_w_qkv, att_w_o, att_q_norm, att_k_norm, norm_mix, norm_ffn, ffn_w_up, ffn_w_down):
    b, t_len, d = x_prompt.shape
    bs, ts, _ = x_sample.shape
    depth = norm_mix.shape[0]
    caches_all = (cache_kv_w128, cache_kv_w512, cache_kv_w2048)
    gblk, gexp = _head_group_matrices()
    sample_bias = _sample_bias(ts)

    yp = x_prompt.reshape(b * t_len, d)
    ys = x_sample.reshape(bs * ts, d)
    tm_p = _tile(b * t_len, 512)
    tm_s = _tile(bs * ts, 256)
    hg_p, hg_s = [], []
    kv_p = [[] for _ in GROUPS]
    kv_s = [[] for _ in GROUPS]
    for layer in range(depth):
        a = layer // 2
        nw = norm_mix[layer][None]
        nf = norm_ffn[layer][None]
        wu = ffn_w_up[layer].astype(BF16)
        wd = ffn_w_down[layer].astype(BF16)
        if layer % 2 == 0:
            w4 = jnp.stack([hg_w_q[a], hg_w_i[a], hg_w_g[a], hg_w_f[a]]).astype(BF16)
            wo = hg_w_o[a].astype(BF16)
            gn = hg_norm_o[a][None]
            zeros = jnp.zeros((b,) + state_hgrn.shape[2:], F32)
            tc = _tile(t_len, 512)
            f, qig = _hgrn_proj(yp, nw, hg_lb_logits, w4, layer, tm_p, BF16)
            o, sp = _hgrn_rec(f, qig, zeros, b, t_len, tc, 128)
            yp = _hgrn_out_ffn(o, qig, gn, yp, wo, nf, wu, wd, tm_p)
            f, qig = _hgrn_proj(ys, nw, hg_lb_logits, w4, layer, tm_s, F32)
            o, ss = _hgrn_rec(f, qig, state_hgrn[a], bs, ts, ts, 16)
            ys = _hgrn_out(o, qig, gn, ys, wo, tm_s)
            hg_p.append(sp)
            hg_s.append(ss)
        else:
            wqkv = att_w_qkv[a].astype(BF16)
            wo = att_w_o[a].astype(BF16)
            qn_row = jnp.tile(att_q_norm[a], ATT_HEADS)
            kn_row = jnp.tile(att_k_norm[a], ATT_HEADS)
            qkn = jnp.stack([qn_row * (ATT_SCALE * LOG2E), kn_row])
            qkv_groups = _qkv_perm(yp, nw, wqkv, qkn, gblk, b, t_len, tm_p)
            outs, ms, ls = [], [], []
            for g, (window, dil) in enumerate(GROUPS):
                qb = max(q for q in (1, 2, 4, 8) if q <= ATTN_STEP_BLOCKS and (t_len // dil) % (q * N_BACK) == 0)
                rb = max(r for r in (1, 2, 4, 8) if r * qb <= ATTN_STEP_BLOCKS and dil % r == 0)
                o, mrow, lrow = _attn_prompt(qkv_groups[g], _prompt_bias(g), rb, qb)
                outs.append(o)
                ms.append(mrow)
                ls.append(lrow)
                keep = min(window, t_len)
                wt = jnp.stack([wqkv[:, (N_GROUPS + g) * d:(N_GROUPS + g + 1) * d].T,
                                wqkv[:, (2 * N_GROUPS + g) * d:(2 * N_GROUPS + g + 1) * d].T])
                kvt = _kv_tail(yp, nw, wt, kn_row[:, None], b, t_len, keep, _tile(keep, 512))
                kvt = kvt.reshape(b, 2, ATT_HEADS, ATT_HEAD_DIM, keep)
                kv_p[g].append(jnp.transpose(kvt, (0, 4, 1, 2, 3)))
            yp = _merge_out_ffn(outs, ms, ls, gexp, yp, wo, nf, wu, wd, t_len, tm_p)
            qkv32 = _qkv_plain(ys, nw, wqkv, qkn, gblk, tm_s).reshape(bs, ts, -1)
            caches_t = [jnp.transpose(c[a], (0, 2, 3, 4, 1)) for c in caches_all]
            om = _attn_sample(qkv32, caches_t, sample_bias)
            ys = _matmul_res(om.reshape(bs * ts, d), ys, wo, tm_s)
            kv = qkv32.reshape(bs, ts, 3, N_GROUPS, ATT_HEADS, ATT_HEAD_DIM)
            for g in range(N_GROUPS):
                kv_s[g].append(kv[:, :, 1:, g])
        ys = _ffn(ys, nf, wu, wd, tm_s, 1024)
    return (yp.reshape(b, t_len, d), ys.reshape(bs, ts, d), jnp.stack(hg_p), jnp.stack(hg_s),
            jnp.stack(kv_p[0]), jnp.stack(kv_s[0]),
            jnp.stack(kv_p[1]), jnp.stack(kv_s[1]),
            jnp.stack(kv_p[2]), jnp.stack(kv_s[2]))
```

```python
import functools

import numpy as np
import jax
import jax.numpy as jnp
from jax import lax
from jax.experimental import pallas as pl
from jax.experimental.pallas import tpu as pltpu

F32 = jnp.float32
BF16 = jnp.bfloat16

RMS_EPS = 1e-6
HG_HEADS = 8
HG_DK = 128
GROUPS = ((128, 1), (512, 4), (2048, 16))
N_GROUPS = len(GROUPS)
ATT_HEADS = 16
ATT_HEAD_DIM = 64
ATT_SCALE = ATT_HEAD_DIM ** -0.5
N_BACK = 128
NEG = -1e30
LOG2E = 1.4426950408889634

LANES = 128
MXU_WIDTH = 256
ATTN_STEP_BLOCKS = 8
ROW_TILE = 512
SAMPLE_ROW_TILE = 256
HG_CHUNK = 128
HG_SUBCHUNK = 16
VMEM_SMALL, VMEM_MID, VMEM_BIG = 40, 48, 56
MIB = 1024 * 1024


def _dot(a, b):
    return jnp.dot(a, b, preferred_element_type=F32)


def _dot_nt(a, b):
    return lax.dot_general(a, b, (((1,), (1,)), ((), ())), preferred_element_type=F32)


def _dot_tn(a, b):
    return lax.dot_general(a, b, (((0,), (0,)), ((), ())), preferred_element_type=F32)


def _split3(x):
    hi = x.astype(BF16)
    r1 = x - hi.astype(F32)
    mid = r1.astype(BF16)
    lo = (r1 - mid.astype(F32)).astype(BF16)
    return hi, mid, lo


def _rms_rows(x, w):
    ms = jnp.mean(x * x, axis=-1, keepdims=True)
    return x * lax.rsqrt(ms + RMS_EPS) * w


def _params(sem, vmem_mib):
    return pltpu.CompilerParams(dimension_semantics=sem, vmem_limit_bytes=vmem_mib * MIB)


def _hgrn_proj_kernel(layer, x_ref, nw_ref, lbl_ref, w_ref, f_ref, o_ref):
    xn = _rms_rows(x_ref[...], nw_ref[...]).astype(BF16)
    lg = lbl_ref[...]
    e = jnp.exp(lg - jnp.max(lg, axis=0, keepdims=True))
    lb_row = jnp.sum(e[:layer + 1], axis=0, keepdims=True) / jnp.sum(e, axis=0, keepdims=True)
    for p in range(4):
        for c in range(0, x_ref.shape[-1], MXU_WIDTH):
            cols = slice(c, c + MXU_WIDTH)
            y = _dot(xn, w_ref[p, :, cols])
            if p == 1:
                o_ref[p, :, cols] = y.astype(o_ref.dtype)
            elif p == 3:
                lb = lb_row[:, cols]
                f_ref[:, cols] = lb + (1.0 - lb) * jax.nn.sigmoid(y)
            else:
                o_ref[p, :, cols] = (y * jax.nn.sigmoid(y)).astype(o_ref.dtype)


def _hgrn_proj(x, nw, lb_logits, w4, layer, tm, qig_dtype):
    m, d = x.shape
    return pl.pallas_call(
        functools.partial(_hgrn_proj_kernel, layer),
        out_shape=(jax.ShapeDtypeStruct((m, d), F32), jax.ShapeDtypeStruct((3, m, d), qig_dtype)),
        grid=(m // tm,),
        in_specs=[
            pl.BlockSpec((tm, d), lambda i: (i, 0)),
            pl.BlockSpec((1, d), lambda i: (0, 0)),
            pl.BlockSpec(lb_logits.shape, lambda i: (0, 0)),
            pl.BlockSpec((4, d, d), lambda i: (0, 0, 0)),
        ],
        out_specs=(pl.BlockSpec((tm, d), lambda i: (i, 0)),
                   pl.BlockSpec((3, tm, d), lambda i: (0, i, 0))),
        compiler_params=_params(("parallel",), VMEM_MID),
        name="hgrn_proj",
    )(x, nw, lb_logits, w4)


def _rec_chunks(c, nchunk, get_q, get_f, get_v, store, st_ref):
    dk = HG_DK
    row = lax.broadcasted_iota(jnp.int32, (c, c), 0)
    col = lax.broadcasted_iota(jnp.int32, (c, c), 1)
    tril = jnp.where(row >= col, 1.0, 0.0).astype(BF16)
    rowl = lax.broadcasted_iota(jnp.int32, (c, dk), 0)
    n0 = min(c, HG_SUBCHUNK)
    same_block = {}
    n = c // 2
    while n >= n0:
        shift = n.bit_length() - 1
        same_block[n] = (row >> shift) == (col >> shift)
        n //= 2

    def block_ref_rows(ch, n, pick):
        parts = [jnp.broadcast_to(ch[j * n + pick:j * n + pick + 1, :], (n, dk)) for j in range(c // n)]
        return parts[0] if len(parts) == 1 else jnp.concatenate(parts, axis=0)

    def do_chunks(nchunk, get_q, get_f, get_v, store):
        cums = []
        for ci in range(nchunk):
            hi, mid, lo = _split3(jnp.log(get_f(ci, slice(None))))
            cums.append(_dot(tril, hi) + _dot(tril, mid) + _dot(tril, lo))

        def first_stage(ci, h):
            sl = slice(h * dk, (h + 1) * dk)
            qh, kh, vh, ch = get_q(ci, sl), 1.0 - get_f(ci, sl), get_v(ci, sl), cums[ci][:, sl]
            s_t = st_ref[h]
            o = _dot_nt((qh * jnp.exp(ch)).astype(BF16), s_t.astype(BF16))
            last = ch[c - 1:c, :]
            kd = kh * jnp.exp(last - ch)
            s_new = s_t * jnp.exp(last) + _dot_tn(vh.astype(BF16), kd.astype(BF16))
            pieces = []
            n = c
            while n > n0:
                half = n // 2
                b = block_ref_rows(ch, n, half - 1)
                upper = (rowl & (n - 1)) >= half
                z = jnp.where(upper, qh, kh) * jnp.exp(jnp.where(upper, ch - b, b - ch))
                qt = jnp.where(upper, z, 0.0).astype(BF16)
                kt = jnp.where(upper, 0.0, z).astype(BF16)
                pieces.append((n, _dot_nt(qt, kt)))
                n = half
            dq = ch - block_ref_rows(ch, n0, n0 // 2 - 1)
            pieces.append((n0, _dot_nt((qh * jnp.exp(dq)).astype(BF16), (kh * jnp.exp(-dq)).astype(BF16))))
            return ci, h, s_new, o, pieces

        def second_stage(ci, h, s_new, o, pieces):
            sl = slice(h * dk, (h + 1) * dk)
            vh = get_v(ci, sl)
            attn = None
            for n, a in pieces:
                if n == n0:
                    keep = col <= row
                    if n0 < c:
                        keep = keep & same_block[n0]
                    a = jnp.where(keep, a, 0.0)
                elif n < c:
                    a = jnp.where(same_block[n], a, 0.0)
                attn = a if attn is None else attn + a
            store(ci, sl, o + _dot(attn.astype(BF16), vh.astype(BF16)))
            st_ref[h] = s_new

        pending = None
        for ci in range(nchunk):
            for h in range(HG_HEADS):
                nxt = first_stage(ci, h)
                if pending is not None:
                    second_stage(*pending)
                pending = nxt
        second_stage(*pending)

    do_chunks(nchunk, get_q, get_f, get_v, store)


def _hgrn_rec_kernel(tc, c, q_ref, f_ref, v_ref, s0_ref, o_ref, so_ref, st_ref):
    t = pl.program_id(1)

    @pl.when(t == 0)
    def _():
        for h in range(HG_HEADS):
            st_ref[h] = s0_ref[0, h].T

    if tc >= c:
        def rows(ci):
            return slice(ci * c, (ci + 1) * c)

        def store(ci, sl, o):
            o_ref[rows(ci), sl] = o

        _rec_chunks(c, tc // c,
                    lambda ci, sl: q_ref[0, rows(ci), sl].astype(F32),
                    lambda ci, sl: f_ref[rows(ci), sl],
                    lambda ci, sl: v_ref[0, rows(ci), sl].astype(F32), store, st_ref)
    else:
        pad = c - tc
        z = jnp.zeros((pad, q_ref.shape[-1]), F32)
        qp = jnp.concatenate([q_ref[0].astype(F32), z], axis=0)
        fp = jnp.concatenate([f_ref[...], z + 1.0], axis=0)
        vp = jnp.concatenate([v_ref[0].astype(F32), z], axis=0)

        def store(ci, sl, o):
            o_ref[:, sl] = o[:tc]

        _rec_chunks(c, 1, lambda ci, sl: qp[:, sl], lambda ci, sl: fp[:, sl], lambda ci, sl: vp[:, sl], store,
                    st_ref)

    @pl.when(t == pl.num_programs(1) - 1)
    def _():
        for h in range(HG_HEADS):
            so_ref[0, h] = st_ref[h].T


def _hgrn_rec(f, qig, s0, b, t_len, tc, c):
    m, d = f.shape
    nt = t_len // tc
    h, dk, dv = s0.shape[1:]

    def row_spec(which):
        return pl.BlockSpec((1, tc, d), lambda bi, ti: (which, bi * nt + ti, 0))

    return pl.pallas_call(
        functools.partial(_hgrn_rec_kernel, tc, c),
        out_shape=(jax.ShapeDtypeStruct((m, d), F32), jax.ShapeDtypeStruct(s0.shape, F32)),
        grid=(b, nt),
        in_specs=[row_spec(0), pl.BlockSpec((tc, d), lambda bi, ti: (bi * nt + ti, 0)), row_spec(1),
                  pl.BlockSpec((1, h, dk, dv), lambda bi, ti: (bi, 0, 0, 0))],
        out_specs=(pl.BlockSpec((tc, d), lambda bi, ti: (bi * nt + ti, 0)),
                   pl.BlockSpec((1, h, dk, dv), lambda bi, ti: (bi, 0, 0, 0))),
        scratch_shapes=[pltpu.VMEM((h, dv, dk), F32)],
        compiler_params=_params(("parallel", "arbitrary"), VMEM_MID),
        name="hgrn_rec",
    )(qig, f, qig, s0)


def _hgrn_out_kernel(o_ref, g_ref, gn_ref, x_ref, w_ref, y_ref):
    a = _rms_rows(o_ref[...], gn_ref[...]) * g_ref[0].astype(F32)
    y_ref[...] = x_ref[...] + _dot(a.astype(BF16), w_ref[...])


def _hgrn_out(o, qig, gn, x, w, tm):
    m, d = x.shape
    return pl.pallas_call(
        _hgrn_out_kernel,
        out_shape=jax.ShapeDtypeStruct((m, d), F32),
        grid=(m // tm,),
        in_specs=[
            pl.BlockSpec((tm, d), lambda i: (i, 0)),
            pl.BlockSpec((1, tm, d), lambda i: (2, i, 0)),
            pl.BlockSpec((1, d), lambda i: (0, 0)),
            pl.BlockSpec((tm, d), lambda i: (i, 0)),
            pl.BlockSpec((d, d), lambda i: (0, 0)),
        ],
        out_specs=pl.BlockSpec((tm, d), lambda i: (i, 0)),
        compiler_params=_params(("parallel",), VMEM_SMALL),
        name="hgrn_out",
    )(o, qig, gn, x, w)


def _ffn_kernel(x_ref, nw_ref, wu_ref, wd_ref, y_ref, xn_ref):
    j = pl.program_id(1)

    @pl.when(j == 0)
    def _():
        x = x_ref[...]
        xn_ref[...] = _rms_rows(x, nw_ref[...]).astype(BF16)
        y_ref[...] = x

    h = jnp.maximum(_dot(xn_ref[...], wu_ref[...]), 0.0)
    y_ref[...] += _dot((h * h).astype(BF16), wd_ref[...])


def _ffn(x, nw, wu, wd, tm, tf):
    m, d = x.shape
    ff = wu.shape[1]
    return pl.pallas_call(
        _ffn_kernel,
        out_shape=jax.ShapeDtypeStruct((m, d), F32),
        grid=(m // tm, ff // tf),
        in_specs=[
            pl.BlockSpec((tm, d), lambda i, j: (i, 0)),
            pl.BlockSpec((1, d), lambda i, j: (0, 0)),
            pl.BlockSpec((d, tf), lambda i, j: (0, j)),
            pl.BlockSpec((tf, d), lambda i, j: (j, 0)),
        ],
        out_specs=pl.BlockSpec((tm, d), lambda i, j: (i, 0)),
        scratch_shapes=[pltpu.VMEM((tm, d), BF16)],
        compiler_params=_params(("parallel", "arbitrary"), VMEM_MID),
        name="ffn",
    )(x, nw, wu, wd)


def _ffn_rows(x, nw_ref, wu_ref, wd_ref, y_ref, h_ref):
    xn = _rms_rows(x, nw_ref[...]).astype(BF16)
    for c in range(0, wu_ref.shape[-1], MXU_WIDTH):
        h = jnp.maximum(_dot(xn, wu_ref[:, c:c + MXU_WIDTH]), 0.0)
        h_ref[:, c:c + MXU_WIDTH] = (h * h).astype(BF16)
    hb = h_ref[...]
    for c in range(0, wd_ref.shape[-1], MXU_WIDTH):
        y_ref[:, c:c + MXU_WIDTH] = x[:, c:c + MXU_WIDTH] + _dot(hb, wd_ref[:, c:c + MXU_WIDTH])


def _hgrn_out_ffn_kernel(o_ref, g_ref, gn_ref, x_ref, wo_ref, nw_ref, wu_ref, wd_ref, y_ref, h_ref):
    a = _rms_rows(o_ref[...], gn_ref[...]) * g_ref[0].astype(F32)
    x = x_ref[...] + _dot(a.astype(BF16), wo_ref[...])
    _ffn_rows(x, nw_ref, wu_ref, wd_ref, y_ref, h_ref)


def _resident(shape):
    return pl.BlockSpec(shape, lambda i: (0,) * len(shape), pipeline_mode=pl.Buffered(1))


def _hgrn_out_ffn(o, qig, gn, x, wo, nw, wu, wd, tm):
    m, d = x.shape
    row = pl.BlockSpec((tm, d), lambda i: (i, 0))
    vec = pl.BlockSpec((1, d), lambda i: (0, 0))
    return pl.pallas_call(
        _hgrn_out_ffn_kernel,
        out_shape=jax.ShapeDtypeStruct((m, d), F32),
        grid=(m // tm,),
        in_specs=[row, pl.BlockSpec((1, tm, d), lambda i: (2, i, 0)), vec, row, _resident(wo.shape),
                  vec, _resident(wu.shape), _resident(wd.shape)],
        out_specs=row,
        scratch_shapes=[pltpu.VMEM((tm, wu.shape[-1]), BF16)],
        compiler_params=_params(("parallel",), VMEM_BIG),
        name="hgrn_out_ffn",
    )(o, qig, gn, x, wo, nw, wu, wd)


def _qk_norm_rows(y, qkn, gblk):
    y2 = (y * y).astype(BF16)
    w = gblk.shape[0]
    ss = jnp.concatenate([_dot(y2[:, c * w:(c + 1) * w], gblk) for c in range(y.shape[-1] // w)], axis=1)
    return y * lax.rsqrt(ss * (1.0 / ATT_HEAD_DIM) + RMS_EPS) * qkn


def _qkv_perm_kernel(tm, x_ref, nw_ref, w_ref, qkn_ref, gblk_ref, o0_ref, o1_ref, o2_ref, xs_ref, xp_ref):
    outs = (o0_ref, o1_ref, o2_ref)
    d = x_ref.shape[-1]
    nslab = d // LANES
    xn = _rms_rows(x_ref[...], nw_ref[...])
    xp_ref[0] = xn.astype(BF16)
    for c in range(nslab):
        xs_ref[c] = xn[:, c * LANES:(c + 1) * LANES]
    for g in range(1, N_GROUPS):
        dil = GROUPS[g][1]
        n = tm // dil
        for c in range(nslab):
            for r in range(dil):
                xp_ref[g, r * n:(r + 1) * n, c * LANES:(c + 1) * LANES] = (
                    xs_ref.at[c][pl.ds(r, n, stride=dil), :].astype(BF16))
    gblk = gblk_ref[...]

    def finish(g, which, c, y):
        dil = GROUPS[g][1]
        if which < 2:
            ss = _dot((y * y).astype(BF16), gblk)
            y = y * lax.rsqrt(ss * (1.0 / ATT_HEAD_DIM) + RMS_EPS) * qkn_ref[which:which + 1, c:c + MXU_WIDTH]
        outs[g][which, 0, :, :, c:c + MXU_WIDTH] = y.astype(BF16).reshape(dil, tm // dil, MXU_WIDTH)

    pending = None
    for g in range(N_GROUPS):
        xg = xp_ref[g]
        for which in range(3):
            base = (which * N_GROUPS + g) * d
            for c in range(0, d, MXU_WIDTH):
                y = _dot(xg, w_ref[:, base + c:base + c + MXU_WIDTH])
                if pending is not None:
                    finish(*pending)
                pending = (g, which, c, y)
    finish(*pending)


def _qkv_perm(x, nw, w, qkn, gblk, b, t_len, tm):
    m, d = x.shape
    nt = t_len // tm
    out_shapes, out_specs = [], []
    for _, dil in GROUPS:
        out_shapes.append(jax.ShapeDtypeStruct((3, b, dil, t_len // dil, d), BF16))
        out_specs.append(pl.BlockSpec((3, 1, dil, tm // dil, d), lambda i: (0, i // nt, 0, i % nt, 0)))
    return pl.pallas_call(
        functools.partial(_qkv_perm_kernel, tm),
        out_shape=tuple(out_shapes),
        grid=(m // tm,),
        in_specs=[
            pl.BlockSpec((tm, d), lambda i: (i, 0)),
            pl.BlockSpec((1, d), lambda i: (0, 0)),
            pl.BlockSpec(w.shape, lambda i: (0, 0), pipeline_mode=pl.Buffered(1)),
            pl.BlockSpec(qkn.shape, lambda i: (0, 0)),
            pl.BlockSpec(gblk.shape, lambda i: (0, 0)),
        ],
        out_specs=tuple(out_specs),
        scratch_shapes=[pltpu.VMEM((d // LANES, tm, LANES), F32),
                        pltpu.VMEM((N_GROUPS, tm, d), BF16)],
        compiler_params=_params(("parallel",), VMEM_BIG),
        name="qkv_perm",
    )(x, nw, w, qkn, gblk)


def _qkv_plain_kernel(x_ref, nw_ref, w_ref, qkn_ref, gblk_ref, o_ref, xn_ref):
    j = pl.program_id(1)

    @pl.when(j == 0)
    def _():
        xn_ref[...] = _rms_rows(x_ref[...], nw_ref[...]).astype(BF16)

    y = _dot(xn_ref[...], w_ref[...])

    @pl.when(j < 2 * N_GROUPS)
    def _():
        gain = jnp.where(j < N_GROUPS, qkn_ref[0:1, :], qkn_ref[1:2, :])
        o_ref[...] = _qk_norm_rows(y, gain, gblk_ref[...])

    @pl.when(j >= 2 * N_GROUPS)
    def _():
        o_ref[...] = y


def _qkv_plain(x, nw, w, qkn, gblk, tm):
    m, d = x.shape
    n = w.shape[1]
    return pl.pallas_call(
        _qkv_plain_kernel,
        out_shape=jax.ShapeDtypeStruct((m, n), F32),
        grid=(m // tm, n // d),
        in_specs=[
            pl.BlockSpec((tm, d), lambda i, j: (i, 0)),
            pl.BlockSpec((1, d), lambda i, j: (0, 0)),
            pl.BlockSpec((d, d), lambda i, j: (0, j)),
            pl.BlockSpec(qkn.shape, lambda i, j: (0, 0)),
            pl.BlockSpec(gblk.shape, lambda i, j: (0, 0)),
        ],
        out_specs=pl.BlockSpec((tm, d), lambda i, j: (i, j)),
        scratch_shapes=[pltpu.VMEM((tm, d), BF16)],
        compiler_params=_params(("parallel", "arbitrary"), VMEM_SMALL),
        name="qkv_plain",
    )(x, nw, w, qkn, gblk)


def _kv_tail_kernel(x_ref, nw_ref, wt_ref, kn_ref, o_ref, xn_ref):
    xn_ref[...] = _rms_rows(x_ref[...], nw_ref[...]).astype(BF16)
    tt = x_ref.shape[0]
    slab = min(tt, MXU_WIDTH)
    for kv in range(2):
        for c in range(0, tt, slab):
            yt = _dot_nt(wt_ref[kv], xn_ref[c:c + slab, :])
            if kv == 0:
                y3 = yt.reshape(ATT_HEADS, ATT_HEAD_DIM, slab)
                ms = jnp.mean(y3 * y3, axis=1, keepdims=True)
                yt = (y3 * lax.rsqrt(ms + RMS_EPS)).reshape(yt.shape) * kn_ref[...]
            o_ref[0, kv, :, c:c + slab] = yt


def _kv_tail(x, nw, wt, kn_col, b, t_len, keep, tt):
    m, d = x.shape
    first = (t_len - keep) // tt
    per_seq = t_len // tt
    return pl.pallas_call(
        _kv_tail_kernel,
        out_shape=jax.ShapeDtypeStruct((b, 2, d, keep), F32),
        grid=(b, keep // tt),
        in_specs=[
            pl.BlockSpec((tt, d), lambda bi, ti: (bi * per_seq + first + ti, 0)),
            pl.BlockSpec((1, d), lambda bi, ti: (0, 0)),
            pl.BlockSpec((2, d, d), lambda bi, ti: (0, 0, 0)),
            pl.BlockSpec((d, 1), lambda bi, ti: (0, 0)),
        ],
        out_specs=pl.BlockSpec((1, 2, d, tt), lambda bi, ti: (bi, 0, 0, ti)),
        scratch_shapes=[pltpu.VMEM((tt, d), BF16)],
        compiler_params=_params(("parallel", "parallel"), VMEM_SMALL),
        name="kv_tail",
    )(x, nw, wt, kn_col)


def _attn_prompt_kernel(rb, qb, q_ref, kc_ref, kp_ref, vc_ref, vp_ref, bias_ref, o_ref, m_ref, l_ref):
    i = pl.program_id(2)
    first = jnp.where(i == 0, 0, 1)
    lane = lax.broadcasted_iota(jnp.int32, (N_BACK, LANES), 1)
    lo_half = lane < ATT_HEAD_DIM
    npair = ATT_HEADS // 2
    stats = {}

    def scores(r, u, pr):
        rows = slice(u * N_BACK, (u + 1) * N_BACK)
        sl = slice(pr * LANES, (pr + 1) * LANES)
        qp = q_ref[0, 0, r, rows, sl]
        kprev = kp_ref[0, 0, r, :, sl] if u == 0 else kc_ref[0, 0, r, (u - 1) * N_BACK:u * N_BACK, sl]
        kk = jnp.concatenate([kprev, kc_ref[0, 0, r, rows, sl]], axis=0)
        zq = jnp.zeros_like(qp)
        q2 = jnp.concatenate([jnp.where(lo_half, qp, zq), jnp.where(lo_half, zq, qp)], axis=0)
        return _dot_nt(q2, kk)

    def finish(r, u, pr, s):
        rows = slice(u * N_BACK, (u + 1) * N_BACK)
        sl = slice(pr * LANES, (pr + 1) * LANES)
        vprev = vp_ref[0, 0, r, :, sl] if u == 0 else vc_ref[0, 0, r, (u - 1) * N_BACK:u * N_BACK, sl]
        vv = jnp.concatenate([vprev, vc_ref[0, 0, r, rows, sl]], axis=0)
        s = s + (bias_ref[first, pr] if u == 0 else bias_ref[1, pr])
        mx = jnp.max(s, axis=-1, keepdims=True)
        p = jnp.exp2(s - mx)
        l = jnp.sum(p, axis=-1, keepdims=True)
        o2 = _dot(p.astype(BF16), vv)
        o_ref[0, r, rows, sl] = jnp.where(lo_half, o2[:N_BACK], o2[N_BACK:]).astype(BF16)
        m_acc, l_acc = stats.get((r, u), (jnp.zeros((N_BACK, LANES), F32), jnp.ones((N_BACK, LANES), F32)))
        m_acc = jnp.where(lane == 2 * pr, mx[:N_BACK], jnp.where(lane == 2 * pr + 1, mx[N_BACK:], m_acc))
        l_acc = jnp.where(lane == 2 * pr, l[:N_BACK], jnp.where(lane == 2 * pr + 1, l[N_BACK:], l_acc))
        stats[(r, u)] = (m_acc, l_acc)
        if pr == npair - 1:
            m_ref[0, r, rows, :] = m_acc
            l_ref[0, r, rows, :] = l_acc

    pending = None
    for r in range(rb):
        for u in range(qb):
            for pr in range(npair):
                s = scores(r, u, pr)
                if pending is not None:
                    finish(*pending)
                pending = (r, u, pr, s)
    finish(*pending)


def _attn_prompt(qkv_g, bias, rb, qb):
    _, b, dil, l, d = qkv_g.shape
    rows = qb * N_BACK

    def cur(which):
        return pl.BlockSpec((1, 1, rb, rows, d), lambda bi, r, i: (which, bi, r, i, 0))

    def prev(which):
        return pl.BlockSpec((1, 1, rb, N_BACK, d),
                            lambda bi, r, i: (which, bi, r, jnp.maximum(i * qb - 1, 0), 0))

    return pl.pallas_call(
        functools.partial(_attn_prompt_kernel, rb, qb),
        out_shape=(jax.ShapeDtypeStruct((b, dil, l, d), BF16),
                   jax.ShapeDtypeStruct((b, dil, l, LANES), F32),
                   jax.ShapeDtypeStruct((b, dil, l, LANES), F32)),
        grid=(b, dil // rb, l // rows),
        in_specs=[cur(0), cur(1), prev(1), cur(2), prev(2),
                  pl.BlockSpec(bias.shape, lambda bi, r, i: (0, 0, 0, 0))],
        out_specs=(pl.BlockSpec((1, rb, rows, d), lambda bi, r, i: (bi, r, i, 0)),
                   pl.BlockSpec((1, rb, rows, LANES), lambda bi, r, i: (bi, r, i, 0)),
                   pl.BlockSpec((1, rb, rows, LANES), lambda bi, r, i: (bi, r, i, 0))),
        compiler_params=_params(("parallel", "parallel", "arbitrary"), VMEM_SMALL),
        name="attn_prompt_d%d" % dil,
    )(qkv_g, qkv_g, qkv_g, qkv_g, qkv_g, bias)


def _merge_out_ffn_kernel(tm, o0_ref, o1_ref, o2_ref, m0_ref, m1_ref, m2_ref, l0_ref, l1_ref, l2_ref,
                          gexp_ref, x_ref, w_ref, nw_ref, wu_ref, wd_ref, y_ref,
                          os_ref, ms_ref, ls_ref, h_ref):
    d = x_ref.shape[-1]
    nslab = d // LANES

    def natural(o_ref, m_ref, l_ref, dil):
        if dil == 1:
            return o_ref[0, 0].astype(F32), m_ref[0, 0], l_ref[0, 0]
        n = tm // dil
        for r in range(dil):
            ms_ref[pl.ds(r, n, stride=dil), :] = m_ref[0, r]
            ls_ref[pl.ds(r, n, stride=dil), :] = l_ref[0, r]
            for c in range(nslab):
                os_ref.at[c][pl.ds(r, n, stride=dil), :] = o_ref[0, r, :, c * LANES:(c + 1) * LANES].astype(F32)
        return jnp.concatenate([os_ref[c] for c in range(nslab)], axis=1), ms_ref[...], ls_ref[...]

    trip = [natural(o_ref, m_ref, l_ref, GROUPS[g][1]) for g, (o_ref, m_ref, l_ref) in enumerate(
        ((o0_ref, m0_ref, l0_ref), (o1_ref, m1_ref, l1_ref), (o2_ref, m2_ref, l2_ref)))]
    mx = jnp.maximum(jnp.maximum(trip[0][1], trip[1][1]), trip[2][1])
    es = [jnp.exp2(t[1] - mx) for t in trip]
    inv = 1.0 / (es[0] * trip[0][2] + es[1] * trip[1][2] + es[2] * trip[2][2])
    acc = None
    for e, (o, _, _) in zip(es, trip):
        wt = e * inv
        hi = wt.astype(BF16)
        lo = (wt - hi.astype(F32)).astype(BF16)
        wx = _dot(jnp.concatenate([hi, lo], axis=1), gexp_ref[...])
        acc = wx * o if acc is None else acc + wx * o
    x = x_ref[...] + _dot(acc.astype(BF16), w_ref[...])
    _ffn_rows(x, nw_ref, wu_ref, wd_ref, y_ref, h_ref)


def _merge_out_ffn(outs, ms, ls, gexp, x, w, nw, wu, wd, t_len, tm):
    m, d = x.shape
    nt = t_len // tm
    row = pl.BlockSpec((tm, d), lambda i: (i, 0))
    o_specs, s_specs = [], []
    for _, dil in GROUPS:
        o_specs.append(pl.BlockSpec((1, dil, tm // dil, d), lambda i: (i // nt, 0, i % nt, 0)))
        s_specs.append(pl.BlockSpec((1, dil, tm // dil, LANES), lambda i: (i // nt, 0, i % nt, 0)))
    return pl.pallas_call(
        functools.partial(_merge_out_ffn_kernel, tm),
        out_shape=jax.ShapeDtypeStruct((m, d), F32),
        grid=(m // tm,),
        in_specs=o_specs + s_specs + s_specs + [_resident(gexp.shape), row, _resident(w.shape),
                                                pl.BlockSpec((1, d), lambda i: (0, 0)),
                                                _resident(wu.shape), _resident(wd.shape)],
        out_specs=row,
        scratch_shapes=[pltpu.VMEM((d // LANES, tm, LANES), F32), pltpu.VMEM((tm, LANES), F32),
                        pltpu.VMEM((tm, LANES), F32), pltpu.VMEM((tm, wu.shape[-1]), BF16)],
        compiler_params=_params(("parallel",), VMEM_BIG),
        name="merge_out_ffn",
    )(*outs, *ms, *ls, gexp, x, w, nw, wu, wd)


SAMPLE_PAIRS = 4


def _attn_sample_kernel(ts, *refs):
    ng = N_GROUPS
    q_refs, k_refs, v_refs = refs[0:ng], refs[ng:2 * ng], refs[2 * ng:3 * ng]
    c_refs = refs[3 * ng:4 * ng]
    bias_ref, o_ref = refs[4 * ng], refs[4 * ng + 1]
    npair = SAMPLE_PAIRS
    lane = lax.broadcasted_iota(jnp.int32, (ts, LANES), 1)
    lo_half = lane < ATT_HEAD_DIM
    zpad = jnp.zeros((LANES - npair * ts, LANES), F32)

    tk, vs = [], []
    for g in range(ng):
        kst = jnp.concatenate([k_refs[g][0, :, p * LANES:(p + 1) * LANES] for p in range(npair)] + [zpad], axis=0)
        vst = jnp.concatenate([v_refs[g][0, :, p * LANES:(p + 1) * LANES] for p in range(npair)] + [zpad], axis=0)
        tk.append(kst.T.astype(BF16))
        vs.append(vst.astype(BF16))

    rows = []
    for p in range(npair):
        sl = slice(p * LANES, (p + 1) * LANES)
        old, new = [], []
        for g in range(ng):
            qp = q_refs[g][0, :, sl]
            q2 = jnp.concatenate([jnp.where(lo_half, qp, 0.0), jnp.where(lo_half, 0.0, qp)], axis=0).astype(BF16)
            kt = c_refs[g][0, 0, 2 * p:2 * p + 2].reshape(LANES, c_refs[g].shape[-1]).astype(BF16)
            old.append(_dot(q2, kt))
            new.append(_dot(q2, tk[g]))
        rows.append(jnp.concatenate(old + new, axis=1))
    s = jnp.concatenate(rows, axis=0) + bias_ref[...].reshape(npair * 2 * ts, -1)
    mx = jnp.max(s, axis=-1, keepdims=True)
    pr = jnp.exp2(s - mx)
    inv = 1.0 / jnp.sum(pr, axis=-1, keepdims=True)
    prb = pr.astype(BF16)

    for p in range(npair):
        sl = slice(p * LANES, (p + 1) * LANES)
        r0 = p * 2 * ts
        acc = jnp.zeros((2 * ts, LANES), F32)
        off = 0
        for g in range(ng):
            window = c_refs[g].shape[-1]
            vt = c_refs[g][0, 1, 2 * p:2 * p + 2].reshape(LANES, window).astype(BF16)
            acc = acc + _dot_nt(prb[r0:r0 + 2 * ts, off:off + window], vt)
            off += window
        for g in range(ng):
            acc = acc + _dot(prb[r0:r0 + 2 * ts, off:off + LANES], vs[g])
            off += LANES
        om = acc * inv[r0:r0 + 2 * ts]
        o_ref[0, :, sl] = jnp.where(lo_half, om[:ts], om[ts:])


def _attn_sample(qkv32, caches_t, bias):
    b, ts, _ = qkv32.shape
    d = ATT_HEADS * ATT_HEAD_DIM
    width = SAMPLE_PAIRS * LANES
    per_blk = d // width
    hb = 2 * SAMPLE_PAIRS

    def new_rows(which, g):
        return pl.BlockSpec((1, ts, width), lambda bi, hq: (bi, 0, (which * N_GROUPS + g) * per_blk + hq))

    in_specs = [new_rows(w, g) for w in range(3) for g in range(N_GROUPS)]
    for c in caches_t:
        in_specs.append(pl.BlockSpec((1, 2, hb, ATT_HEAD_DIM, c.shape[-1]), lambda bi, hq: (bi, 0, hq, 0, 0)))
    in_specs.append(pl.BlockSpec((SAMPLE_PAIRS,) + bias.shape[1:], lambda bi, hq: (hq, 0, 0)))
    return pl.pallas_call(
        functools.partial(_attn_sample_kernel, ts),
        out_shape=jax.ShapeDtypeStruct((b, ts, d), F32),
        grid=(b, per_blk),
        in_specs=in_specs,
        out_specs=pl.BlockSpec((1, ts, width), lambda bi, hq: (bi, 0, hq)),
        compiler_params=_params(("parallel", "arbitrary"), VMEM_BIG),
        name="attn_sample",
    )(*([qkv32] * (3 * N_GROUPS)), *caches_t, bias)


def _matmul_res_kernel(a_ref, x_ref, w_ref, y_ref):
    y_ref[...] = x_ref[...] + _dot(a_ref[...].astype(BF16), w_ref[...])


def _matmul_res(a, x, w, tm):
    m, d = x.shape
    row = pl.BlockSpec((tm, d), lambda i: (i, 0))
    return pl.pallas_call(
        _matmul_res_kernel,
        out_shape=jax.ShapeDtypeStruct((m, d), F32),
        grid=(m // tm,),
        in_specs=[row, row, pl.BlockSpec((d, d), lambda i: (0, 0))],
        out_specs=row,
        compiler_params=_params(("parallel",), VMEM_SMALL),
        name="matmul_res",
    )(a, x, w)


def _alibi_slopes():
    n = N_GROUPS * ATT_HEADS
    return LOG2E * (2.0 ** (-8.0 * np.arange(1, n + 1) / n)).reshape(N_GROUPS, ATT_HEADS)


def _prompt_bias(g):
    slopes = _alibi_slopes()[g] * GROUPS[g][1]
    dist = np.arange(N_BACK)[:, None] - np.arange(2 * N_BACK)[None, :] + N_BACK
    band = (dist >= 0) & (dist <= N_BACK)
    has_prev = np.arange(2 * N_BACK)[None, :] >= N_BACK
    out = np.empty((2, ATT_HEADS, N_BACK, 2 * N_BACK), np.float32)
    for k, valid in enumerate((band & has_prev, band)):
        out[k] = np.where(valid[None], -slopes[:, None, None] * dist[None], NEG)
    return jnp.asarray(out.reshape(2, ATT_HEADS // 2, 2 * N_BACK, 2 * N_BACK))


def _sample_bias(ts):
    slopes = _alibi_slopes()
    npairs = ATT_HEADS // 2
    old, new = [], []
    for g, (window, dil) in enumerate(GROUPS):
        pos = np.arange(window)
        bc = np.full((npairs, 2 * ts, window), NEG, np.float32)
        bn = np.full((npairs, 2 * ts, LANES), NEG, np.float32)
        for h in range(ATT_HEADS):
            for s in range(ts):
                row = (h % 2) * ts + s
                back = window + s - pos
                valid = (back % dil == 0) & (back <= window)
                bc[h // 2, row] = np.where(valid, -slopes[g, h] * back, NEG)
                lane0 = ((h // 2) % SAMPLE_PAIRS) * ts
                for s2 in range(s + 1):
                    if (s - s2) % dil == 0:
                        bn[h // 2, row, lane0 + s2] = -slopes[g, h] * (s - s2)
        old.append(bc)
        new.append(bn)
    return jnp.asarray(np.concatenate(old + new, axis=-1))


def _head_group_matrices():
    d = ATT_HEADS * ATT_HEAD_DIM
    i = np.arange(2 * LANES)
    gblk = (i[:, None] // ATT_HEAD_DIM == i[None, :] // ATT_HEAD_DIM).astype(np.float32)
    gexp = (np.arange(LANES)[:, None] == np.arange(d)[None, :] // ATT_HEAD_DIM).astype(np.float32)
    return jnp.asarray(gblk, BF16), jnp.asarray(np.concatenate([gexp, gexp], axis=0), BF16)


def _tile(m, want):
    return want if m % want == 0 else m


def kernel(x_prompt, x_sample, state_hgrn, cache_kv_w128, cache_kv_w512, cache_kv_w2048, hg_lb_logits, hg_w_q, hg_w_f, hg_w_i, hg_w_g, hg_w_o, hg_norm_o, att_w_qkv, att_w_o, att_q_norm, att_k_norm, norm_mix, norm_ffn, ffn_w_up, ffn_w_down):
    b, t_len, d = x_prompt.shape
    bs, ts, _ = x_sample.shape
    depth = norm_mix.shape[0]
    caches_all = (cache_kv_w128, cache_kv_w512, cache_kv_w2048)
    gblk, gexp = _head_group_matrices()
    sample_bias = _sample_bias(ts)

    yp = x_prompt.reshape(b * t_len, d)
    ys = x_sample.reshape(bs * ts, d)
    tm_p = _tile(b * t_len, ROW_TILE)
    tm_s = _tile(bs * ts, SAMPLE_ROW_TILE)
    hg_p, hg_s = [], []
    kv_p = [[] for _ in GROUPS]
    kv_s = [[] for _ in GROUPS]
    for layer in range(depth):
        a = layer // 2
        nw = norm_mix[layer][None]
        nf = norm_ffn[layer][None]
        wu = ffn_w_up[layer].astype(BF16)
        wd = ffn_w_down[layer].astype(BF16)
        if layer % 2 == 0:
            w4 = jnp.stack([hg_w_q[a], hg_w_i[a], hg_w_g[a], hg_w_f[a]]).astype(BF16)
            wo = hg_w_o[a].astype(BF16)
            gn = hg_norm_o[a][None]
            zeros = jnp.zeros((b,) + state_hgrn.shape[2:], F32)
            f, qig = _hgrn_proj(yp, nw, hg_lb_logits, w4, layer, tm_p, BF16)
            o, sp = _hgrn_rec(f, qig, zeros, b, t_len, _tile(t_len, ROW_TILE), HG_CHUNK)
            yp = _hgrn_out_ffn(o, qig, gn, yp, wo, nf, wu, wd, tm_p)
            f, qig = _hgrn_proj(ys, nw, hg_lb_logits, w4, layer, tm_s, F32)
            o, ss = _hgrn_rec(f, qig, state_hgrn[a], bs, ts, ts, HG_SUBCHUNK)
            ys = _hgrn_out(o, qig, gn, ys, wo, tm_s)
            hg_p.append(sp)
            hg_s.append(ss)
        else:
            wqkv = att_w_qkv[a].astype(BF16)
            wo = att_w_o[a].astype(BF16)
            qn_row = jnp.tile(att_q_norm[a], ATT_HEADS)
            kn_row = jnp.tile(att_k_norm[a], ATT_HEADS)
            qkn = jnp.stack([qn_row * (ATT_SCALE * LOG2E), kn_row])
            qkv_groups = _qkv_perm(yp, nw, wqkv, qkn, gblk, b, t_len, tm_p)
            outs, ms, ls = [], [], []
            for g, (window, dil) in enumerate(GROUPS):
                qb = max(q for q in (1, 2, 4, 8) if q <= ATTN_STEP_BLOCKS and (t_len // dil) % (q * N_BACK) == 0)
                rb = max(r for r in (1, 2, 4, 8) if r * qb <= ATTN_STEP_BLOCKS and dil % r == 0)
                o, mrow, lrow = _attn_prompt(qkv_groups[g], _prompt_bias(g), rb, qb)
                outs.append(o)
                ms.append(mrow)
                ls.append(lrow)
                keep = min(window, t_len)
                wt = jnp.stack([wqkv[:, (N_GROUPS + g) * d:(N_GROUPS + g + 1) * d].T,
                                wqkv[:, (2 * N_GROUPS + g) * d:(2 * N_GROUPS + g + 1) * d].T])
                kvt = _kv_tail(yp, nw, wt, kn_row[:, None], b, t_len, keep, _tile(keep, ROW_TILE))
                kvt = kvt.reshape(b, 2, ATT_HEADS, ATT_HEAD_DIM, keep)
                kv_p[g].append(jnp.transpose(kvt, (0, 4, 1, 2, 3)))
            yp = _merge_out_ffn(outs, ms, ls, gexp, yp, wo, nf, wu, wd, t_len, tm_p)
            qkv32 = _qkv_plain(ys, nw, wqkv, qkn, gblk, tm_s).reshape(bs, ts, -1)
            caches_t = [jnp.transpose(c[a], (0, 2, 3, 4, 1)) for c in caches_all]
            om = _attn_sample(qkv32, caches_t, sample_bias)
            ys = _matmul_res(om.reshape(bs * ts, d), ys, wo, tm_s)
            kv = qkv32.reshape(bs, ts, 3, N_GROUPS, ATT_HEADS, ATT_HEAD_DIM)
            for g in range(N_GROUPS):
                kv_s[g].append(kv[:, :, 1:, g])
        ys = _ffn(ys, nf, wu, wd, tm_s, 1024)
    return (yp.reshape(b, t_len, d), ys.reshape(bs, ts, d), jnp.stack(hg_p), jnp.stack(hg_s),
            jnp.stack(kv_p[0]), jnp.stack(kv_s[0]),
            jnp.stack(kv_p[1]), jnp.stack(kv_s[1]),
            jnp.stack(kv_p[2]), jnp.stack(kv_s[2]))
```

```python
import functools

import numpy as np
import jax
import jax.numpy as jnp
from jax import lax
from jax.experimental import pallas as pl
from jax.experimental.pallas import tpu as pltpu

F32 = jnp.float32
BF16 = jnp.bfloat16

RMS_EPS = 1e-6
HG_HEADS = 8
HG_DK = 128
GROUPS = ((128, 1), (512, 4), (2048, 16))
N_GROUPS = len(GROUPS)
ATT_HEADS = 16
ATT_HEAD_DIM = 64
ATT_SCALE = ATT_HEAD_DIM ** -0.5
N_BACK = 128
NEG = -1e30
LOG2E = 1.4426950408889634

LANES = 128
MXU_WIDTH = 256
ATTN_STEP_BLOCKS = 8
ROW_TILE = 512
PROJ_ROW_TILE = 1024
SAMPLE_ROW_TILE = 256
HG_CHUNK = 128
HG_SUBCHUNK = 16
VMEM_SMALL, VMEM_MID, VMEM_BIG = 40, 48, 56
MIB = 1024 * 1024


def _dot(a, b):
    return jnp.dot(a, b, preferred_element_type=F32)


def _dot_nt(a, b):
    return lax.dot_general(a, b, (((1,), (1,)), ((), ())), preferred_element_type=F32)


def _dot_tn(a, b):
    return lax.dot_general(a, b, (((0,), (0,)), ((), ())), preferred_element_type=F32)


def _split3(x):
    hi = x.astype(BF16)
    r1 = x - hi.astype(F32)
    mid = r1.astype(BF16)
    lo = (r1 - mid.astype(F32)).astype(BF16)
    return hi, mid, lo


def _rms_rows(x, w):
    ms = jnp.mean(x * x, axis=-1, keepdims=True)
    return x * lax.rsqrt(ms + RMS_EPS) * w


def _params(sem, vmem_mib):
    return pltpu.CompilerParams(dimension_semantics=sem, vmem_limit_bytes=vmem_mib * MIB)


def _hgrn_proj_kernel(layer, x_ref, nw_ref, lbl_ref, w_ref, f_ref, o_ref):
    xn = _rms_rows(x_ref[...], nw_ref[...]).astype(BF16)
    lg = lbl_ref[...]
    e = jnp.exp(lg - jnp.max(lg, axis=0, keepdims=True))
    lb_row = jnp.sum(e[:layer + 1], axis=0, keepdims=True) / jnp.sum(e, axis=0, keepdims=True)
    for p in range(4):
        for c in range(0, x_ref.shape[-1], MXU_WIDTH):
            cols = slice(c, c + MXU_WIDTH)
            y = _dot(xn, w_ref[p, :, cols])
            if p == 1:
                o_ref[p, :, cols] = y.astype(o_ref.dtype)
            elif p == 3:
                lb = lb_row[:, cols]
                f_ref[:, cols] = lb + (1.0 - lb) * jax.nn.sigmoid(y)
            else:
                o_ref[p, :, cols] = (y * jax.nn.sigmoid(y)).astype(o_ref.dtype)


def _hgrn_proj(x, nw, lb_logits, w4, layer, tm, qig_dtype):
    m, d = x.shape
    return pl.pallas_call(
        functools.partial(_hgrn_proj_kernel, layer),
        out_shape=(jax.ShapeDtypeStruct((m, d), F32), jax.ShapeDtypeStruct((3, m, d), qig_dtype)),
        grid=(m // tm,),
        in_specs=[
            pl.BlockSpec((tm, d), lambda i: (i, 0)),
            pl.BlockSpec((1, d), lambda i: (0, 0)),
            pl.BlockSpec(lb_logits.shape, lambda i: (0, 0)),
            pl.BlockSpec((4, d, d), lambda i: (0, 0, 0)),
        ],
        out_specs=(pl.BlockSpec((tm, d), lambda i: (i, 0)),
                   pl.BlockSpec((3, tm, d), lambda i: (0, i, 0))),
        compiler_params=_params(("parallel",), VMEM_MID),
        name="hgrn_proj",
    )(x, nw, lb_logits, w4)


def _rec_chunks(c, nchunk, get_q, get_f, get_v, store, st_ref):
    dk = HG_DK
    row = lax.broadcasted_iota(jnp.int32, (c, c), 0)
    col = lax.broadcasted_iota(jnp.int32, (c, c), 1)
    tril = jnp.where(row >= col, 1.0, 0.0).astype(BF16)
    rowl = lax.broadcasted_iota(jnp.int32, (c, dk), 0)
    n0 = min(c, HG_SUBCHUNK)
    same_block = {}
    n = c // 2
    while n >= n0:
        shift = n.bit_length() - 1
        same_block[n] = (row >> shift) == (col >> shift)
        n //= 2

    def block_ref_rows(ch, n, pick):
        parts = [jnp.broadcast_to(ch[j * n + pick:j * n + pick + 1, :], (n, dk)) for j in range(c // n)]
        return parts[0] if len(parts) == 1 else jnp.concatenate(parts, axis=0)

    def do_chunks(nchunk, get_q, get_f, get_v, store):
        cums = []
        for ci in range(nchunk):
            hi, mid, lo = _split3(jnp.log(get_f(ci, slice(None))))
            cums.append(_dot(tril, hi) + _dot(tril, mid) + _dot(tril, lo))

        def first_stage(ci, h):
            sl = slice(h * dk, (h + 1) * dk)
            qh, kh, vh, ch = get_q(ci, sl), 1.0 - get_f(ci, sl), get_v(ci, sl), cums[ci][:, sl]
            s_t = st_ref[h]
            o = _dot_nt((qh * jnp.exp(ch)).astype(BF16), s_t.astype(BF16))
            last = ch[c - 1:c, :]
            kd = kh * jnp.exp(last - ch)
            s_new = s_t * jnp.exp(last) + _dot_tn(vh.astype(BF16), kd.astype(BF16))
            pieces = []
            n = c
            while n > n0:
                half = n // 2
                b = block_ref_rows(ch, n, half - 1)
                upper = (rowl & (n - 1)) >= half
                z = jnp.where(upper, qh, kh) * jnp.exp(jnp.where(upper, ch - b, b - ch))
                qt = jnp.where(upper, z, 0.0).astype(BF16)
                kt = jnp.where(upper, 0.0, z).astype(BF16)
                pieces.append((n, _dot_nt(qt, kt)))
                n = half
            dq = ch - block_ref_rows(ch, n0, n0 // 2 - 1)
            pieces.append((n0, _dot_nt((qh * jnp.exp(dq)).astype(BF16), (kh * jnp.exp(-dq)).astype(BF16))))
            return ci, h, s_new, o, pieces

        def second_stage(ci, h, s_new, o, pieces):
            sl = slice(h * dk, (h + 1) * dk)
            vh = get_v(ci, sl)
            attn = None
            for n, a in pieces:
                if n == n0:
                    keep = col <= row
                    if n0 < c:
                        keep = keep & same_block[n0]
                    a = jnp.where(keep, a, 0.0)
                elif n < c:
                    a = jnp.where(same_block[n], a, 0.0)
                attn = a if attn is None else attn + a
            store(ci, sl, o + _dot(attn.astype(BF16), vh.astype(BF16)))
            st_ref[h] = s_new

        pending = None
        for ci in range(nchunk):
            for h in range(HG_HEADS):
                nxt = first_stage(ci, h)
                if pending is not None:
                    second_stage(*pending)
                pending = nxt
        second_stage(*pending)

    do_chunks(nchunk, get_q, get_f, get_v, store)


def _hgrn_rec_kernel(tc, c, q_ref, f_ref, v_ref, s0_ref, o_ref, so_ref, st_ref):
    t = pl.program_id(1)

    @pl.when(t == 0)
    def _():
        for h in range(HG_HEADS):
            st_ref[h] = s0_ref[0, h].T

    if tc >= c:
        def rows(ci):
            return slice(ci * c, (ci + 1) * c)

        def store(ci, sl, o):
            o_ref[rows(ci), sl] = o

        _rec_chunks(c, tc // c,
                    lambda ci, sl: q_ref[0, rows(ci), sl].astype(F32),
                    lambda ci, sl: f_ref[rows(ci), sl],
                    lambda ci, sl: v_ref[0, rows(ci), sl].astype(F32), store, st_ref)
    else:
        pad = c - tc
        z = jnp.zeros((pad, q_ref.shape[-1]), F32)
        qp = jnp.concatenate([q_ref[0].astype(F32), z], axis=0)
        fp = jnp.concatenate([f_ref[...], z + 1.0], axis=0)
        vp = jnp.concatenate([v_ref[0].astype(F32), z], axis=0)

        def store(ci, sl, o):
            o_ref[:, sl] = o[:tc]

        _rec_chunks(c, 1, lambda ci, sl: qp[:, sl], lambda ci, sl: fp[:, sl], lambda ci, sl: vp[:, sl], store,
                    st_ref)

    @pl.when(t == pl.num_programs(1) - 1)
    def _():
        for h in range(HG_HEADS):
            so_ref[0, h] = st_ref[h].T


def _hgrn_rec(f, qig, s0, b, t_len, tc, c):
    m, d = f.shape
    nt = t_len // tc
    h, dk, dv = s0.shape[1:]

    def row_spec(which):
        return pl.BlockSpec((1, tc, d), lambda bi, ti: (which, bi * nt + ti, 0))

    return pl.pallas_call(
        functools.partial(_hgrn_rec_kernel, tc, c),
        out_shape=(jax.ShapeDtypeStruct((m, d), F32), jax.ShapeDtypeStruct(s0.shape, F32)),
        grid=(b, nt),
        in_specs=[row_spec(0), pl.BlockSpec((tc, d), lambda bi, ti: (bi * nt + ti, 0)), row_spec(1),
                  pl.BlockSpec((1, h, dk, dv), lambda bi, ti: (bi, 0, 0, 0))],
        out_specs=(pl.BlockSpec((tc, d), lambda bi, ti: (bi * nt + ti, 0)),
                   pl.BlockSpec((1, h, dk, dv), lambda bi, ti: (bi, 0, 0, 0))),
        scratch_shapes=[pltpu.VMEM((h, dv, dk), F32)],
        compiler_params=_params(("parallel", "arbitrary"), VMEM_MID),
        name="hgrn_rec",
    )(qig, f, qig, s0)


def _hgrn_out_kernel(o_ref, g_ref, gn_ref, x_ref, w_ref, y_ref):
    a = _rms_rows(o_ref[...], gn_ref[...]) * g_ref[0].astype(F32)
    y_ref[...] = x_ref[...] + _dot(a.astype(BF16), w_ref[...])


def _hgrn_out(o, qig, gn, x, w, tm):
    m, d = x.shape
    return pl.pallas_call(
        _hgrn_out_kernel,
        out_shape=jax.ShapeDtypeStruct((m, d), F32),
        grid=(m // tm,),
        in_specs=[
            pl.BlockSpec((tm, d), lambda i: (i, 0)),
            pl.BlockSpec((1, tm, d), lambda i: (2, i, 0)),
            pl.BlockSpec((1, d), lambda i: (0, 0)),
            pl.BlockSpec((tm, d), lambda i: (i, 0)),
            pl.BlockSpec((d, d), lambda i: (0, 0)),
        ],
        out_specs=pl.BlockSpec((tm, d), lambda i: (i, 0)),
        compiler_params=_params(("parallel",), VMEM_SMALL),
        name="hgrn_out",
    )(o, qig, gn, x, w)


def _ffn_kernel(x_ref, nw_ref, wu_ref, wd_ref, y_ref, xn_ref):
    j = pl.program_id(1)

    @pl.when(j == 0)
    def _():
        x = x_ref[...]
        xn_ref[...] = _rms_rows(x, nw_ref[...]).astype(BF16)
        y_ref[...] = x

    h = jnp.maximum(_dot(xn_ref[...], wu_ref[...]), 0.0)
    y_ref[...] += _dot((h * h).astype(BF16), wd_ref[...])


def _ffn(x, nw, wu, wd, tm, tf):
    m, d = x.shape
    ff = wu.shape[1]
    return pl.pallas_call(
        _ffn_kernel,
        out_shape=jax.ShapeDtypeStruct((m, d), F32),
        grid=(m // tm, ff // tf),
        in_specs=[
            pl.BlockSpec((tm, d), lambda i, j: (i, 0)),
            pl.BlockSpec((1, d), lambda i, j: (0, 0)),
            pl.BlockSpec((d, tf), lambda i, j: (0, j)),
            pl.BlockSpec((tf, d), lambda i, j: (j, 0)),
        ],
        out_specs=pl.BlockSpec((tm, d), lambda i, j: (i, 0)),
        scratch_shapes=[pltpu.VMEM((tm, d), BF16)],
        compiler_params=_params(("parallel", "arbitrary"), VMEM_MID),
        name="ffn",
    )(x, nw, wu, wd)


def _ffn_rows(x, nw_ref, wu_ref, wd_ref, y_ref, h_ref):
    xn = _rms_rows(x, nw_ref[...]).astype(BF16)
    for c in range(0, wu_ref.shape[-1], MXU_WIDTH):
        h = jnp.maximum(_dot(xn, wu_ref[:, c:c + MXU_WIDTH]), 0.0)
        h_ref[:, c:c + MXU_WIDTH] = (h * h).astype(BF16)
    hb = h_ref[...]
    for c in range(0, wd_ref.shape[-1], MXU_WIDTH):
        y_ref[:, c:c + MXU_WIDTH] = x[:, c:c + MXU_WIDTH] + _dot(hb, wd_ref[:, c:c + MXU_WIDTH])


def _hgrn_out_ffn_kernel(o_ref, g_ref, gn_ref, x_ref, wo_ref, nw_ref, wu_ref, wd_ref, y_ref, h_ref):
    a = _rms_rows(o_ref[...], gn_ref[...]) * g_ref[0].astype(F32)
    x = x_ref[...] + _dot(a.astype(BF16), wo_ref[...])
    _ffn_rows(x, nw_ref, wu_ref, wd_ref, y_ref, h_ref)


def _resident(shape):
    return pl.BlockSpec(shape, lambda i: (0,) * len(shape), pipeline_mode=pl.Buffered(1))


def _hgrn_out_ffn(o, qig, gn, x, wo, nw, wu, wd, tm):
    m, d = x.shape
    row = pl.BlockSpec((tm, d), lambda i: (i, 0))
    vec = pl.BlockSpec((1, d), lambda i: (0, 0))
    return pl.pallas_call(
        _hgrn_out_ffn_kernel,
        out_shape=jax.ShapeDtypeStruct((m, d), F32),
        grid=(m // tm,),
        in_specs=[row, pl.BlockSpec((1, tm, d), lambda i: (2, i, 0)), vec, row, _resident(wo.shape),
                  vec, _resident(wu.shape), _resident(wd.shape)],
        out_specs=row,
        scratch_shapes=[pltpu.VMEM((tm, wu.shape[-1]), BF16)],
        compiler_params=_params(("parallel",), VMEM_BIG),
        name="hgrn_out_ffn",
    )(o, qig, gn, x, wo, nw, wu, wd)


def _qk_norm_rows(y, qkn, gblk):
    y2 = (y * y).astype(BF16)
    w = gblk.shape[0]
    ss = jnp.concatenate([_dot(y2[:, c * w:(c + 1) * w], gblk) for c in range(y.shape[-1] // w)], axis=1)
    return y * lax.rsqrt(ss * (1.0 / ATT_HEAD_DIM) + RMS_EPS) * qkn


def _qkv_perm_kernel(tm, x_ref, nw_ref, w_ref, qkn_ref, gblk_ref, o0_ref, o1_ref, o2_ref, xs_ref, xp_ref):
    outs = (o0_ref, o1_ref, o2_ref)
    d = x_ref.shape[-1]
    nslab = d // LANES
    xn = _rms_rows(x_ref[...], nw_ref[...])
    xp_ref[0] = xn.astype(BF16)
    for c in range(nslab):
        xs_ref[c] = xn[:, c * LANES:(c + 1) * LANES]
    for g in range(1, N_GROUPS):
        dil = GROUPS[g][1]
        n = tm // dil
        for c in range(nslab):
            for r in range(dil):
                xp_ref[g, r * n:(r + 1) * n, c * LANES:(c + 1) * LANES] = (
                    xs_ref.at[c][pl.ds(r, n, stride=dil), :].astype(BF16))
    gblk = gblk_ref[...]

    def finish(g, which, c, y):
        dil = GROUPS[g][1]
        if which < 2:
            ss = _dot((y * y).astype(BF16), gblk)
            y = y * lax.rsqrt(ss * (1.0 / ATT_HEAD_DIM) + RMS_EPS) * qkn_ref[which:which + 1, c:c + MXU_WIDTH]
        outs[g][which, 0, :, :, c:c + MXU_WIDTH] = y.astype(BF16).reshape(dil, tm // dil, MXU_WIDTH)

    pending = None
    for g in range(N_GROUPS):
        xg = xp_ref[g]
        for which in range(3):
            base = (which * N_GROUPS + g) * d
            for c in range(0, d, MXU_WIDTH):
                y = _dot(xg, w_ref[:, base + c:base + c + MXU_WIDTH])
                if pending is not None:
                    finish(*pending)
                pending = (g, which, c, y)
    finish(*pending)


def _qkv_perm(x, nw, w, qkn, gblk, b, t_len, tm):
    m, d = x.shape
    nt = t_len // tm
    out_shapes, out_specs = [], []
    for _, dil in GROUPS:
        out_shapes.append(jax.ShapeDtypeStruct((3, b, dil, t_len // dil, d), BF16))
        out_specs.append(pl.BlockSpec((3, 1, dil, tm // dil, d), lambda i: (0, i // nt, 0, i % nt, 0)))
    return pl.pallas_call(
        functools.partial(_qkv_perm_kernel, tm),
        out_shape=tuple(out_shapes),
        grid=(m // tm,),
        in_specs=[
            pl.BlockSpec((tm, d), lambda i: (i, 0)),
            pl.BlockSpec((1, d), lambda i: (0, 0)),
            pl.BlockSpec(w.shape, lambda i: (0, 0), pipeline_mode=pl.Buffered(1)),
            pl.BlockSpec(qkn.shape, lambda i: (0, 0)),
            pl.BlockSpec(gblk.shape, lambda i: (0, 0)),
        ],
        out_specs=tuple(out_specs),
        scratch_shapes=[pltpu.VMEM((d // LANES, tm, LANES), F32),
                        pltpu.VMEM((N_GROUPS, tm, d), BF16)],
        compiler_params=_params(("parallel",), VMEM_BIG),
        name="qkv_perm",
    )(x, nw, w, qkn, gblk)


def _qkv_plain_kernel(x_ref, nw_ref, w_ref, qkn_ref, gblk_ref, o_ref, xn_ref):
    j = pl.program_id(1)

    @pl.when(j == 0)
    def _():
        xn_ref[...] = _rms_rows(x_ref[...], nw_ref[...]).astype(BF16)

    y = _dot(xn_ref[...], w_ref[...])

    @pl.when(j < 2 * N_GROUPS)
    def _():
        gain = jnp.where(j < N_GROUPS, qkn_ref[0:1, :], qkn_ref[1:2, :])
        o_ref[...] = _qk_norm_rows(y, gain, gblk_ref[...])

    @pl.when(j >= 2 * N_GROUPS)
    def _():
        o_ref[...] = y


def _qkv_plain(x, nw, w, qkn, gblk, tm):
    m, d = x.shape
    n = w.shape[1]
    return pl.pallas_call(
        _qkv_plain_kernel,
        out_shape=jax.ShapeDtypeStruct((m, n), F32),
        grid=(m // tm, n // d),
        in_specs=[
            pl.BlockSpec((tm, d), lambda i, j: (i, 0)),
            pl.BlockSpec((1, d), lambda i, j: (0, 0)),
            pl.BlockSpec((d, d), lambda i, j: (0, j)),
            pl.BlockSpec(qkn.shape, lambda i, j: (0, 0)),
            pl.BlockSpec(gblk.shape, lambda i, j: (0, 0)),
        ],
        out_specs=pl.BlockSpec((tm, d), lambda i, j: (i, j)),
        scratch_shapes=[pltpu.VMEM((tm, d), BF16)],
        compiler_params=_params(("parallel", "arbitrary"), VMEM_SMALL),
        name="qkv_plain",
    )(x, nw, w, qkn, gblk)


def _kv_tail_kernel(x_ref, nw_ref, wt_ref, kn_ref, o_ref, xn_ref):
    xn_ref[...] = _rms_rows(x_ref[...], nw_ref[...]).astype(BF16)
    tt = x_ref.shape[0]
    slab = min(tt, MXU_WIDTH)
    for kv in range(2):
        for c in range(0, tt, slab):
            yt = _dot_nt(wt_ref[kv], xn_ref[c:c + slab, :])
            if kv == 0:
                y3 = yt.reshape(ATT_HEADS, ATT_HEAD_DIM, slab)
                ms = jnp.mean(y3 * y3, axis=1, keepdims=True)
                yt = (y3 * lax.rsqrt(ms + RMS_EPS)).reshape(yt.shape) * kn_ref[...]
            o_ref[0, kv, :, c:c + slab] = yt


def _kv_tail(x, nw, wt, kn_col, b, t_len, keep, tt):
    m, d = x.shape
    first = (t_len - keep) // tt
    per_seq = t_len // tt
    return pl.pallas_call(
        _kv_tail_kernel,
        out_shape=jax.ShapeDtypeStruct((b, 2, d, keep), F32),
        grid=(b, keep // tt),
        in_specs=[
            pl.BlockSpec((tt, d), lambda bi, ti: (bi * per_seq + first + ti, 0)),
            pl.BlockSpec((1, d), lambda bi, ti: (0, 0)),
            pl.BlockSpec((2, d, d), lambda bi, ti: (0, 0, 0)),
            pl.BlockSpec((d, 1), lambda bi, ti: (0, 0)),
        ],
        out_specs=pl.BlockSpec((1, 2, d, tt), lambda bi, ti: (bi, 0, 0, ti)),
        scratch_shapes=[pltpu.VMEM((tt, d), BF16)],
        compiler_params=_params(("parallel", "parallel"), VMEM_SMALL),
        name="kv_tail",
    )(x, nw, wt, kn_col)


def _attn_prompt_kernel(rb, qb, q_ref, kc_ref, kp_ref, vc_ref, vp_ref, bias_ref, o_ref, m_ref, l_ref):
    i = pl.program_id(2)
    first = jnp.where(i == 0, 0, 1)
    lane = lax.broadcasted_iota(jnp.int32, (N_BACK, LANES), 1)
    lo_half = lane < ATT_HEAD_DIM
    npair = ATT_HEADS // 2
    stats = {}

    def scores(r, u, pr):
        rows = slice(u * N_BACK, (u + 1) * N_BACK)
        sl = slice(pr * LANES, (pr + 1) * LANES)
        qp = q_ref[0, 0, r, rows, sl]
        kprev = kp_ref[0, 0, r, :, sl] if u == 0 else kc_ref[0, 0, r, (u - 1) * N_BACK:u * N_BACK, sl]
        kk = jnp.concatenate([kprev, kc_ref[0, 0, r, rows, sl]], axis=0)
        zq = jnp.zeros_like(qp)
        q2 = jnp.concatenate([jnp.where(lo_half, qp, zq), jnp.where(lo_half, zq, qp)], axis=0)
        return _dot_nt(q2, kk)

    def finish(r, u, pr, s):
        rows = slice(u * N_BACK, (u + 1) * N_BACK)
        sl = slice(pr * LANES, (pr + 1) * LANES)
        vprev = vp_ref[0, 0, r, :, sl] if u == 0 else vc_ref[0, 0, r, (u - 1) * N_BACK:u * N_BACK, sl]
        vv = jnp.concatenate([vprev, vc_ref[0, 0, r, rows, sl]], axis=0)
        s = s + (bias_ref[first, pr] if u == 0 else bias_ref[1, pr])
        mx = jnp.max(s, axis=-1, keepdims=True)
        p = jnp.exp2(s - mx)
        l = jnp.sum(p, axis=-1, keepdims=True)
        o2 = _dot(p.astype(BF16), vv)
        o_ref[0, r, rows, sl] = jnp.where(lo_half, o2[:N_BACK], o2[N_BACK:]).astype(BF16)
        m_acc, l_acc = stats.get((r, u), (jnp.zeros((N_BACK, LANES), F32), jnp.ones((N_BACK, LANES), F32)))
        m_acc = jnp.where(lane == 2 * pr, mx[:N_BACK], jnp.where(lane == 2 * pr + 1, mx[N_BACK:], m_acc))
        l_acc = jnp.where(lane == 2 * pr, l[:N_BACK], jnp.where(lane == 2 * pr + 1, l[N_BACK:], l_acc))
        stats[(r, u)] = (m_acc, l_acc)
        if pr == npair - 1:
            m_ref[0, r, rows, :] = m_acc
            l_ref[0, r, rows, :] = l_acc

    pending = None
    for r in range(rb):
        for u in range(qb):
            for pr in range(npair):
                s = scores(r, u, pr)
                if pending is not None:
                    finish(*pending)
                pending = (r, u, pr, s)
    finish(*pending)


def _attn_prompt(qkv_g, bias, rb, qb):
    _, b, dil, l, d = qkv_g.shape
    rows = qb * N_BACK

    def cur(which):
        return pl.BlockSpec((1, 1, rb, rows, d), lambda bi, r, i: (which, bi, r, i, 0))

    def prev(which):
        return pl.BlockSpec((1, 1, rb, N_BACK, d),
                            lambda bi, r, i: (which, bi, r, jnp.maximum(i * qb - 1, 0), 0))

    return pl.pallas_call(
        functools.partial(_attn_prompt_kernel, rb, qb),
        out_shape=(jax.ShapeDtypeStruct((b, dil, l, d), BF16),
                   jax.ShapeDtypeStruct((b, dil, l, LANES), F32),
                   jax.ShapeDtypeStruct((b, dil, l, LANES), F32)),
        grid=(b, dil // rb, l // rows),
        in_specs=[cur(0), cur(1), prev(1), cur(2), prev(2),
                  pl.BlockSpec(bias.shape, lambda bi, r, i: (0, 0, 0, 0))],
        out_specs=(pl.BlockSpec((1, rb, rows, d), lambda bi, r, i: (bi, r, i, 0)),
                   pl.BlockSpec((1, rb, rows, LANES), lambda bi, r, i: (bi, r, i, 0)),
                   pl.BlockSpec((1, rb, rows, LANES), lambda bi, r, i: (bi, r, i, 0))),
        compiler_params=_params(("parallel", "parallel", "arbitrary"), VMEM_SMALL),
        name="attn_prompt_d%d" % dil,
    )(qkv_g, qkv_g, qkv_g, qkv_g, qkv_g, bias)


def _merge_out_ffn_kernel(tm, o0_ref, o1_ref, o2_ref, m0_ref, m1_ref, m2_ref, l0_ref, l1_ref, l2_ref,
                          gexp_ref, x_ref, w_ref, nw_ref, wu_ref, wd_ref, y_ref,
                          os_ref, ms_ref, ls_ref, h_ref):
    d = x_ref.shape[-1]
    nslab = d // LANES

    def natural(o_ref, m_ref, l_ref, dil):
        if dil == 1:
            return o_ref[0, 0].astype(F32), m_ref[0, 0], l_ref[0, 0]
        n = tm // dil
        for r in range(dil):
            ms_ref[pl.ds(r, n, stride=dil), :] = m_ref[0, r]
            ls_ref[pl.ds(r, n, stride=dil), :] = l_ref[0, r]
            for c in range(nslab):
                os_ref.at[c][pl.ds(r, n, stride=dil), :] = o_ref[0, r, :, c * LANES:(c + 1) * LANES].astype(F32)
        return jnp.concatenate([os_ref[c] for c in range(nslab)], axis=1), ms_ref[...], ls_ref[...]

    trip = [natural(o_ref, m_ref, l_ref, GROUPS[g][1]) for g, (o_ref, m_ref, l_ref) in enumerate(
        ((o0_ref, m0_ref, l0_ref), (o1_ref, m1_ref, l1_ref), (o2_ref, m2_ref, l2_ref)))]
    mx = jnp.maximum(jnp.maximum(trip[0][1], trip[1][1]), trip[2][1])
    es = [jnp.exp2(t[1] - mx) for t in trip]
    inv = 1.0 / (es[0] * trip[0][2] + es[1] * trip[1][2] + es[2] * trip[2][2])
    acc = None
    for e, (o, _, _) in zip(es, trip):
        wt = e * inv
        hi = wt.astype(BF16)
        lo = (wt - hi.astype(F32)).astype(BF16)
        wx = _dot(jnp.concatenate([hi, lo], axis=1), gexp_ref[...])
        acc = wx * o if acc is None else acc + wx * o
    x = x_ref[...] + _dot(acc.astype(BF16), w_ref[...])
    _ffn_rows(x, nw_ref, wu_ref, wd_ref, y_ref, h_ref)


def _merge_out_ffn(outs, ms, ls, gexp, x, w, nw, wu, wd, t_len, tm):
    m, d = x.shape
    nt = t_len // tm
    row = pl.BlockSpec((tm, d), lambda i: (i, 0))
    o_specs, s_specs = [], []
    for _, dil in GROUPS:
        o_specs.append(pl.BlockSpec((1, dil, tm // dil, d), lambda i: (i // nt, 0, i % nt, 0)))
        s_specs.append(pl.BlockSpec((1, dil, tm // dil, LANES), lambda i: (i // nt, 0, i % nt, 0)))
    return pl.pallas_call(
        functools.partial(_merge_out_ffn_kernel, tm),
        out_shape=jax.ShapeDtypeStruct((m, d), F32),
        grid=(m // tm,),
        in_specs=o_specs + s_specs + s_specs + [_resident(gexp.shape), row, _resident(w.shape),
                                                pl.BlockSpec((1, d), lambda i: (0, 0)),
                                                _resident(wu.shape), _resident(wd.shape)],
        out_specs=row,
        scratch_shapes=[pltpu.VMEM((d // LANES, tm, LANES), F32), pltpu.VMEM((tm, LANES), F32),
                        pltpu.VMEM((tm, LANES), F32), pltpu.VMEM((tm, wu.shape[-1]), BF16)],
        compiler_params=_params(("parallel",), VMEM_BIG),
        name="merge_out_ffn",
    )(*outs, *ms, *ls, gexp, x, w, nw, wu, wd)


SAMPLE_PAIRS = 8


def _attn_sample_kernel(ts, *refs):
    ng = N_GROUPS
    q_refs, k_refs, v_refs = refs[0:ng], refs[ng:2 * ng], refs[2 * ng:3 * ng]
    c_refs = refs[3 * ng:4 * ng]
    bias_ref, o_ref = refs[4 * ng], refs[4 * ng + 1]
    npair = SAMPLE_PAIRS
    lane = lax.broadcasted_iota(jnp.int32, (ts, LANES), 1)
    lo_half = lane < ATT_HEAD_DIM
    zpad = jnp.zeros((LANES - npair * ts, LANES), F32)

    tk, vs = [], []
    for g in range(ng):
        kst = jnp.concatenate([k_refs[g][0, :, p * LANES:(p + 1) * LANES] for p in range(npair)] + [zpad], axis=0)
        vst = jnp.concatenate([v_refs[g][0, :, p * LANES:(p + 1) * LANES] for p in range(npair)] + [zpad], axis=0)
        tk.append(kst.T.astype(BF16))
        vs.append(vst.astype(BF16))

    rows = []
    for p in range(npair):
        sl = slice(p * LANES, (p + 1) * LANES)
        old, new = [], []
        for g in range(ng):
            qp = q_refs[g][0, :, sl]
            q2 = jnp.concatenate([jnp.where(lo_half, qp, 0.0), jnp.where(lo_half, 0.0, qp)], axis=0).astype(BF16)
            kt = c_refs[g][0, 0, 2 * p:2 * p + 2].reshape(LANES, c_refs[g].shape[-1]).astype(BF16)
            old.append(_dot(q2, kt))
            new.append(_dot(q2, tk[g]))
        rows.append(jnp.concatenate(old + new, axis=1))
    s = jnp.concatenate(rows, axis=0) + bias_ref[...].reshape(npair * 2 * ts, -1)
    mx = jnp.max(s, axis=-1, keepdims=True)
    pr = jnp.exp2(s - mx)
    inv = 1.0 / jnp.sum(pr, axis=-1, keepdims=True)
    prb = pr.astype(BF16)

    for p in range(npair):
        sl = slice(p * LANES, (p + 1) * LANES)
        r0 = p * 2 * ts
        acc = jnp.zeros((2 * ts, LANES), F32)
        off = 0
        for g in range(ng):
            window = c_refs[g].shape[-1]
            vt = c_refs[g][0, 1, 2 * p:2 * p + 2].reshape(LANES, window).astype(BF16)
            acc = acc + _dot_nt(prb[r0:r0 + 2 * ts, off:off + window], vt)
            off += window
        for g in range(ng):
            acc = acc + _dot(prb[r0:r0 + 2 * ts, off:off + LANES], vs[g])
            off += LANES
        om = acc * inv[r0:r0 + 2 * ts]
        o_ref[0, :, sl] = jnp.where(lo_half, om[:ts], om[ts:])


def _attn_sample(qkv32, caches_t, bias):
    b, ts, _ = qkv32.shape
    d = ATT_HEADS * ATT_HEAD_DIM
    width = SAMPLE_PAIRS * LANES
    per_blk = d // width
    hb = 2 * SAMPLE_PAIRS

    def new_rows(which, g):
        return pl.BlockSpec((1, ts, width), lambda bi, hq: (bi, 0, (which * N_GROUPS + g) * per_blk + hq))

    in_specs = [new_rows(w, g) for w in range(3) for g in range(N_GROUPS)]
    for c in caches_t:
        in_specs.append(pl.BlockSpec((1, 2, hb, ATT_HEAD_DIM, c.shape[-1]), lambda bi, hq: (bi, 0, hq, 0, 0)))
    in_specs.append(pl.BlockSpec((SAMPLE_PAIRS,) + bias.shape[1:], lambda bi, hq: (hq, 0, 0)))
    return pl.pallas_call(
        functools.partial(_attn_sample_kernel, ts),
        out_shape=jax.ShapeDtypeStruct((b, ts, d), F32),
        grid=(b, per_blk),
        in_specs=in_specs,
        out_specs=pl.BlockSpec((1, ts, width), lambda bi, hq: (bi, 0, hq)),
        compiler_params=_params(("parallel", "arbitrary"), VMEM_BIG),
        name="attn_sample",
    )(*([qkv32] * (3 * N_GROUPS)), *caches_t, bias)


def _matmul_res_kernel(a_ref, x_ref, w_ref, y_ref):
    y_ref[...] = x_ref[...] + _dot(a_ref[...].astype(BF16), w_ref[...])


def _matmul_res(a, x, w, tm):
    m, d = x.shape
    row = pl.BlockSpec((tm, d), lambda i: (i, 0))
    return pl.pallas_call(
        _matmul_res_kernel,
        out_shape=jax.ShapeDtypeStruct((m, d), F32),
        grid=(m // tm,),
        in_specs=[row, row, pl.BlockSpec((d, d), lambda i: (0, 0))],
        out_specs=row,
        compiler_params=_params(("parallel",), VMEM_SMALL),
        name="matmul_res",
    )(a, x, w)


def _alibi_slopes():
    n = N_GROUPS * ATT_HEADS
    return LOG2E * (2.0 ** (-8.0 * np.arange(1, n + 1) / n)).reshape(N_GROUPS, ATT_HEADS)


def _prompt_bias(g):
    slopes = _alibi_slopes()[g] * GROUPS[g][1]
    dist = np.arange(N_BACK)[:, None] - np.arange(2 * N_BACK)[None, :] + N_BACK
    band = (dist >= 0) & (dist <= N_BACK)
    has_prev = np.arange(2 * N_BACK)[None, :] >= N_BACK
    out = np.empty((2, ATT_HEADS, N_BACK, 2 * N_BACK), np.float32)
    for k, valid in enumerate((band & has_prev, band)):
        out[k] = np.where(valid[None], -slopes[:, None, None] * dist[None], NEG)
    return jnp.asarray(out.reshape(2, ATT_HEADS // 2, 2 * N_BACK, 2 * N_BACK))


def _sample_bias(ts):
    slopes = _alibi_slopes()
    npairs = ATT_HEADS // 2
    old, new = [], []
    for g, (window, dil) in enumerate(GROUPS):
        pos = np.arange(window)
        bc = np.full((npairs, 2 * ts, window), NEG, np.float32)
        bn = np.full((npairs, 2 * ts, LANES), NEG, np.float32)
        for h in range(ATT_HEADS):
            for s in range(ts):
                row = (h % 2) * ts + s
                back = window + s - pos
                valid = (back % dil == 0) & (back <= window)
                bc[h // 2, row] = np.where(valid, -slopes[g, h] * back, NEG)
                lane0 = ((h // 2) % SAMPLE_PAIRS) * ts
                for s2 in range(s + 1):
                    if (s - s2) % dil == 0:
                        bn[h // 2, row, lane0 + s2] = -slopes[g, h] * (s - s2)
        old.append(bc)
        new.append(bn)
    return jnp.asarray(np.concatenate(old + new, axis=-1))


def _head_group_matrices():
    d = ATT_HEADS * ATT_HEAD_DIM
    i = np.arange(2 * LANES)
    gblk = (i[:, None] // ATT_HEAD_DIM == i[None, :] // ATT_HEAD_DIM).astype(np.float32)
    gexp = (np.arange(LANES)[:, None] == np.arange(d)[None, :] // ATT_HEAD_DIM).astype(np.float32)
    return jnp.asarray(gblk, BF16), jnp.asarray(np.concatenate([gexp, gexp], axis=0), BF16)


def _tile(m, want):
    return want if m % want == 0 else m


def kernel(x_prompt, x_sample, state_hgrn, cache_kv_w128, cache_kv_w512, cache_kv_w2048, hg_lb_logits, hg_w_q, hg_w_f, hg_w_i, hg_w_g, hg_w_o, hg_norm_o, att_w_qkv, att_w_o, att_q_norm, att_k_norm, norm_mix, norm_ffn, ffn_w_up, ffn_w_down):
    b, t_len, d = x_prompt.shape
    bs, ts, _ = x_sample.shape
    depth = norm_mix.shape[0]
    caches_all = (cache_kv_w128, cache_kv_w512, cache_kv_w2048)
    gblk, gexp = _head_group_matrices()
    sample_bias = _sample_bias(ts)

    yp = x_prompt.reshape(b * t_len, d)
    ys = x_sample.reshape(bs * ts, d)
    tm_p = _tile(b * t_len, ROW_TILE)
    tm_s = _tile(bs * ts, SAMPLE_ROW_TILE)
    hg_p, hg_s = [], []
    kv_p = [[] for _ in GROUPS]
    kv_s = [[] for _ in GROUPS]
    for layer in range(depth):
        a = layer // 2
        nw = norm_mix[layer][None]
        nf = norm_ffn[layer][None]
        wu = ffn_w_up[layer].astype(BF16)
        wd = ffn_w_down[layer].astype(BF16)
        if layer % 2 == 0:
            w4 = jnp.stack([hg_w_q[a], hg_w_i[a], hg_w_g[a], hg_w_f[a]]).astype(BF16)
            wo = hg_w_o[a].astype(BF16)
            gn = hg_norm_o[a][None]
            zeros = jnp.zeros((b,) + state_hgrn.shape[2:], F32)
            f, qig = _hgrn_proj(yp, nw, hg_lb_logits, w4, layer, _tile(b * t_len, PROJ_ROW_TILE), BF16)
            o, sp = _hgrn_rec(f, qig, zeros, b, t_len, _tile(t_len, ROW_TILE), HG_CHUNK)
            yp = _hgrn_out_ffn(o, qig, gn, yp, wo, nf, wu, wd, tm_p)
            f, qig = _hgrn_proj(ys, nw, hg_lb_logits, w4, layer, tm_s, F32)
            o, ss = _hgrn_rec(f, qig, state_hgrn[a], bs, ts, ts, HG_SUBCHUNK)
            ys = _hgrn_out(o, qig, gn, ys, wo, tm_s)
            hg_p.append(sp)
            hg_s.append(ss)
        else:
            wqkv = att_w_qkv[a].astype(BF16)
            wo = att_w_o[a].astype(BF16)
            qn_row = jnp.tile(att_q_norm[a], ATT_HEADS)
            kn_row = jnp.tile(att_k_norm[a], ATT_HEADS)
            qkn = jnp.stack([qn_row * (ATT_SCALE * LOG2E), kn_row])
            qkv_groups = _qkv_perm(yp, nw, wqkv, qkn, gblk, b, t_len, tm_p)
            outs, ms, ls = [], [], []
            for g, (window, dil) in enumerate(GROUPS):
                qb = max(q for q in (1, 2, 4, 8) if q <= ATTN_STEP_BLOCKS and (t_len // dil) % (q * N_BACK) == 0)
                rb = max(r for r in (1, 2, 4, 8) if r * qb <= ATTN_STEP_BLOCKS and dil % r == 0)
                o, mrow, lrow = _attn_prompt(qkv_groups[g], _prompt_bias(g), rb, qb)
                outs.append(o)
                ms.append(mrow)
                ls.append(lrow)
                keep = min(window, t_len)
                wt = jnp.stack([wqkv[:, (N_GROUPS + g) * d:(N_GROUPS + g + 1) * d].T,
                                wqkv[:, (2 * N_GROUPS + g) * d:(2 * N_GROUPS + g + 1) * d].T])
                kvt = _kv_tail(yp, nw, wt, kn_row[:, None], b, t_len, keep, _tile(keep, ROW_TILE))
                kvt = kvt.reshape(b, 2, ATT_HEADS, ATT_HEAD_DIM, keep)
                kv_p[g].append(jnp.transpose(kvt, (0, 4, 1, 2, 3)))
            yp = _merge_out_ffn(outs, ms, ls, gexp, yp, wo, nf, wu, wd, t_len, tm_p)
            qkv32 = _qkv_plain(ys, nw, wqkv, qkn, gblk, tm_s).reshape(bs, ts, -1)
            caches_t = [jnp.transpose(c[a], (0, 2, 3, 4, 1)) for c in caches_all]
            om = _attn_sample(qkv32, caches_t, sample_bias)
            ys = _matmul_res(om.reshape(bs * ts, d), ys, wo, tm_s)
            kv = qkv32.reshape(bs, ts, 3, N_GROUPS, ATT_HEADS, ATT_HEAD_DIM)
            for g in range(N_GROUPS):
                kv_s[g].append(kv[:, :, 1:, g])
        ys = _ffn(ys, nf, wu, wd, tm_s, 1024)
    return (yp.reshape(b, t_len, d), ys.reshape(bs, ts, d), jnp.stack(hg_p), jnp.stack(hg_s),
            jnp.stack(kv_p[0]), jnp.stack(kv_s[0]),
            jnp.stack(kv_p[1]), jnp.stack(kv_s[1]),
            jnp.stack(kv_p[2]), jnp.stack(kv_s[2]))
```
